```python
import math
import jax
import jax.numpy as jnp
from jax import lax
import numpy as np

D_MODEL = 1024
BATCH = 8
SEQ = 2048
DEPTH = 2

HEAD_DIM = 64
RWKV_HEADS = 4
RWKV_N = 64
RWKV_W = RWKV_HEADS * RWKV_N
RWKV_DECAY_RANK = 64
RWKV_AAA_RANK = 64
RWKV_GATE_RANK = 128
RWKV_LN_EPS = 64e-5
MLSTM_HEADS = 4
MLSTM_DV = 64
MLSTM_DK = 32
MLSTM_W = MLSTM_HEADS * MLSTM_DV
MLSTM_CONV = 4
MLSTM_CHUNK = 64
GATE_SOFTCAP = 15.0
SWA_Q_HEADS = 4
SWA_KV_HEADS = 2
SWA_WINDOW = 128
SWA_W = SWA_Q_HEADS * HEAD_DIM
SWA_KV_W = SWA_KV_HEADS * HEAD_DIM
FOX_HEADS = 4
FOX_W = FOX_HEADS * HEAD_DIM
ATTN_BLOCK = 128
REL_BUCKETS = 32
REL_MAX_DIST = 128
D_FF = 2816
FFN_CONV = 3
NORM_EPS = 1e-6

RWKV_SPLITS = (RWKV_W, RWKV_W, RWKV_W, RWKV_DECAY_RANK, RWKV_AAA_RANK, RWKV_GATE_RANK)
MLSTM_SPLITS = (2 * MLSTM_HEADS * MLSTM_DK, MLSTM_W, MLSTM_W, MLSTM_HEADS, MLSTM_HEADS)
SWA_SPLITS = (SWA_W, SWA_KV_W, SWA_KV_W)
FOX_SPLITS = (FOX_W, FOX_W, FOX_W, FOX_HEADS)
GROUP_SPLITS = (sum(RWKV_SPLITS), sum(MLSTM_SPLITS), sum(SWA_SPLITS), sum(FOX_SPLITS))
N_IN = sum(GROUP_SPLITS)
D_MIX = RWKV_W + MLSTM_W + SWA_W + FOX_W

kernel_name = 'hybrid_parallel_heads_rwkv7_mlstm_swa_fox'


def _split(z, sizes):
    idx = np.cumsum(sizes)[:-1].tolist()
    return jnp.split(z, idx, axis=-1)


def _rmsnorm(x, g):
    xf = x.astype(jnp.float32)
    return xf * lax.rsqrt(jnp.mean(xf * xf, axis=-1, keepdims=True) + NORM_EPS) * g.astype(jnp.float32)


def _token_shift(u):
    return jnp.pad(u, ((0, 0), (1, 0), (0, 0)))[:, :-1]


def _causal_dwconv(u, w, b):
    k = w.shape[0]
    y = lax.conv_general_dilated(u, w[:, None, :].astype(u.dtype), window_strides=(1,),
                                 padding=[(k - 1, 0)], dimension_numbers=('NWC', 'WIO', 'NWC'),
                                 feature_group_count=u.shape[-1])
    return y + b.astype(u.dtype)


def _t5_bucket(dist):
    max_exact = REL_BUCKETS // 2
    d = np.maximum(dist, 1).astype(np.float32)
    large = max_exact + (np.log(d / max_exact) / math.log(REL_MAX_DIST / max_exact)
                         * (REL_BUCKETS - max_exact)).astype(np.int32)
    large = np.minimum(large, REL_BUCKETS - 1)
    return np.where(dist < max_exact, dist, large).astype(np.int32)


def _rwkv7(z, mu, w0, w_up, a0, a_up, g_up, k_k, k_a, r_k, ln_w, ln_b):
    z = z.astype(jnp.float32)
    z = z + mu * (_token_shift(z) - z)
    r, k, v, wd, ad, gd = _split(z, RWKV_SPLITS)
    bsz, t = r.shape[0], r.shape[1]
    log_decay = -jnp.exp(-jax.nn.softplus(-(w0 + jnp.tanh(wd) @ w_up)) - 0.5)
    decay = jnp.exp(log_decay)
    a = jax.nn.sigmoid(a0 + ad @ a_up)
    g = jax.nn.sigmoid(gd) @ g_up
    heads = lambda u: u.reshape(bsz, t, RWKV_HEADS, RWKV_N)
    kk = heads(k * k_k)
    kk = kk / jnp.maximum(jnp.sqrt(jnp.sum(kk * kk, axis=-1, keepdims=True)), 1e-12)
    k = k * (1.0 + (a - 1.0) * k_a)
    r, k, v, decay, a = heads(r), heads(k), heads(v), heads(decay), heads(a)

    def step(s, inp):
        r_t, w_t, k_t, v_t, kk_t, a_t = inp
        sa = jnp.einsum('bhvk,bhk->bhv', s, -kk_t)
        s = (s * w_t[:, :, None, :] + sa[..., None] * (kk_t * a_t)[:, :, None, :]
             + v_t[..., None] * k_t[:, :, None, :])
        return s, jnp.einsum('bhvk,bhk->bhv', s, r_t)

    xs = tuple(jnp.moveaxis(u, 1, 0) for u in (r, decay, k, v, kk, a))
    s0 = jnp.zeros((bsz, RWKV_HEADS, RWKV_N, RWKV_N), jnp.float32)
    _, y = lax.scan(step, s0, xs)
    y = jnp.moveaxis(y, 0, 1)
    mean = jnp.mean(y, axis=-1, keepdims=True)
    var = jnp.mean(jnp.square(y - mean), axis=-1, keepdims=True)
    y = ((y - mean) * lax.rsqrt(var + RWKV_LN_EPS)).reshape(bsz, t, RWKV_W) * ln_w + ln_b
    bonus = jnp.sum(r * k * r_k, axis=-1, keepdims=True) * v
    return (y + bonus.reshape(bsz, t, RWKV_W)) * g


def _mlstm(z, conv_w, conv_b, b_i, b_f, norm_g):
    z = z.astype(jnp.float32)
    qk, v, o, i_pre, f_pre = _split(z, MLSTM_SPLITS)
    qk = jax.nn.silu(_causal_dwconv(qk, conv_w, conv_b))
    q, k = jnp.split(qk, 2, axis=-1)
    cap = lambda u: GATE_SOFTCAP * jnp.tanh(u / GATE_SOFTCAP)
    li = cap(i_pre + b_i)
    lf = jax.nn.log_sigmoid(cap(f_pre + b_f))
    bsz, t = z.shape[0], z.shape[1]
    nc, L = t // MLSTM_CHUNK, MLSTM_CHUNK

    def chunks(u, d):
        return u.reshape(bsz, nc, L, MLSTM_HEADS, d).transpose(1, 0, 3, 2, 4)

    def gchunks(u):
        return u.reshape(bsz, nc, L, MLSTM_HEADS).transpose(1, 0, 3, 2)

    causal = np.tril(np.ones((L, L), dtype=bool))

    def step(carry, inp):
        c_st, n_st, m_st = carry
        qc, kc, vc, lic, lfc = inp
        b = jnp.cumsum(lfc, axis=-1)
        dmat = jnp.where(causal, b[..., :, None] - b[..., None, :] + lic[..., None, :], -jnp.inf)
        m_inter = b + m_st[..., None]
        m_t = jnp.maximum(m_inter, jnp.max(dmat, axis=-1))
        inter = jnp.exp(m_inter - m_t)
        s = jnp.einsum('bhtk,bhsk->bhts', qc, kc) * jnp.exp(dmat - m_t[..., None])
        num = jnp.einsum('bhts,bhsv->bhtv', s, vc) + inter[..., None] * jnp.einsum('bhvk,bhtk->bhtv', c_st, qc)
        den = jnp.sum(s, axis=-1) + inter * jnp.einsum('bhk,bhtk->bht', n_st, qc)
        h = num / jnp.maximum(jnp.abs(den), jnp.exp(-m_t))[..., None]
        b_last = b[..., -1]
        gexp = b_last[..., None] - b + lic
        m_new = jnp.maximum(b_last + m_st, jnp.max(gexp, axis=-1))
        wts = jnp.exp(gexp - m_new[..., None])
        dec = jnp.exp(b_last + m_st - m_new)
        c_new = dec[..., None, None] * c_st + jnp.einsum('bhs,bhsv,bhsk->bhvk', wts, vc, kc)
        n_new = dec[..., None] * n_st + jnp.einsum('bhs,bhsk->bhk', wts, kc)
        return (c_new, n_new, m_new), h

    xs = (chunks(q * MLSTM_DK ** -0.5, MLSTM_DK), chunks(k, MLSTM_DK), chunks(v, MLSTM_DV), gchunks(li), gchunks(lf))
    init = (jnp.zeros((bsz, MLSTM_HEADS, MLSTM_DV, MLSTM_DK), jnp.float32),
            jnp.zeros((bsz, MLSTM_HEADS, MLSTM_DK), jnp.float32),
            jnp.zeros((bsz, MLSTM_HEADS), jnp.float32))
    _, h = lax.scan(step, init, xs)
    h = h.transpose(1, 0, 3, 2, 4).reshape(bsz, t, MLSTM_HEADS, MLSTM_DV)
    h = h * lax.rsqrt(jnp.mean(h * h, axis=-1, keepdims=True) + NORM_EPS)
    return h.reshape(bsz, t, MLSTM_W) * norm_g * jax.nn.sigmoid(o)


def _swa(z, sinks, rel_bias):
    z = z.astype(jnp.float32)
    q, k, v = _split(z, SWA_SPLITS)
    bsz, t = z.shape[0], z.shape[1]
    nb, blk, grp = t // ATTN_BLOCK, ATTN_BLOCK, SWA_Q_HEADS // SWA_KV_HEADS
    qb = q.reshape(bsz, nb, blk, SWA_KV_HEADS, grp, HEAD_DIM)

    def kv_band(u):
        ub = u.reshape(bsz, nb, blk, SWA_KV_HEADS, HEAD_DIM)
        prev = jnp.pad(ub, ((0, 0), (1, 0), (0, 0), (0, 0), (0, 0)))[:, :-1]
        return jnp.concatenate([prev, ub], axis=2)

    kw, vw = kv_band(k), kv_band(v)
    logits = jnp.einsum('bnqhgd,bnkhd->bnhgqk', qb, kw) * HEAD_DIM ** -0.5
    tq = np.arange(blk)[:, None]
    sk = np.arange(2 * blk)[None, :]
    dist = tq + blk - sk
    in_window = (dist >= 0) & (dist < SWA_WINDOW)
    key_pos = np.arange(nb)[:, None, None] * blk - blk + sk[None]
    mask = in_window[None] & (key_pos >= 0)
    bias = rel_bias[_t5_bucket(np.clip(dist, 0, SWA_WINDOW - 1))].astype(jnp.float32)
    bias = jnp.transpose(bias, (2, 0, 1)).reshape(SWA_KV_HEADS, grp, blk, 2 * blk)
    logits = jnp.where(mask[None, :, None, None], logits + bias, -jnp.inf)
    sink = sinks.astype(jnp.float32).reshape(SWA_KV_HEADS, grp)[None, None, :, :, None, None]
    m = jnp.maximum(jnp.max(logits, axis=-1, keepdims=True), sink)
    p = jnp.exp(logits - m)
    denom = jnp.sum(p, axis=-1, keepdims=True) + jnp.exp(sink - m)
    out = jnp.einsum('bnhgqk,bnkhd->bnqhgd', p / denom, vw)
    return out.reshape(bsz, t, SWA_W)


def _fox(z, b_f):
    z = z.astype(jnp.float32)
    q, k, v, f_pre = _split(z, FOX_SPLITS)
    bsz, t = z.shape[0], z.shape[1]
    nb, blk = t // ATTN_BLOCK, ATTN_BLOCK
    k = k.reshape(bsz, t, FOX_HEADS, HEAD_DIM)
    v = v.reshape(bsz, t, FOX_HEADS, HEAD_DIM)
    c = jnp.cumsum(jax.nn.log_sigmoid(f_pre + b_f), axis=1)
    c_keys = c.transpose(0, 2, 1)
    qb = (q * HEAD_DIM ** -0.5).reshape(bsz, nb, blk, FOX_HEADS, HEAD_DIM).transpose(1, 0, 2, 3, 4)
    cqb = c.reshape(bsz, nb, blk, FOX_HEADS).transpose(1, 0, 3, 2)
    qpos = jnp.arange(t).reshape(nb, blk)
    kpos = jnp.arange(t)

    def block(args):
        qi, cqi, pi = args
        s = jnp.einsum('bqhd,bkhd->bhqk', qi, k) + cqi[..., :, None] - c_keys[:, :, None, :]
        s = jnp.where(kpos[None, :] <= pi[:, None], s, -jnp.inf)
        return jnp.einsum('bhqk,bkhd->bqhd', jax.nn.softmax(s, axis=-1), v)

    out = lax.map(block, (qb, cqb, qpos))
    return out.transpose(1, 0, 2, 3, 4).reshape(bsz, t, FOX_W)


def setup_inputs(seed: int = 0) -> dict:
    key = jax.random.key(seed)
    ks = iter(jax.random.split(key, 40))
    L = DEPTH
    nrm = lambda shape, scale: jax.random.normal(next(ks), shape, jnp.float32) * scale
    uni = lambda shape, lo, hi: jax.random.uniform(next(ks), shape, jnp.float32, lo, hi)
    gain = lambda shape: 1.0 + nrm(shape, 0.05)
    return {
        'x': nrm((BATCH, SEQ, D_MODEL), 1.0),
        'w_in': nrm((L, D_MODEL, N_IN), D_MODEL ** -0.5),
        'w_out': nrm((L, D_MIX, D_MODEL), D_MIX ** -0.5),
        'norm_mix_pre': gain((L, D_MODEL)),
        'norm_mix_post': gain((L, D_MODEL)),
        'norm_ffn_pre': gain((L, D_MODEL)),
        'norm_ffn_post': gain((L, D_MODEL)),
        'rwkv_mu': uni((L, GROUP_SPLITS[0]), 0.0, 1.0),
        'rwkv_w0': uni((L, RWKV_W), -6.0, 1.0),
        'rwkv_w_up': nrm((L, RWKV_DECAY_RANK, RWKV_W), 0.5 * RWKV_DECAY_RANK ** -0.5),
        'rwkv_a0': nrm((L, RWKV_W), 0.5),
        'rwkv_a_up': nrm((L, RWKV_AAA_RANK, RWKV_W), 0.5 * RWKV_AAA_RANK ** -0.5),
        'rwkv_g_up': nrm((L, RWKV_GATE_RANK, RWKV_W), RWKV_GATE_RANK ** -0.5),
        'rwkv_k_k': 0.85 + nrm((L, RWKV_W), 0.05),
        'rwkv_k_a': gain((L, RWKV_W)),
        'rwkv_r_k': nrm((L, RWKV_HEADS, RWKV_N), 0.1),
        'rwkv_ln_w': gain((L, RWKV_W)),
        'rwkv_ln_b': nrm((L, RWKV_W), 0.02),
        'mlstm_conv_w': nrm((L, MLSTM_CONV, 2 * MLSTM_HEADS * MLSTM_DK), MLSTM_CONV ** -0.5),
        'mlstm_conv_b': nrm((L, 2 * MLSTM_HEADS * MLSTM_DK), 0.02),
        'mlstm_b_i': nrm((L, MLSTM_HEADS), 0.5),
        'mlstm_b_f': uni((L, MLSTM_HEADS), 3.0, 6.0),
        'mlstm_norm': gain((L, MLSTM_W)),
        'swa_sinks': nrm((L, SWA_Q_HEADS), 0.5),
        'fox_b_f': uni((L, FOX_HEADS), 2.0, 4.0),
        'rel_bias': nrm((REL_BUCKETS, SWA_Q_HEADS), 0.5),
        'ffn_w_up': nrm((L, D_MODEL, 2 * D_FF), D_MODEL ** -0.5),
        'ffn_conv_w': nrm((L, FFN_CONV, D_FF), FFN_CONV ** -0.5),
        'ffn_conv_b': nrm((L, D_FF), 0.02),
        'ffn_w_down': nrm((L, D_FF, D_MODEL), D_FF ** -0.5),
    }


def reference(x, w_in, w_out, norm_mix_pre, norm_mix_post, norm_ffn_pre, norm_ffn_post,
              rwkv_mu, rwkv_w0, rwkv_w_up, rwkv_a0, rwkv_a_up, rwkv_g_up, rwkv_k_k, rwkv_k_a,
              rwkv_r_k, rwkv_ln_w, rwkv_ln_b, mlstm_conv_w, mlstm_conv_b, mlstm_b_i, mlstm_b_f,
              mlstm_norm, swa_sinks, fox_b_f, rel_bias, ffn_w_up, ffn_conv_w, ffn_conv_b, ffn_w_down):
    dt = x.dtype
    for l in range(DEPTH):
        h = _rmsnorm(x, norm_mix_pre[l]).astype(dt)
        z = h @ w_in[l]
        z_a, z_b, z_c, z_d = _split(z, GROUP_SPLITS)
        y_a = _rwkv7(z_a, rwkv_mu[l], rwkv_w0[l], rwkv_w_up[l], rwkv_a0[l], rwkv_a_up[l], rwkv_g_up[l],
                     rwkv_k_k[l], rwkv_k_a[l], rwkv_r_k[l], rwkv_ln_w[l], rwkv_ln_b[l])
        y_b = _mlstm(z_b, mlstm_conv_w[l], mlstm_conv_b[l], mlstm_b_i[l], mlstm_b_f[l], mlstm_norm[l])
        y_c = _swa(z_c, swa_sinks[l], rel_bias)
        y_d = _fox(z_d, fox_b_f[l])
        y = jnp.concatenate([y_a, y_b, y_c, y_d], axis=-1).astype(dt)
        x = x + _rmsnorm(y @ w_out[l], norm_mix_post[l]).astype(dt)
        h = _rmsnorm(x, norm_ffn_pre[l]).astype(dt)
        gate, up = jnp.split(h @ ffn_w_up[l], 2, axis=-1)
        gate = _causal_dwconv(gate, ffn_conv_w[l], ffn_conv_b[l])
        f = jax.nn.gelu(gate, approximate=True) * up
        x = x + _rmsnorm(f @ ffn_w_down[l], norm_ffn_post[l]).astype(dt)
    return x
```

```python
import functools
import math

import jax
import jax.numpy as jnp
import numpy as np
from jax import lax
from jax.experimental import pallas as pl
from jax.experimental.pallas import tpu as pltpu

F32 = jnp.float32
BF16 = jnp.bfloat16

D_MODEL = 1024
HEAD_DIM = 64
N_HEADS = 4
MIX_W = N_HEADS * HEAD_DIM
RWKV_DECAY_RANK = 64
RWKV_AAA_RANK = 64
RWKV_GATE_RANK = 128
RWKV_LN_EPS = 64e-5
RWKV_CHUNK = 64
MLSTM_DK = 32
MLSTM_CONV = 4
MLSTM_CHUNK = 64
GATE_SOFTCAP = 15.0
SWA_KV_HEADS = 2
SWA_WINDOW = 128
ATTN_BLOCK = 128
REL_BUCKETS = 32
REL_MAX_DIST = 128
D_FF = 2816
FFN_CONV = 3
NORM_EPS = 1e-6

LANES = 128
SUBLANES = 8

ZA_W = 3 * MIX_W + RWKV_DECAY_RANK + RWKV_AAA_RANK + RWKV_GATE_RANK
ZB_W = 3 * MIX_W + LANES
ZC_W = MIX_W + 2 * SWA_KV_HEADS * HEAD_DIM
ZD_W = 3 * MIX_W + LANES
Z_W = ZA_W + ZB_W + ZC_W + ZD_W

HI = lax.Precision.HIGHEST


def _dot(a, b, precision=None):
    return jnp.dot(a, b, preferred_element_type=F32, precision=precision)


def _dot_nt(a, b, precision=None):
    return lax.dot_general(a, b, (((1,), (1,)), ((), ())), preferred_element_type=F32, precision=precision)


def _dot_tn(a, b, precision=None):
    return lax.dot_general(a, b, (((0,), (0,)), ((), ())), preferred_element_type=F32, precision=precision)


def _bdot(a, b):
    return _dot(a.astype(BF16), b.astype(BF16))


def _bdot_nt(a, b):
    return _dot_nt(a.astype(BF16), b.astype(BF16))


def _bdot_tn(a, b):
    return _dot_tn(a.astype(BF16), b.astype(BF16))


def _sigmoid(x):
    return 1.0 / (1.0 + jnp.exp(-x))


def _log_sigmoid(x):
    return jnp.minimum(x, 0.0) - jnp.log1p(jnp.exp(-jnp.abs(x)))


def _softplus(x):
    return jnp.maximum(x, 0.0) + jnp.log1p(jnp.exp(-jnp.abs(x)))


def _rms_scale(x):
    return lax.rsqrt(jnp.mean(x * x, axis=-1, keepdims=True) + NORM_EPS)


def _shift_rows(x, prev_tail, d):
    rolled = pltpu.roll(x, d, axis=0)
    head_rows = lax.broadcasted_iota(jnp.int32, (SUBLANES, 1), 0)
    head = jnp.where(head_rows < d, pltpu.roll(prev_tail, d, axis=0), rolled[:SUBLANES])
    return jnp.concatenate([head, rolled[SUBLANES:]], axis=0)


def _head_ones(width, head):
    r = lax.broadcasted_iota(jnp.int32, (width, width), 0) // head
    c = lax.broadcasted_iota(jnp.int32, (width, width), 1) // head
    return (r == c).astype(F32)


def _inproj_kernel(x_ref, g_ref, w_ref, za_ref, zb_ref, zc_ref, zd_ref):
    x = x_ref[...]
    h = (x * _rms_scale(x) * g_ref[...]).astype(BF16)
    off = 0
    for ref in (za_ref, zb_ref, zc_ref, zd_ref):
        w = ref.shape[1]
        ref[...] = _dot(h, w_ref[:, off:off + w])
        off += w


def _inproj(x2, g, w, tm):
    rows = x2.shape[0]
    return pl.pallas_call(
        _inproj_kernel,
        grid=(rows // tm,),
        in_specs=[
            pl.BlockSpec((tm, D_MODEL), lambda i: (i, 0)),
            pl.BlockSpec((1, D_MODEL), lambda i: (0, 0)),
            pl.BlockSpec((D_MODEL, Z_W), lambda i: (0, 0)),
        ],
        out_specs=[pl.BlockSpec((tm, w_), lambda i: (i, 0)) for w_ in (ZA_W, ZB_W, ZC_W, ZD_W)],
        out_shape=[jax.ShapeDtypeStruct((rows, w_), F32) for w_ in (ZA_W, ZB_W, ZC_W, ZD_W)],
        compiler_params=pltpu.CompilerParams(dimension_semantics=("arbitrary",)),
        name="inproj",
    )(x2, g, w)


def _rwkv_kernel(z_ref, mu_ref, w0_ref, wup_ref, a0_ref, aup_ref, gup_ref, kk_ref, ka_ref, rk_ref,
                 lnw_ref, lnb_ref, o_ref, tail_ref, st_ref, y_ref, *, tc):
    L = RWKV_CHUNK
    W = MIX_W

    @pl.when(pl.program_id(1) == 0)
    def _():
        tail_ref[...] = jnp.zeros_like(tail_ref)
        st_ref[...] = jnp.zeros_like(st_ref)

    z = z_ref[...]
    zz = z + mu_ref[...] * (_shift_rows(z, tail_ref[...], 1) - z)
    tail_ref[...] = z[tc - SUBLANES:, :]

    r = zz[:, 0:W]
    k = zz[:, W:2 * W]
    v = zz[:, 2 * W:3 * W]
    o1 = 3 * W
    wd = zz[:, o1:o1 + RWKV_DECAY_RANK]
    ad = zz[:, o1 + RWKV_DECAY_RANK:o1 + RWKV_DECAY_RANK + RWKV_AAA_RANK]
    gd = zz[:, o1 + RWKV_DECAY_RANK + RWKV_AAA_RANK:]

    lw = -jnp.exp(-_softplus(-(w0_ref[...] + _bdot(jnp.tanh(wd), wup_ref[...]))) - 0.5)
    alpha = _sigmoid(a0_ref[...] + _bdot(ad, aup_ref[...]))
    g = _bdot(_sigmoid(gd), gup_ref[...])

    hsum = _head_ones(W, HEAD_DIM)
    kk = k * kk_ref[...]
    kk = kk / jnp.maximum(jnp.sqrt(_dot(kk * kk, hsum, HI)), 1e-12)
    k = k * (1.0 + (alpha - 1.0) * ka_ref[...])

    rt = lax.broadcasted_iota(jnp.int32, (tc, tc), 0)
    ct = lax.broadcasted_iota(jnp.int32, (tc, tc), 1)
    tri = ((rt // L == ct // L) & (ct <= rt)).astype(F32)
    cum = _dot(tri, lw, HI)
    e_in = jnp.exp(cum)
    e_out = jnp.exp(-cum)
    r_t = r * e_in
    a_t = -kk * jnp.exp(cum - lw)
    b_t = kk * alpha * e_out
    k_t = k * e_out

    lane_head = lax.broadcasted_iota(jnp.int32, (1, W), 1) // HEAD_DIM

    def stack(xc):
        return jnp.concatenate([jnp.where(lane_head == h, xc, 0.0) for h in range(N_HEADS)], axis=0)

    n = N_HEADS * L
    rr = lax.broadcasted_iota(jnp.int32, (n, n), 0)
    cc = lax.broadcasted_iota(jnp.int32, (n, n), 1)
    strict = (rr % L) > (cc % L)
    incl = (rr % L) >= (cc % L)
    eye = (rr == cc).astype(F32)

    st = st_ref[...]
    for c in range(tc // L):
        sl = slice(c * L, (c + 1) * L)
        a_s, r_s, b_s, k_s, v_s = (stack(u[sl]) for u in (a_t, r_t, b_t, k_t, v))
        a_ab = jnp.where(strict, _dot_nt(a_s, b_s, HI), 0.0)
        a_ak = jnp.where(strict, _dot_nt(a_s, k_s, HI), 0.0)
        a_rb = jnp.where(incl, _dot_nt(r_s, b_s, HI), 0.0)
        a_rk = jnp.where(incl, _dot_nt(r_s, k_s, HI), 0.0)
        inv = eye + a_ab
        pw = a_ab
        for _ in range(int(math.log2(L)) - 1):
            pw = _dot(pw, pw, HI)
            inv = inv + _dot(inv, pw, HI)
        w_m = _dot(inv, a_s, HI)
        u2 = _dot(inv, _dot(a_ak, v_s, HI), HI)
        u = _dot_nt(w_m, st, HI) + u2
        y_s = _dot_nt(r_s, st, HI) + _dot(a_rb, u, HI) + _dot(a_rk, v_s, HI)
        st = (st + _dot_tn(u, b_s, HI) + _dot_tn(v_s, k_s, HI)) * e_in[(c + 1) * L - 1:(c + 1) * L, :]
        y_c = y_s[0:L]
        for h in range(1, N_HEADS):
            y_c = y_c + y_s[h * L:(h + 1) * L]
        y_ref[sl, :] = y_c
    st_ref[...] = st

    y = y_ref[...]
    inv_n = 1.0 / HEAD_DIM
    mean = _dot(y, hsum, HI) * inv_n
    yc = y - mean
    var = _dot(yc * yc, hsum, HI) * inv_n
    y = yc * lax.rsqrt(var + RWKV_LN_EPS) * lnw_ref[...] + lnb_ref[...]
    bonus = _dot(r * k * rk_ref[...], hsum, HI) * v
    o_ref[...] = (y + bonus) * g


def _rwkv(za, p, bsz, t, tc):
    nt = t // tc
    full = lambda a: pl.BlockSpec(a.shape, lambda b, i: (0,) * a.ndim)
    params = [p["mu"], p["w0"], p["w_up"], p["a0"], p["a_up"], p["g_up"], p["k_k"], p["k_a"], p["r_k"],
              p["ln_w"], p["ln_b"]]
    return pl.pallas_call(
        functools.partial(_rwkv_kernel, tc=tc),
        grid=(bsz, nt),
        in_specs=[pl.BlockSpec((tc, ZA_W), lambda b, i: (b * nt + i, 0))] + [full(a) for a in params],
        out_specs=pl.BlockSpec((tc, MIX_W), lambda b, i: (b * nt + i, 0)),
        out_shape=jax.ShapeDtypeStruct((bsz * t, MIX_W), F32),
        scratch_shapes=[
            pltpu.VMEM((SUBLANES, ZA_W), F32),
            pltpu.VMEM((MIX_W, MIX_W), F32),
            pltpu.VMEM((tc, MIX_W), F32),
        ],
        compiler_params=pltpu.CompilerParams(dimension_semantics=("arbitrary", "arbitrary")),
        name="rwkv7",
    )(za, *params)


def _mlstm_kernel(z_ref, cw_ref, cb_ref, gb_ref, ng_ref, o_ref, tail_ref, c_ref, n_ref, m_ref, h_ref, *, tc):
    L = MLSTM_CHUNK
    W = MIX_W
    DK = MLSTM_DK
    DV = HEAD_DIM

    @pl.when(pl.program_id(1) == 0)
    def _():
        tail_ref[...] = jnp.zeros_like(tail_ref)
        c_ref[...] = jnp.zeros_like(c_ref)
        n_ref[...] = jnp.zeros_like(n_ref)
        m_ref[...] = jnp.zeros_like(m_ref)

    qk_in = z_ref[:, 0:W]
    v = z_ref[:, W:2 * W]
    og = z_ref[:, 2 * W:3 * W]
    gates = z_ref[:, 3 * W:3 * W + LANES]

    tail = tail_ref[...]
    conv = cb_ref[...] + cw_ref[MLSTM_CONV - 1:MLSTM_CONV, :] * qk_in
    for d in range(1, MLSTM_CONV):
        conv = conv + cw_ref[MLSTM_CONV - 1 - d:MLSTM_CONV - d, :] * _shift_rows(qk_in, tail, d)
    tail_ref[...] = qk_in[tc - SUBLANES:, :]
    qk = conv * _sigmoid(conv)
    q = qk[:, 0:N_HEADS * DK] * (DK ** -0.5)
    k = qk[:, N_HEADS * DK:]

    capped = GATE_SOFTCAP * jnp.tanh((gates + gb_ref[...]) / GATE_SOFTCAP)
    li = capped
    lf = _log_sigmoid(capped)
    rt = lax.broadcasted_iota(jnp.int32, (tc, tc), 0)
    ct = lax.broadcasted_iota(jnp.int32, (tc, tc), 1)
    tri = ((rt // L == ct // L) & (ct <= rt)).astype(F32)
    bcum = _dot(tri, lf, HI)
    li_r = li.T
    bcum_r = bcum.T

    causal = lax.broadcasted_iota(jnp.int32, (L, L), 1) <= lax.broadcasted_iota(jnp.int32, (L, L), 0)

    for c in range(tc // L):
        sl = slice(c * L, (c + 1) * L)
        for h in range(N_HEADS):
            qc = q[sl, h * DK:(h + 1) * DK]
            kc = k[sl, h * DK:(h + 1) * DK]
            vc = v[sl, h * DV:(h + 1) * DV]
            b_col = bcum[sl, N_HEADS + h:N_HEADS + h + 1]
            li_col = li[sl, h:h + 1]
            b_row = bcum_r[N_HEADS + h:N_HEADS + h + 1, sl]
            li_row = li_r[h:h + 1, sl]
            c_st = c_ref[h]
            n_st = n_ref[h:h + 1, :]
            m_st = m_ref[h:h + 1, :]

            dmat = jnp.where(causal, b_col - b_row + li_row, -jnp.inf)
            m_inter = b_col + m_st
            m_t = jnp.maximum(m_inter, jnp.max(dmat, axis=-1, keepdims=True))
            inter = jnp.exp(m_inter - m_t)
            s = _bdot_nt(qc, kc) * jnp.exp(dmat - m_t)
            num = _bdot(s, vc) + inter * _bdot_nt(qc, c_st)
            den = jnp.sum(s, axis=-1, keepdims=True) + inter * jnp.sum(qc * n_st, axis=-1, keepdims=True)
            hh = num / jnp.maximum(jnp.abs(den), jnp.exp(-m_t))

            b_last = b_col[L - 1:L, :]
            m_new = jnp.maximum(b_last + m_st, jnp.max(b_last - b_row + li_row, axis=-1, keepdims=True))
            wts = jnp.exp(b_last - b_col + li_col - m_new)
            dec = jnp.exp(b_last + m_st - m_new)
            c_ref[h] = dec * c_st + _bdot_tn(wts * vc, kc)
            n_ref[h:h + 1, :] = dec * n_st + jnp.sum(wts * kc, axis=0, keepdims=True)
            m_ref[h:h + 1, :] = m_new
            h_ref[sl, h * DV:(h + 1) * DV] = hh

    hv = h_ref[...]
    ms = _dot(hv * hv, _head_ones(W, DV), HI) * (1.0 / DV)
    o_ref[...] = hv * lax.rsqrt(ms + NORM_EPS) * ng_ref[...] * _sigmoid(og)


def _mlstm(zb, p, bsz, t, tc):
    nt = t // tc
    full = lambda a: pl.BlockSpec(a.shape, lambda b, i: (0,) * a.ndim)
    params = [p["conv_w"], p["conv_b"], p["gate_b"], p["norm_g"]]
    return pl.pallas_call(
        functools.partial(_mlstm_kernel, tc=tc),
        grid=(bsz, nt),
        in_specs=[pl.BlockSpec((tc, ZB_W), lambda b, i: (b * nt + i, 0))] + [full(a) for a in params],
        out_specs=pl.BlockSpec((tc, MIX_W), lambda b, i: (b * nt + i, 0)),
        out_shape=jax.ShapeDtypeStruct((bsz * t, MIX_W), F32),
        scratch_shapes=[
            pltpu.VMEM((SUBLANES, MIX_W), F32),
            pltpu.VMEM((N_HEADS, HEAD_DIM, MLSTM_DK), F32),
            pltpu.VMEM((N_HEADS, MLSTM_DK), F32),
            pltpu.VMEM((N_HEADS, 1), F32),
            pltpu.VMEM((tc, MIX_W), F32),
        ],
        compiler_params=pltpu.CompilerParams(dimension_semantics=("arbitrary", "arbitrary")),
        name="mlstm",
    )(zb, *params)


def _t5_bucket(dist):
    max_exact = REL_BUCKETS // 2
    d = np.maximum(dist, 1).astype(np.float32)
    large = max_exact + (np.log(d / max_exact) / math.log(REL_MAX_DIST / max_exact)
                         * (REL_BUCKETS - max_exact)).astype(np.int32)
    large = np.minimum(large, REL_BUCKETS - 1)
    return np.where(dist < max_exact, dist, large).astype(np.int32)


def _swa_kernel(rb_ref, sink_ref, bucket_ref, q_ref, kp_ref, kc_ref, vp_ref, vc_ref, o_ref, bias_ref):
    blk = ATTN_BLOCK
    first = (pl.program_id(0) == 0) & (pl.program_id(1) == 0)

    @pl.when(first)
    def _():
        bucket = bucket_ref[...]
        for h in range(N_HEADS):
            acc = jnp.zeros((blk, 2 * blk), F32)
            for bk in range(REL_BUCKETS):
                acc = jnp.where(bucket == bk, rb_ref[bk, h], acc)
            bias_ref[h] = acc

    tq = lax.broadcasted_iota(jnp.int32, (blk, 2 * blk), 0)
    sk = lax.broadcasted_iota(jnp.int32, (blk, 2 * blk), 1)
    dist = tq + blk - sk
    mask = (dist >= 0) & (dist < SWA_WINDOW) & ((sk >= blk) | (pl.program_id(1) > 0))
    grp = N_HEADS // SWA_KV_HEADS
    for h in range(N_HEADS):
        kvh = h // grp
        hs = slice(h * HEAD_DIM, (h + 1) * HEAD_DIM)
        ks = slice(kvh * HEAD_DIM, (kvh + 1) * HEAD_DIM)
        kw = jnp.concatenate([kp_ref[:, ks], kc_ref[:, ks]], axis=0)
        vw = jnp.concatenate([vp_ref[:, ks], vc_ref[:, ks]], axis=0)
        logits = _bdot_nt(q_ref[:, hs], kw) * (HEAD_DIM ** -0.5) + bias_ref[h]
        logits = jnp.where(mask, logits, -jnp.inf)
        sink = sink_ref[h]
        m = jnp.maximum(jnp.max(logits, axis=-1, keepdims=True), sink)
        p = jnp.exp(logits - m)
        denom = jnp.sum(p, axis=-1, keepdims=True) + jnp.exp(sink - m)
        o_ref[:, hs] = _bdot(p / denom, vw)


def _swa(zc, rel_bias, sinks, bsz, t):
    blk = ATTN_BLOCK
    nb = t // blk
    dist = np.arange(blk)[:, None] + blk - np.arange(2 * blk)[None, :]
    bucket = jnp.asarray(_t5_bucket(np.clip(dist, 0, SWA_WINDOW - 1)))
    kvw = SWA_KV_HEADS * HEAD_DIM
    kcol = MIX_W // kvw
    cur = lambda b, i: b * nb + i
    prev = lambda b, i: b * nb + jnp.maximum(i - 1, 0)
    return pl.pallas_call(
        _swa_kernel,
        grid=(bsz, nb),
        in_specs=[
            pl.BlockSpec(memory_space=pltpu.SMEM),
            pl.BlockSpec(memory_space=pltpu.SMEM),
            pl.BlockSpec((blk, 2 * blk), lambda b, i: (0, 0)),
            pl.BlockSpec((blk, MIX_W), lambda b, i: (cur(b, i), 0)),
            pl.BlockSpec((blk, kvw), lambda b, i: (prev(b, i), kcol)),
            pl.BlockSpec((blk, kvw), lambda b, i: (cur(b, i), kcol)),
            pl.BlockSpec((blk, kvw), lambda b, i: (prev(b, i), kcol + 1)),
            pl.BlockSpec((blk, kvw), lambda b, i: (cur(b, i), kcol + 1)),
        ],
        out_specs=pl.BlockSpec((blk, MIX_W), lambda b, i: (cur(b, i), 0)),
        out_shape=jax.ShapeDtypeStruct((bsz * t, MIX_W), F32),
        scratch_shapes=[pltpu.VMEM((N_HEADS, blk, 2 * blk), F32)],
        compiler_params=pltpu.CompilerParams(dimension_semantics=("arbitrary", "arbitrary")),
        name="swa",
    )(rel_bias, sinks, bucket, zc, zc, zc, zc, zc)


def _fox_kernel(q_ref, k_ref, v_ref, f_ref, fb_ref, o_ref, crow_ref, clast_ref):
    blk = ATTN_BLOCK
    i = pl.program_id(1)

    @pl.when(i == 0)
    def _():
        clast_ref[...] = jnp.zeros_like(clast_ref)

    ls = _log_sigmoid(f_ref[...] + fb_ref[...])
    tri = (lax.broadcasted_iota(jnp.int32, (blk, blk), 1)
           <= lax.broadcasted_iota(jnp.int32, (blk, blk), 0)).astype(F32)
    cq = _dot(tri, ls, HI) + clast_ref[0:1, :]
    clast_ref[0:1, :] = cq[blk - 1:blk, :]
    crow_ref[:, pl.ds(pl.multiple_of(i * blk, blk), blk)] = cq.T[0:SUBLANES, :]

    row = lax.broadcasted_iota(jnp.int32, (blk, blk), 0)
    col = lax.broadcasted_iota(jnp.int32, (blk, blk), 1)
    qs = [q_ref[:, h * HEAD_DIM:(h + 1) * HEAD_DIM] * (HEAD_DIM ** -0.5) for h in range(N_HEADS)]

    def body(j, carry):
        off = pl.multiple_of(j * blk, blk)
        keep = (col + j * blk) <= (row + i * blk)
        new = []
        for h in range(N_HEADS):
            m, l, acc = carry[h]
            hs = slice(h * HEAD_DIM, (h + 1) * HEAD_DIM)
            kj = k_ref[pl.ds(off, blk), hs]
            vj = v_ref[pl.ds(off, blk), hs]
            s = _bdot_nt(qs[h], kj) + cq[:, h:h + 1] - crow_ref[h:h + 1, pl.ds(off, blk)]
            s = jnp.where(keep, s, -jnp.inf)
            m2 = jnp.maximum(m, jnp.max(s, axis=-1, keepdims=True))
            p = jnp.exp(s - m2)
            scale = jnp.exp(m - m2)
            new.append((m2, scale * l + jnp.sum(p, axis=-1, keepdims=True), scale * acc + _bdot(p, vj)))
        return tuple(new)

    init = tuple((jnp.full((blk, 1), -jnp.inf, F32), jnp.zeros((blk, 1), F32), jnp.zeros((blk, HEAD_DIM), F32))
                 for _ in range(N_HEADS))
    res = body(i, init)
    res = lax.fori_loop(0, i, body, res)
    for h in range(N_HEADS):
        _, l, acc = res[h]
        o_ref[:, h * HEAD_DIM:(h + 1) * HEAD_DIM] = acc / l


def _fox(zd, fb, bsz, t):
    blk = ATTN_BLOCK
    nb = t // blk
    return pl.pallas_call(
        _fox_kernel,
        grid=(bsz, nb),
        in_specs=[
            pl.BlockSpec((blk, MIX_W), lambda b, i: (b * nb + i, 0)),
            pl.BlockSpec((t, MIX_W), lambda b, i: (b, 1)),
            pl.BlockSpec((t, MIX_W), lambda b, i: (b, 2)),
            pl.BlockSpec((blk, LANES), lambda b, i: (b * nb + i, 3 * MIX_W // LANES)),
            pl.BlockSpec((1, LANES), lambda b, i: (0, 0)),
        ],
        out_specs=pl.BlockSpec((blk, MIX_W), lambda b, i: (b * nb + i, 0)),
        out_shape=jax.ShapeDtypeStruct((bsz * t, MIX_W), F32),
        scratch_shapes=[
            pltpu.VMEM((SUBLANES, t), F32),
            pltpu.VMEM((SUBLANES, LANES), F32),
        ],
        compiler_params=pltpu.CompilerParams(dimension_semantics=("arbitrary", "arbitrary")),
        name="fox",
    )(zd, zd, zd, zd, fb)


def _outproj_kernel(x_ref, ya_ref, yb_ref, yc_ref, yd_ref, w_ref, g_ref, o_ref):
    acc = None
    for n, ref in enumerate((ya_ref, yb_ref, yc_ref, yd_ref)):
        part = _dot(ref[...].astype(BF16), w_ref[n * MIX_W:(n + 1) * MIX_W, :])
        acc = part if acc is None else acc + part
    o_ref[...] = x_ref[...] + acc * _rms_scale(acc) * g_ref[...]


def _outproj(x2, ys, w, g, tm):
    rows = x2.shape[0]
    return pl.pallas_call(
        _outproj_kernel,
        grid=(rows // tm,),
        in_specs=[pl.BlockSpec((tm, D_MODEL), lambda i: (i, 0))]
        + [pl.BlockSpec((tm, MIX_W), lambda i: (i, 0))] * 4
        + [pl.BlockSpec((N_HEADS * MIX_W, D_MODEL), lambda i: (0, 0)),
           pl.BlockSpec((1, D_MODEL), lambda i: (0, 0))],
        out_specs=pl.BlockSpec((tm, D_MODEL), lambda i: (i, 0)),
        out_shape=jax.ShapeDtypeStruct((rows, D_MODEL), F32),
        compiler_params=pltpu.CompilerParams(dimension_semantics=("arbitrary",)),
        name="outproj",
    )(x2, *ys, w, g)


FFN_COL_CHUNK = D_FF // 2


def _ffn_kernel(x_ref, xh_ref, gpre_ref, wg_ref, wu_ref, cw_ref, cb_ref, wd_ref, gpost_ref, o_ref, *, tm, t):
    x = x_ref[...]
    h = (x * _rms_scale(x) * gpre_ref[...]).astype(BF16)
    xh = xh_ref[...]
    hh = (xh * _rms_scale(xh) * gpre_ref[...]).astype(BF16)
    seq_start = (pl.program_id(0) * tm) % t == 0
    acc = None
    for c0 in range(0, D_FF, FFN_COL_CHUNK):
        cs = slice(c0, c0 + FFN_COL_CHUNK)
        gate = _dot(h, wg_ref[:, cs])
        halo = jnp.where(seq_start, 0.0, _dot(hh, wg_ref[:, cs]))
        conv = cb_ref[:, cs] + cw_ref[FFN_CONV - 1:FFN_CONV, cs] * gate
        for d in range(1, FFN_CONV):
            conv = conv + cw_ref[FFN_CONV - 1 - d:FFN_CONV - d, cs] * _shift_rows(gate, halo, d)
        f = jax.nn.gelu(conv, approximate=True) * _dot(h, wu_ref[:, cs])
        part = _dot(f.astype(BF16), wd_ref[cs, :])
        acc = part if acc is None else acc + part
    o_ref[...] = x + acc * _rms_scale(acc) * gpost_ref[...]


def _ffn(x2, p, t, tm):
    rows = x2.shape[0]
    hb = tm // SUBLANES
    const = lambda a: pl.BlockSpec(a.shape, lambda i: (0, 0), pipeline_mode=pl.Buffered(1))
    params = [p["g_pre"], p["w_gate"], p["w_up"], p["conv_w"], p["conv_b"], p["w_down"], p["g_post"]]
    return pl.pallas_call(
        functools.partial(_ffn_kernel, tm=tm, t=t),
        grid=(rows // tm,),
        in_specs=[pl.BlockSpec((tm, D_MODEL), lambda i: (i, 0)),
                  pl.BlockSpec((SUBLANES, D_MODEL), lambda i: (jnp.maximum(i * hb - 1, 0), 0))]
        + [const(a) for a in params],
        out_specs=pl.BlockSpec((tm, D_MODEL), lambda i: (i, 0)),
        out_shape=jax.ShapeDtypeStruct((rows, D_MODEL), F32),
        compiler_params=pltpu.CompilerParams(dimension_semantics=("arbitrary",),
                                             vmem_limit_bytes=56 * 1024 * 1024),
        name="ffn",
    )(x2, x2, *params)


def _pad_cols(a, width):
    return jnp.pad(a, ((0, 0), (0, width - a.shape[1])))


def _row(a):
    return a.reshape(1, -1).astype(F32)


def _layer_params(l, w_in, w_out, norm_mix_pre, norm_mix_post, norm_ffn_pre, norm_ffn_post,
                  rwkv_mu, rwkv_w0, rwkv_w_up, rwkv_a0, rwkv_a_up, rwkv_g_up, rwkv_k_k, rwkv_k_a,
                  rwkv_r_k, rwkv_ln_w, rwkv_ln_b, mlstm_conv_w, mlstm_conv_b, mlstm_b_i, mlstm_b_f,
                  mlstm_norm, swa_sinks, fox_b_f, ffn_w_up, ffn_conv_w, ffn_conv_b, ffn_w_down):
    wa_w = ZA_W
    wb_w = 3 * MIX_W + 2 * N_HEADS
    wc_w = ZC_W
    wi = w_in[l]
    groups = [wi[:, :wa_w],
              _pad_cols(wi[:, wa_w:wa_w + wb_w], ZB_W),
              wi[:, wa_w + wb_w:wa_w + wb_w + wc_w],
              _pad_cols(wi[:, wa_w + wb_w + wc_w:], ZD_W)]
    gate_b = _pad_cols(jnp.concatenate([mlstm_b_i[l], mlstm_b_f[l]]).reshape(1, -1), LANES)
    return {
        "w_in": jnp.concatenate(groups, axis=1).astype(BF16),
        "g_mix_pre": _row(norm_mix_pre[l]),
        "g_mix_post": _row(norm_mix_post[l]),
        "w_out": w_out[l].astype(BF16),
        "rwkv": {"mu": _row(rwkv_mu[l]), "w0": _row(rwkv_w0[l]), "w_up": rwkv_w_up[l].astype(BF16),
                 "a0": _row(rwkv_a0[l]), "a_up": rwkv_a_up[l].astype(BF16), "g_up": rwkv_g_up[l].astype(BF16),
                 "k_k": _row(rwkv_k_k[l]), "k_a": _row(rwkv_k_a[l]), "r_k": _row(rwkv_r_k[l]),
                 "ln_w": _row(rwkv_ln_w[l]), "ln_b": _row(rwkv_ln_b[l])},
        "mlstm": {"conv_w": mlstm_conv_w[l], "conv_b": _row(mlstm_conv_b[l]), "gate_b": gate_b,
                  "norm_g": _row(mlstm_norm[l])},
        "swa_sinks": swa_sinks[l],
        "fox_b": _pad_cols(fox_b_f[l].reshape(1, -1), LANES),
        "ffn": {"g_pre": _row(norm_ffn_pre[l]), "w_gate": ffn_w_up[l][:, :D_FF].astype(BF16),
                "w_up": ffn_w_up[l][:, D_FF:].astype(BF16), "conv_w": ffn_conv_w[l],
                "conv_b": _row(ffn_conv_b[l]), "w_down": ffn_w_down[l].astype(BF16),
                "g_post": _row(norm_ffn_post[l])},
    }


def _tiles(t):
    return min(512, t), min(256, t), min(256, t)


def kernel(x, w_in, w_out, norm_mix_pre, norm_mix_post, norm_ffn_pre, norm_ffn_post, rwkv_mu, rwkv_w0, rwkv_w_up, rwkv_a0, rwkv_a_up, rwkv_g_up, rwkv_k_k, rwkv_k_a, rwkv_r_k, rwkv_ln_w, rwkv_ln_b, mlstm_conv_w, mlstm_conv_b, mlstm_b_i, mlstm_b_f, mlstm_norm, swa_sinks, fox_b_f, rel_bias, ffn_w_up, ffn_conv_w, ffn_conv_b, ffn_w_down):
    bsz, t, d = x.shape
    assert d == D_MODEL and t % ATTN_BLOCK == 0
    tm, tc, tf = _tiles(t)
    x2 = x.reshape(bsz * t, d)
    for l in range(w_in.shape[0]):
        p = _layer_params(l, w_in, w_out, norm_mix_pre, norm_mix_post, norm_ffn_pre, norm_ffn_post,
                          rwkv_mu, rwkv_w0, rwkv_w_up, rwkv_a0, rwkv_a_up, rwkv_g_up, rwkv_k_k, rwkv_k_a,
                          rwkv_r_k, rwkv_ln_w, rwkv_ln_b, mlstm_conv_w, mlstm_conv_b, mlstm_b_i, mlstm_b_f,
                          mlstm_norm, swa_sinks, fox_b_f, ffn_w_up, ffn_conv_w, ffn_conv_b, ffn_w_down)
        za, zb, zc, zd = _inproj(x2, p["g_mix_pre"], p["w_in"], tm)
        ya = _rwkv(za, p["rwkv"], bsz, t, tc)
        yb = _mlstm(zb, p["mlstm"], bsz, t, tc)
        yc = _swa(zc, rel_bias, p["swa_sinks"], bsz, t)
        yd = _fox(zd, p["fox_b"], bsz, t)
        x2 = _outproj(x2, (ya, yb, yc, yd), p["w_out"], p["g_mix_post"], tm)
        x2 = _ffn(x2, p["ffn"], t, tf)
    return x2.reshape(bsz, t, d)
```

```python
import functools
import math

import jax
import jax.numpy as jnp
import numpy as np
from jax import lax
from jax.experimental import pallas as pl
from jax.experimental.pallas import tpu as pltpu

F32 = jnp.float32
BF16 = jnp.bfloat16

D_MODEL = 1024
HEAD_DIM = 64
N_HEADS = 4
MIX_W = N_HEADS * HEAD_DIM
RWKV_DECAY_RANK = 64
RWKV_AAA_RANK = 64
RWKV_GATE_RANK = 128
RWKV_LN_EPS = 64e-5
RWKV_CHUNK = 64
MLSTM_DK = 32
MLSTM_CONV = 4
MLSTM_CHUNK = 64
GATE_SOFTCAP = 15.0
SWA_KV_HEADS = 2
SWA_WINDOW = 128
ATTN_BLOCK = 128
REL_BUCKETS = 32
REL_MAX_DIST = 128
D_FF = 2816
FFN_CONV = 3
NORM_EPS = 1e-6

LANES = 128
SUBLANES = 8

ZA_W = 3 * MIX_W + RWKV_DECAY_RANK + RWKV_AAA_RANK + RWKV_GATE_RANK
ZB_W = 3 * MIX_W + LANES
ZC_W = MIX_W + 2 * SWA_KV_HEADS * HEAD_DIM
ZD_W = 3 * MIX_W + LANES
Z_W = ZA_W + ZB_W + ZC_W + ZD_W

def _split_bf16(x, parts):
    out = []
    for n in range(parts):
        piece = x.astype(BF16)
        out.append(piece)
        if n + 1 < parts:
            x = x - piece.astype(F32)
    return out


def _dot(a, b, precision=None):
    return jnp.dot(a, b, preferred_element_type=F32, precision=precision)


def _dot_nt(a, b, precision=None):
    return lax.dot_general(a, b, (((1,), (1,)), ((), ())), preferred_element_type=F32, precision=precision)


def _dot_tn(a, b, precision=None):
    return lax.dot_general(a, b, (((0,), (0,)), ((), ())), preferred_element_type=F32, precision=precision)


def _dot_split(x, ones, parts):
    return sum(_dot(piece, ones) for piece in _split_bf16(x, parts))


def _split_dot(ones, x, parts):
    return sum(_dot(ones, piece) for piece in _split_bf16(x, parts))


def _bdot(a, b):
    return _dot(a.astype(BF16), b.astype(BF16))


def _bdot_nt(a, b):
    return _dot_nt(a.astype(BF16), b.astype(BF16))


def _bdot_tn(a, b):
    return _dot_tn(a.astype(BF16), b.astype(BF16))


def _sigmoid(x):
    return 1.0 / (1.0 + jnp.exp(-x))


def _log_sigmoid(x):
    return jnp.minimum(x, 0.0) - jnp.log1p(jnp.exp(-jnp.abs(x)))


def _softplus(x):
    return jnp.maximum(x, 0.0) + jnp.log1p(jnp.exp(-jnp.abs(x)))


def _rms_scale(x):
    return lax.rsqrt(jnp.mean(x * x, axis=-1, keepdims=True) + NORM_EPS)


def _shift_rows(x, prev_tail, d):
    rolled = pltpu.roll(x, d, axis=0)
    head_rows = lax.broadcasted_iota(jnp.int32, (SUBLANES, 1), 0)
    head = jnp.where(head_rows < d, pltpu.roll(prev_tail, d, axis=0), rolled[:SUBLANES])
    return jnp.concatenate([head, rolled[SUBLANES:]], axis=0)


def _head_ones(width, head):
    r = lax.broadcasted_iota(jnp.int32, (width, width), 0) // head
    c = lax.broadcasted_iota(jnp.int32, (width, width), 1) // head
    return (r == c).astype(BF16)


def _inproj_kernel(x_ref, g_ref, w_ref, za_ref, zb_ref, zc_ref, zd_ref):
    x = x_ref[...]
    h = (x * _rms_scale(x) * g_ref[...]).astype(BF16)
    off = 0
    for ref in (za_ref, zb_ref, zc_ref, zd_ref):
        w = ref.shape[1]
        ref[...] = _dot(h, w_ref[:, off:off + w])
        off += w


def _inproj(x2, g, w, tm):
    rows = x2.shape[0]
    return pl.pallas_call(
        _inproj_kernel,
        grid=(rows // tm,),
        in_specs=[
            pl.BlockSpec((tm, D_MODEL), lambda i: (i, 0)),
            pl.BlockSpec((1, D_MODEL), lambda i: (0, 0)),
            pl.BlockSpec((D_MODEL, Z_W), lambda i: (0, 0)),
        ],
        out_specs=[pl.BlockSpec((tm, w_), lambda i: (i, 0)) for w_ in (ZA_W, ZB_W, ZC_W, ZD_W)],
        out_shape=[jax.ShapeDtypeStruct((rows, w_), F32) for w_ in (ZA_W, ZB_W, ZC_W, ZD_W)],
        compiler_params=pltpu.CompilerParams(dimension_semantics=("arbitrary",)),
        name="inproj",
    )(x2, g, w)


def _rwkv_kernel(z_ref, mu_ref, w0_ref, wup_ref, a0_ref, aup_ref, gup_ref, kk_ref, ka_ref, rk_ref,
                 lnw_ref, lnb_ref, o_ref, tail_ref, st_ref, y_ref, *, tc):
    L = RWKV_CHUNK
    W = MIX_W

    @pl.when(pl.program_id(1) == 0)
    def _():
        tail_ref[...] = jnp.zeros_like(tail_ref)
        st_ref[...] = jnp.zeros_like(st_ref)

    z = z_ref[...]
    zz = z + mu_ref[...] * (_shift_rows(z, tail_ref[...], 1) - z)
    tail_ref[...] = z[tc - SUBLANES:, :]

    r = zz[:, 0:W]
    k = zz[:, W:2 * W]
    v = zz[:, 2 * W:3 * W]
    o1 = 3 * W
    wd = zz[:, o1:o1 + RWKV_DECAY_RANK]
    ad = zz[:, o1 + RWKV_DECAY_RANK:o1 + RWKV_DECAY_RANK + RWKV_AAA_RANK]
    gd = zz[:, o1 + RWKV_DECAY_RANK + RWKV_AAA_RANK:]

    lw = -jnp.exp(-_softplus(-(w0_ref[...] + _bdot(jnp.tanh(wd), wup_ref[...]))) - 0.5)
    alpha = _sigmoid(a0_ref[...] + _bdot(ad, aup_ref[...]))
    g = _bdot(_sigmoid(gd), gup_ref[...])

    hsum = _head_ones(W, HEAD_DIM)
    kk = k * kk_ref[...]
    kk = kk / jnp.maximum(jnp.sqrt(_dot_split(kk * kk, hsum, 2)), 1e-12)
    k = k * (1.0 + (alpha - 1.0) * ka_ref[...])

    rt = lax.broadcasted_iota(jnp.int32, (tc, tc), 0)
    ct = lax.broadcasted_iota(jnp.int32, (tc, tc), 1)
    tri = ((rt // L == ct // L) & (ct <= rt)).astype(BF16)
    cum = _split_dot(tri, lw, 3)
    e_in = jnp.exp(cum)
    e_out = jnp.exp(-cum)
    r_t = r * e_in
    a_t = -kk * jnp.exp(cum - lw)
    b_t = kk * alpha * e_out
    k_t = k * e_out

    lane_head = lax.broadcasted_iota(jnp.int32, (1, W), 1) // HEAD_DIM

    def stack(xc):
        return jnp.concatenate([jnp.where(lane_head == h, xc, 0.0) for h in range(N_HEADS)], axis=0)

    n = N_HEADS * L
    rr = lax.broadcasted_iota(jnp.int32, (n, n), 0)
    cc = lax.broadcasted_iota(jnp.int32, (n, n), 1)
    strict = (rr % L) > (cc % L)
    incl = (rr % L) >= (cc % L)
    eye = (rr == cc).astype(F32)

    st = st_ref[...]
    for c in range(tc // L):
        sl = slice(c * L, (c + 1) * L)
        a_s, r_s, b_s, k_s, v_s = (stack(u[sl]).astype(BF16) for u in (a_t, r_t, b_t, k_t, v))
        a_ab = jnp.where(strict, _dot_nt(a_s, b_s), 0.0)
        a_ak = jnp.where(strict, _dot_nt(a_s, k_s), 0.0).astype(BF16)
        a_rb = jnp.where(incl, _dot_nt(r_s, b_s), 0.0).astype(BF16)
        a_rk = jnp.where(incl, _dot_nt(r_s, k_s), 0.0).astype(BF16)
        inv = eye + a_ab
        pw = a_ab.astype(BF16)
        for _ in range(int(math.log2(L)) - 1):
            pw = _dot(pw, pw).astype(BF16)
            inv = inv + _dot(inv.astype(BF16), pw)
        inv = inv.astype(BF16)
        w_m = _dot(inv, a_s).astype(BF16)
        u2 = _dot(inv, _dot(a_ak, v_s).astype(BF16))
        st_b = st.astype(BF16)
        u = _dot_nt(w_m, st_b) + u2
        u_b = u.astype(BF16)
        y_s = _dot_nt(r_s, st_b) + _dot(a_rb, u_b) + _dot(a_rk, v_s)
        st = (st + _dot_tn(u_b, b_s) + _dot_tn(v_s, k_s)) * e_in[(c + 1) * L - 1:(c + 1) * L, :]
        y_c = y_s[0:L]
        for h in range(1, N_HEADS):
            y_c = y_c + y_s[h * L:(h + 1) * L]
        y_ref[sl, :] = y_c
    st_ref[...] = st

    y = y_ref[...]
    inv_n = 1.0 / HEAD_DIM
    mean = _dot_split(y, hsum, 2) * inv_n
    yc = y - mean
    var = _dot_split(yc * yc, hsum, 2) * inv_n
    y = yc * lax.rsqrt(var + RWKV_LN_EPS) * lnw_ref[...] + lnb_ref[...]
    bonus = _dot_split(r * k * rk_ref[...], hsum, 2) * v
    o_ref[...] = (y + bonus) * g


def _rwkv(za, p, bsz, t, tc):
    nt = t // tc
    full = lambda a: pl.BlockSpec(a.shape, lambda b, i: (0,) * a.ndim)
    params = [p["mu"], p["w0"], p["w_up"], p["a0"], p["a_up"], p["g_up"], p["k_k"], p["k_a"], p["r_k"],
              p["ln_w"], p["ln_b"]]
    return pl.pallas_call(
        functools.partial(_rwkv_kernel, tc=tc),
        grid=(bsz, nt),
        in_specs=[pl.BlockSpec((tc, ZA_W), lambda b, i: (b * nt + i, 0))] + [full(a) for a in params],
        out_specs=pl.BlockSpec((tc, MIX_W), lambda b, i: (b * nt + i, 0)),
        out_shape=jax.ShapeDtypeStruct((bsz * t, MIX_W), F32),
        scratch_shapes=[
            pltpu.VMEM((SUBLANES, ZA_W), F32),
            pltpu.VMEM((MIX_W, MIX_W), F32),
            pltpu.VMEM((tc, MIX_W), F32),
        ],
        compiler_params=pltpu.CompilerParams(dimension_semantics=("arbitrary", "arbitrary")),
        name="rwkv7",
    )(za, *params)


def _mlstm_kernel(z_ref, cw_ref, cb_ref, gb_ref, ng_ref, o_ref, tail_ref, c_ref, n_ref, m_ref, h_ref, *, tc):
    L = MLSTM_CHUNK
    W = MIX_W
    DK = MLSTM_DK
    DV = HEAD_DIM

    @pl.when(pl.program_id(1) == 0)
    def _():
        tail_ref[...] = jnp.zeros_like(tail_ref)
        c_ref[...] = jnp.zeros_like(c_ref)
        n_ref[...] = jnp.zeros_like(n_ref)
        m_ref[...] = jnp.zeros_like(m_ref)

    qk_in = z_ref[:, 0:W]
    v = z_ref[:, W:2 * W]
    og = z_ref[:, 2 * W:3 * W]
    gates = z_ref[:, 3 * W:3 * W + LANES]

    tail = tail_ref[...]
    conv = cb_ref[...] + cw_ref[MLSTM_CONV - 1:MLSTM_CONV, :] * qk_in
    for d in range(1, MLSTM_CONV):
        conv = conv + cw_ref[MLSTM_CONV - 1 - d:MLSTM_CONV - d, :] * _shift_rows(qk_in, tail, d)
    tail_ref[...] = qk_in[tc - SUBLANES:, :]
    qk = conv * _sigmoid(conv)
    q = qk[:, 0:N_HEADS * DK] * (DK ** -0.5)
    k = qk[:, N_HEADS * DK:]

    capped = GATE_SOFTCAP * jnp.tanh((gates + gb_ref[...]) / GATE_SOFTCAP)
    li = capped
    lf = _log_sigmoid(capped)
    rt = lax.broadcasted_iota(jnp.int32, (tc, tc), 0)
    ct = lax.broadcasted_iota(jnp.int32, (tc, tc), 1)
    tri = ((rt // L == ct // L) & (ct <= rt)).astype(BF16)
    bcum = _split_dot(tri, lf, 3)
    li_r = li.T
    bcum_r = bcum.T

    causal = lax.broadcasted_iota(jnp.int32, (L, L), 1) <= lax.broadcasted_iota(jnp.int32, (L, L), 0)

    for c in range(tc // L):
        sl = slice(c * L, (c + 1) * L)
        for h in range(N_HEADS):
            qc = q[sl, h * DK:(h + 1) * DK]
            kc = k[sl, h * DK:(h + 1) * DK]
            vc = v[sl, h * DV:(h + 1) * DV]
            b_col = bcum[sl, N_HEADS + h:N_HEADS + h + 1]
            li_col = li[sl, h:h + 1]
            b_row = bcum_r[N_HEADS + h:N_HEADS + h + 1, sl]
            li_row = li_r[h:h + 1, sl]
            c_st = c_ref[h]
            n_st = n_ref[h:h + 1, :]
            m_st = m_ref[h:h + 1, :]

            dmat = jnp.where(causal, b_col - b_row + li_row, -jnp.inf)
            m_inter = b_col + m_st
            m_t = jnp.maximum(m_inter, jnp.max(dmat, axis=-1, keepdims=True))
            inter = jnp.exp(m_inter - m_t)
            s = _bdot_nt(qc, kc) * jnp.exp(dmat - m_t)
            num = _bdot(s, vc) + inter * _bdot_nt(qc, c_st)
            den = jnp.sum(s, axis=-1, keepdims=True) + inter * jnp.sum(qc * n_st, axis=-1, keepdims=True)
            hh = num / jnp.maximum(jnp.abs(den), jnp.exp(-m_t))

            b_last = b_col[L - 1:L, :]
            m_new = jnp.maximum(b_last + m_st, jnp.max(b_last - b_row + li_row, axis=-1, keepdims=True))
            wts = jnp.exp(b_last - b_col + li_col - m_new)
            dec = jnp.exp(b_last + m_st - m_new)
            c_ref[h] = dec * c_st + _bdot_tn(wts * vc, kc)
            n_ref[h:h + 1, :] = dec * n_st + jnp.sum(wts * kc, axis=0, keepdims=True)
            m_ref[h:h + 1, :] = m_new
            h_ref[sl, h * DV:(h + 1) * DV] = hh

    hv = h_ref[...]
    ms = _dot_split(hv * hv, _head_ones(W, DV), 2) * (1.0 / DV)
    o_ref[...] = hv * lax.rsqrt(ms + NORM_EPS) * ng_ref[...] * _sigmoid(og)


def _mlstm(zb, p, bsz, t, tc):
    nt = t // tc
    full = lambda a: pl.BlockSpec(a.shape, lambda b, i: (0,) * a.ndim)
    params = [p["conv_w"], p["conv_b"], p["gate_b"], p["norm_g"]]
    return pl.pallas_call(
        functools.partial(_mlstm_kernel, tc=tc),
        grid=(bsz, nt),
        in_specs=[pl.BlockSpec((tc, ZB_W), lambda b, i: (b * nt + i, 0))] + [full(a) for a in params],
        out_specs=pl.BlockSpec((tc, MIX_W), lambda b, i: (b * nt + i, 0)),
        out_shape=jax.ShapeDtypeStruct((bsz * t, MIX_W), F32),
        scratch_shapes=[
            pltpu.VMEM((SUBLANES, MIX_W), F32),
            pltpu.VMEM((N_HEADS, HEAD_DIM, MLSTM_DK), F32),
            pltpu.VMEM((N_HEADS, MLSTM_DK), F32),
            pltpu.VMEM((N_HEADS, 1), F32),
            pltpu.VMEM((tc, MIX_W), F32),
        ],
        compiler_params=pltpu.CompilerParams(dimension_semantics=("arbitrary", "arbitrary")),
        name="mlstm",
    )(zb, *params)


def _t5_bucket(dist):
    max_exact = REL_BUCKETS // 2
    d = np.maximum(dist, 1).astype(np.float32)
    large = max_exact + (np.log(d / max_exact) / math.log(REL_MAX_DIST / max_exact)
                         * (REL_BUCKETS - max_exact)).astype(np.int32)
    large = np.minimum(large, REL_BUCKETS - 1)
    return np.where(dist < max_exact, dist, large).astype(np.int32)


def _swa_kernel(rb_ref, sink_ref, bucket_ref, q_ref, kp_ref, kc_ref, vp_ref, vc_ref, o_ref, bias_ref):
    blk = ATTN_BLOCK
    first = (pl.program_id(0) == 0) & (pl.program_id(1) == 0)

    @pl.when(first)
    def _():
        bucket = bucket_ref[...]
        for h in range(N_HEADS):
            acc = jnp.zeros((blk, 2 * blk), F32)
            for bk in range(REL_BUCKETS):
                acc = jnp.where(bucket == bk, rb_ref[bk, h], acc)
            bias_ref[h] = acc

    tq = lax.broadcasted_iota(jnp.int32, (blk, 2 * blk), 0)
    sk = lax.broadcasted_iota(jnp.int32, (blk, 2 * blk), 1)
    dist = tq + blk - sk
    mask = (dist >= 0) & (dist < SWA_WINDOW) & ((sk >= blk) | (pl.program_id(1) > 0))
    grp = N_HEADS // SWA_KV_HEADS
    for h in range(N_HEADS):
        kvh = h // grp
        hs = slice(h * HEAD_DIM, (h + 1) * HEAD_DIM)
        ks = slice(kvh * HEAD_DIM, (kvh + 1) * HEAD_DIM)
        kw = jnp.concatenate([kp_ref[:, ks], kc_ref[:, ks]], axis=0)
        vw = jnp.concatenate([vp_ref[:, ks], vc_ref[:, ks]], axis=0)
        logits = _bdot_nt(q_ref[:, hs], kw) * (HEAD_DIM ** -0.5) + bias_ref[h]
        logits = jnp.where(mask, logits, -jnp.inf)
        sink = sink_ref[h]
        m = jnp.maximum(jnp.max(logits, axis=-1, keepdims=True), sink)
        p = jnp.exp(logits - m)
        denom = jnp.sum(p, axis=-1, keepdims=True) + jnp.exp(sink - m)
        o_ref[:, hs] = _bdot(p / denom, vw)


def _swa(zc, rel_bias, sinks, bsz, t):
    blk = ATTN_BLOCK
    nb = t // blk
    dist = np.arange(blk)[:, None] + blk - np.arange(2 * blk)[None, :]
    bucket = jnp.asarray(_t5_bucket(np.clip(dist, 0, SWA_WINDOW - 1)))
    kvw = SWA_KV_HEADS * HEAD_DIM
    kcol = MIX_W // kvw
    cur = lambda b, i: b * nb + i
    prev = lambda b, i: b * nb + jnp.maximum(i - 1, 0)
    return pl.pallas_call(
        _swa_kernel,
        grid=(bsz, nb),
        in_specs=[
            pl.BlockSpec(memory_space=pltpu.SMEM),
            pl.BlockSpec(memory_space=pltpu.SMEM),
            pl.BlockSpec((blk, 2 * blk), lambda b, i: (0, 0)),
            pl.BlockSpec((blk, MIX_W), lambda b, i: (cur(b, i), 0)),
            pl.BlockSpec((blk, kvw), lambda b, i: (prev(b, i), kcol)),
            pl.BlockSpec((blk, kvw), lambda b, i: (cur(b, i), kcol)),
            pl.BlockSpec((blk, kvw), lambda b, i: (prev(b, i), kcol + 1)),
            pl.BlockSpec((blk, kvw), lambda b, i: (cur(b, i), kcol + 1)),
        ],
        out_specs=pl.BlockSpec((blk, MIX_W), lambda b, i: (cur(b, i), 0)),
        out_shape=jax.ShapeDtypeStruct((bsz * t, MIX_W), F32),
        scratch_shapes=[pltpu.VMEM((N_HEADS, blk, 2 * blk), F32)],
        compiler_params=pltpu.CompilerParams(dimension_semantics=("arbitrary", "arbitrary")),
        name="swa",
    )(rel_bias, sinks, bucket, zc, zc, zc, zc, zc)


FOX_BLOCK = 256
FOX_AUG = LANES
FOX_PARTS = 3


def _fox_placements():
    wide = N_HEADS * FOX_AUG
    pk = np.zeros((MIX_W, wide), np.float32)
    pck = np.zeros((LANES, wide), np.float32)
    ones_k = np.zeros((SUBLANES, wide), np.float32)
    pq_t = np.zeros((wide, MIX_W), np.float32)
    pv_t = np.zeros((wide, MIX_W), np.float32)
    pcq_t = np.zeros((wide, LANES), np.float32)
    ones_t = np.zeros((2, wide, LANES), np.float32)
    for h in range(N_HEADS):
        base = h * FOX_AUG
        for d in range(HEAD_DIM):
            pk[h * HEAD_DIM + d, base + d] = 1.0
            pq_t[base + d, h * HEAD_DIM + d] = HEAD_DIM ** -0.5
            pv_t[base + d, h * HEAD_DIM + d] = 1.0
        for n in range(FOX_PARTS):
            pcq_t[base + HEAD_DIM + n, n * N_HEADS + h] = 1.0
            pck[n * N_HEADS + h, base + HEAD_DIM + FOX_PARTS + n] = -1.0
            ones_t[0, base + HEAD_DIM + FOX_PARTS + n, :] = 1.0
            ones_k[0, base + HEAD_DIM + n] = 1.0
        ones_t[1, base + HEAD_DIM, :] = 1.0
    bf = lambda a: jnp.asarray(a, BF16)
    return bf(pk), bf(pck), jnp.asarray(ones_k), bf(np.stack([pq_t, pv_t])), bf(pcq_t), jnp.asarray(ones_t)


def _fox_kernel(q_ref, k_ref, v_ref, f_ref, fb_ref, pk_ref, pck_ref, onesk_ref, pqv_ref, pcq_ref, onest_ref,
                o_ref, kaug_ref, vaug_ref, m_ref, acc_ref, clast_ref, s_ref, p_ref):
    blk = FOX_BLOCK
    i = pl.program_id(1)

    @pl.when(i == 0)
    def _():
        clast_ref[...] = jnp.zeros_like(clast_ref)

    ls = _log_sigmoid(f_ref[...] + fb_ref[...])
    row = lax.broadcasted_iota(jnp.int32, (blk, blk), 0)
    col = lax.broadcasted_iota(jnp.int32, (blk, blk), 1)
    cq = _split_dot((col <= row).astype(BF16), ls, FOX_PARTS) + clast_ref[0:1, :]
    clast_ref[0:1, :] = cq[blk - 1:blk, :]

    widen = lambda a: jnp.concatenate([a] * (blk // LANES), axis=1)
    qa = _dot_nt(pqv_ref[0], q_ref[...].astype(BF16)) + widen(onest_ref[0])
    va = _dot_nt(pqv_ref[1], v_ref[...].astype(BF16)) + widen(onest_ref[1])
    ka = _dot(k_ref[...].astype(BF16), pk_ref[...]) + onesk_ref[0:1, :]
    lane = lax.broadcasted_iota(jnp.int32, (1, LANES), 1)
    pieces = _split_bf16(jnp.where(lane < N_HEADS, cq, 0.0), FOX_PARTS)
    packed = pieces[0].astype(F32)
    for n in range(1, FOX_PARTS):
        packed = packed + pltpu.roll(pieces[n].astype(F32), n * N_HEADS, axis=1)
    packed = packed.astype(BF16)
    qa = (qa + _dot_nt(pcq_ref[...], packed)).astype(BF16)
    ka = ka + _dot(packed, pck_ref[...])
    row0 = pl.multiple_of(i * blk, blk)
    for h in range(N_HEADS):
        hs = slice(h * FOX_AUG, (h + 1) * FOX_AUG)
        kaug_ref[h, pl.ds(row0, blk), :] = ka[:, hs].astype(BF16)
        vaug_ref[h, :, pl.ds(row0, blk)] = va[hs, :].astype(BF16)

    key_le_query = row <= col

    def attend(off, diagonal):
        for h in range(N_HEADS):
            s = _dot(kaug_ref[h, pl.ds(off, blk), :], qa[h * FOX_AUG:(h + 1) * FOX_AUG, :])
            s_ref[h] = jnp.where(key_le_query, s, -jnp.inf) if diagonal else s
        m_new = [jnp.maximum(m_ref[h], jnp.max(s_ref[h], axis=0, keepdims=True)) for h in range(N_HEADS)]
        for h in range(N_HEADS):
            p_ref[h] = jnp.exp(s_ref[h] - m_new[h]).astype(BF16)
        for h in range(N_HEADS):
            pv = _dot(vaug_ref[h, :, pl.ds(off, blk)], p_ref[h])
            acc_ref[h] = jnp.exp(m_ref[h] - m_new[h]) * acc_ref[h] + pv
            m_ref[h] = m_new[h]

    m_ref[...] = jnp.full_like(m_ref, -jnp.inf)
    acc_ref[...] = jnp.zeros_like(acc_ref)
    attend(row0, True)

    def body(j, carry):
        attend(pl.multiple_of(j * blk, blk), False)
        return carry

    lax.fori_loop(0, i, body, 0)
    outs = []
    for h in range(N_HEADS):
        acc = acc_ref[h]
        outs.append(acc[0:HEAD_DIM, :] / acc[HEAD_DIM:HEAD_DIM + 1, :])
    o_ref[...] = jnp.concatenate(outs, axis=0).T


def _fox(zd, fb, bsz, t):
    blk = FOX_BLOCK
    assert t % blk == 0
    nb = t // blk
    consts = _fox_placements()
    const = lambda a: pl.BlockSpec(a.shape, lambda b, i: (0,) * a.ndim)
    cur = lambda c: (lambda b, i: (b * nb + i, c))
    return pl.pallas_call(
        _fox_kernel,
        grid=(bsz, nb),
        in_specs=[
            pl.BlockSpec((blk, MIX_W), cur(0)),
            pl.BlockSpec((blk, MIX_W), cur(1)),
            pl.BlockSpec((blk, MIX_W), cur(2)),
            pl.BlockSpec((blk, LANES), cur(3 * MIX_W // LANES)),
            const(fb)] + [const(a) for a in consts],
        out_specs=pl.BlockSpec((blk, MIX_W), cur(0)),
        out_shape=jax.ShapeDtypeStruct((bsz * t, MIX_W), F32),
        scratch_shapes=[
            pltpu.VMEM((N_HEADS, t, FOX_AUG), BF16),
            pltpu.VMEM((N_HEADS, FOX_AUG, t), BF16),
            pltpu.VMEM((N_HEADS, 1, blk), F32),
            pltpu.VMEM((N_HEADS, FOX_AUG, blk), F32),
            pltpu.VMEM((SUBLANES, LANES), F32),
            pltpu.VMEM((N_HEADS, blk, blk), F32),
            pltpu.VMEM((N_HEADS, blk, blk), BF16),
        ],
        compiler_params=pltpu.CompilerParams(dimension_semantics=("arbitrary", "arbitrary")),
        name="fox",
    )(zd, zd, zd, zd, fb, *consts)


def _outproj_kernel(x_ref, ya_ref, yb_ref, yc_ref, yd_ref, w_ref, g_ref, o_ref):
    acc = None
    for n, ref in enumerate((ya_ref, yb_ref, yc_ref, yd_ref)):
        part = _dot(ref[...].astype(BF16), w_ref[n * MIX_W:(n + 1) * MIX_W, :])
        acc = part if acc is None else acc + part
    o_ref[...] = x_ref[...] + acc * _rms_scale(acc) * g_ref[...]


def _outproj(x2, ys, w, g, tm):
    rows = x2.shape[0]
    return pl.pallas_call(
        _outproj_kernel,
        grid=(rows // tm,),
        in_specs=[pl.BlockSpec((tm, D_MODEL), lambda i: (i, 0))]
        + [pl.BlockSpec((tm, MIX_W), lambda i: (i, 0))] * 4
        + [pl.BlockSpec((N_HEADS * MIX_W, D_MODEL), lambda i: (0, 0)),
           pl.BlockSpec((1, D_MODEL), lambda i: (0, 0))],
        out_specs=pl.BlockSpec((tm, D_MODEL), lambda i: (i, 0)),
        out_shape=jax.ShapeDtypeStruct((rows, D_MODEL), F32),
        compiler_params=pltpu.CompilerParams(dimension_semantics=("arbitrary",)),
        name="outproj",
    )(x2, *ys, w, g)


FFN_COL_CHUNK = D_FF // 2


def _ffn_kernel(x_ref, xh_ref, gpre_ref, wg_ref, wu_ref, cw_ref, cb_ref, wd_ref, gpost_ref, o_ref, *, tm, t):
    x = x_ref[...]
    h = (x * _rms_scale(x) * gpre_ref[...]).astype(BF16)
    xh = xh_ref[...]
    hh = (xh * _rms_scale(xh) * gpre_ref[...]).astype(BF16)
    seq_start = (pl.program_id(0) * tm) % t == 0
    acc = None
    for c0 in range(0, D_FF, FFN_COL_CHUNK):
        cs = slice(c0, c0 + FFN_COL_CHUNK)
        gate = _dot(h, wg_ref[:, cs])
        halo = jnp.where(seq_start, 0.0, _dot(hh, wg_ref[:, cs]))
        conv = cb_ref[:, cs] + cw_ref[FFN_CONV - 1:FFN_CONV, cs] * gate
        for d in range(1, FFN_CONV):
            conv = conv + cw_ref[FFN_CONV - 1 - d:FFN_CONV - d, cs] * _shift_rows(gate, halo, d)
        f = jax.nn.gelu(conv, approximate=True) * _dot(h, wu_ref[:, cs])
        part = _dot(f.astype(BF16), wd_ref[cs, :])
        acc = part if acc is None else acc + part
    o_ref[...] = x + acc * _rms_scale(acc) * gpost_ref[...]


def _ffn(x2, p, t, tm):
    rows = x2.shape[0]
    hb = tm // SUBLANES
    const = lambda a: pl.BlockSpec(a.shape, lambda i: (0, 0), pipeline_mode=pl.Buffered(1))
    params = [p["g_pre"], p["w_gate"], p["w_up"], p["conv_w"], p["conv_b"], p["w_down"], p["g_post"]]
    return pl.pallas_call(
        functools.partial(_ffn_kernel, tm=tm, t=t),
        grid=(rows // tm,),
        in_specs=[pl.BlockSpec((tm, D_MODEL), lambda i: (i, 0)),
                  pl.BlockSpec((SUBLANES, D_MODEL), lambda i: (jnp.maximum(i * hb - 1, 0), 0))]
        + [const(a) for a in params],
        out_specs=pl.BlockSpec((tm, D_MODEL), lambda i: (i, 0)),
        out_shape=jax.ShapeDtypeStruct((rows, D_MODEL), F32),
        compiler_params=pltpu.CompilerParams(dimension_semantics=("arbitrary",),
                                             vmem_limit_bytes=56 * 1024 * 1024),
        name="ffn",
    )(x2, x2, *params)


def _pad_cols(a, width):
    return jnp.pad(a, ((0, 0), (0, width - a.shape[1])))


def _row(a):
    return a.reshape(1, -1).astype(F32)


def _layer_params(l, w_in, w_out, norm_mix_pre, norm_mix_post, norm_ffn_pre, norm_ffn_post,
                  rwkv_mu, rwkv_w0, rwkv_w_up, rwkv_a0, rwkv_a_up, rwkv_g_up, rwkv_k_k, rwkv_k_a,
                  rwkv_r_k, rwkv_ln_w, rwkv_ln_b, mlstm_conv_w, mlstm_conv_b, mlstm_b_i, mlstm_b_f,
                  mlstm_norm, swa_sinks, fox_b_f, ffn_w_up, ffn_conv_w, ffn_conv_b, ffn_w_down):
    wa_w = ZA_W
    wb_w = 3 * MIX_W + 2 * N_HEADS
    wc_w = ZC_W
    wi = w_in[l]
    groups = [wi[:, :wa_w],
              _pad_cols(wi[:, wa_w:wa_w + wb_w], ZB_W),
              wi[:, wa_w + wb_w:wa_w + wb_w + wc_w],
              _pad_cols(wi[:, wa_w + wb_w + wc_w:], ZD_W)]
    gate_b = _pad_cols(jnp.concatenate([mlstm_b_i[l], mlstm_b_f[l]]).reshape(1, -1), LANES)
    return {
        "w_in": jnp.concatenate(groups, axis=1).astype(BF16),
        "g_mix_pre": _row(norm_mix_pre[l]),
        "g_mix_post": _row(norm_mix_post[l]),
        "w_out": w_out[l].astype(BF16),
        "rwkv": {"mu": _row(rwkv_mu[l]), "w0": _row(rwkv_w0[l]), "w_up": rwkv_w_up[l].astype(BF16),
                 "a0": _row(rwkv_a0[l]), "a_up": rwkv_a_up[l].astype(BF16), "g_up": rwkv_g_up[l].astype(BF16),
                 "k_k": _row(rwkv_k_k[l]), "k_a": _row(rwkv_k_a[l]), "r_k": _row(rwkv_r_k[l]),
                 "ln_w": _row(rwkv_ln_w[l]), "ln_b": _row(rwkv_ln_b[l])},
        "mlstm": {"conv_w": mlstm_conv_w[l], "conv_b": _row(mlstm_conv_b[l]), "gate_b": gate_b,
                  "norm_g": _row(mlstm_norm[l])},
        "swa_sinks": swa_sinks[l],
        "fox_b": _pad_cols(fox_b_f[l].reshape(1, -1), LANES),
        "ffn": {"g_pre": _row(norm_ffn_pre[l]), "w_gate": ffn_w_up[l][:, :D_FF].astype(BF16),
                "w_up": ffn_w_up[l][:, D_FF:].astype(BF16), "conv_w": ffn_conv_w[l],
                "conv_b": _row(ffn_conv_b[l]), "w_down": ffn_w_down[l].astype(BF16),
                "g_post": _row(norm_ffn_post[l])},
    }


def _tiles(t):
    return min(512, t), min(256, t), min(256, t)


def kernel(x, w_in, w_out, norm_mix_pre, norm_mix_post, norm_ffn_pre, norm_ffn_post, rwkv_mu, rwkv_w0, rwkv_w_up, rwkv_a0, rwkv_a_up, rwkv_g_up, rwkv_k_k, rwkv_k_a, rwkv_r_k, rwkv_ln_w, rwkv_ln_b, mlstm_conv_w, mlstm_conv_b, mlstm_b_i, mlstm_b_f, mlstm_norm, swa_sinks, fox_b_f, rel_bias, ffn_w_up, ffn_conv_w, ffn_conv_b, ffn_w_down):
    bsz, t, d = x.shape
    assert d == D_MODEL and t % ATTN_BLOCK == 0
    tm, tc, tf = _tiles(t)
    x2 = x.reshape(bsz * t, d)
    for l in range(w_in.shape[0]):
        p = _layer_params(l, w_in, w_out, norm_mix_pre, norm_mix_post, norm_ffn_pre, norm_ffn_post,
                          rwkv_mu, rwkv_w0, rwkv_w_up, rwkv_a0, rwkv_a_up, rwkv_g_up, rwkv_k_k, rwkv_k_a,
                          rwkv_r_k, rwkv_ln_w, rwkv_ln_b, mlstm_conv_w, mlstm_conv_b, mlstm_b_i, mlstm_b_f,
                          mlstm_norm, swa_sinks, fox_b_f, ffn_w_up, ffn_conv_w, ffn_conv_b, ffn_w_down)
        za, zb, zc, zd = _inproj(x2, p["g_mix_pre"], p["w_in"], tm)
        ya = _rwkv(za, p["rwkv"], bsz, t, tc)
        yb = _mlstm(zb, p["mlstm"], bsz, t, tc)
        yc = _swa(zc, rel_bias, p["swa_sinks"], bsz, t)
        yd = _fox(zd, p["fox_b"], bsz, t)
        x2 = _outproj(x2, (ya, yb, yc, yd), p["w_out"], p["g_mix_post"], tm)
        x2 = _ffn(x2, p["ffn"], t, tf)
    return x2.reshape(bsz, t, d)
```

```python
import functools
import math

import jax
import jax.numpy as jnp
import numpy as np
from jax import lax
from jax.experimental import pallas as pl
from jax.experimental.pallas import tpu as pltpu

F32 = jnp.float32
BF16 = jnp.bfloat16

D_MODEL = 1024
HEAD_DIM = 64
N_HEADS = 4
MIX_W = N_HEADS * HEAD_DIM
RWKV_DECAY_RANK = 64
RWKV_AAA_RANK = 64
RWKV_GATE_RANK = 128
RWKV_LN_EPS = 64e-5
RWKV_CHUNK = 64
MLSTM_DK = 32
MLSTM_CONV = 4
MLSTM_CHUNK = 64
GATE_SOFTCAP = 15.0
SWA_KV_HEADS = 2
SWA_WINDOW = 128
ATTN_BLOCK = 128
REL_BUCKETS = 32
REL_MAX_DIST = 128
D_FF = 2816
FFN_CONV = 3
NORM_EPS = 1e-6

LANES = 128
SUBLANES = 8

ZA_W = 3 * MIX_W + RWKV_DECAY_RANK + RWKV_AAA_RANK + RWKV_GATE_RANK
ZB_W = 3 * MIX_W + LANES
ZC_W = MIX_W + 2 * SWA_KV_HEADS * HEAD_DIM
ZD_W = 3 * MIX_W + LANES
Z_W = ZA_W + ZB_W + ZC_W + ZD_W

def _split_bf16(x, parts):
    out = []
    for n in range(parts):
        piece = x.astype(BF16)
        out.append(piece)
        if n + 1 < parts:
            x = x - piece.astype(F32)
    return out


def _dot(a, b, precision=None):
    return jnp.dot(a, b, preferred_element_type=F32, precision=precision)


def _dot_nt(a, b, precision=None):
    return lax.dot_general(a, b, (((1,), (1,)), ((), ())), preferred_element_type=F32, precision=precision)


def _dot_tn(a, b, precision=None):
    return lax.dot_general(a, b, (((0,), (0,)), ((), ())), preferred_element_type=F32, precision=precision)


def _dot_split(x, ones, parts):
    return sum(_dot(piece, ones) for piece in _split_bf16(x, parts))


def _split_dot(ones, x, parts):
    return sum(_dot(ones, piece) for piece in _split_bf16(x, parts))


def _bdot(a, b):
    return _dot(a.astype(BF16), b.astype(BF16))


def _bdot_nt(a, b):
    return _dot_nt(a.astype(BF16), b.astype(BF16))


def _bdot_tn(a, b):
    return _dot_tn(a.astype(BF16), b.astype(BF16))


def _sigmoid(x):
    return 1.0 / (1.0 + jnp.exp(-x))


def _log_sigmoid(x):
    return jnp.minimum(x, 0.0) - jnp.log1p(jnp.exp(-jnp.abs(x)))


def _softplus(x):
    return jnp.maximum(x, 0.0) + jnp.log1p(jnp.exp(-jnp.abs(x)))


def _rms_scale(x):
    return lax.rsqrt(jnp.mean(x * x, axis=-1, keepdims=True) + NORM_EPS)


def _shift_rows(x, prev_tail, d):
    rolled = pltpu.roll(x, d, axis=0)
    head_rows = lax.broadcasted_iota(jnp.int32, (SUBLANES, 1), 0)
    head = jnp.where(head_rows < d, pltpu.roll(prev_tail, d, axis=0), rolled[:SUBLANES])
    return jnp.concatenate([head, rolled[SUBLANES:]], axis=0)


def _head_ones(width, head):
    r = lax.broadcasted_iota(jnp.int32, (width, width), 0) // head
    c = lax.broadcasted_iota(jnp.int32, (width, width), 1) // head
    return (r == c).astype(BF16)


def _inproj_kernel(x_ref, g_ref, w_ref, za_ref, zb_ref, zc_ref, zd_ref):
    x = x_ref[...]
    h = (x * _rms_scale(x) * g_ref[...]).astype(BF16)
    off = 0
    for ref in (za_ref, zb_ref, zc_ref, zd_ref):
        w = ref.shape[1]
        ref[...] = _dot_nt(h, w_ref[off:off + w, :])
        off += w


def _inproj(x2, g, w, tm):
    rows = x2.shape[0]
    return pl.pallas_call(
        _inproj_kernel,
        grid=(rows // tm,),
        in_specs=[
            pl.BlockSpec((tm, D_MODEL), lambda i: (i, 0)),
            pl.BlockSpec((1, D_MODEL), lambda i: (0, 0)),
            pl.BlockSpec((Z_W, D_MODEL), lambda i: (0, 0)),
        ],
        out_specs=[pl.BlockSpec((tm, w_), lambda i: (i, 0)) for w_ in (ZA_W, ZB_W, ZC_W, ZD_W)],
        out_shape=[jax.ShapeDtypeStruct((rows, w_), F32) for w_ in (ZA_W, ZB_W, ZC_W, ZD_W)],
        compiler_params=pltpu.CompilerParams(dimension_semantics=("arbitrary",)),
        name="inproj",
    )(x2, g, w)


def _rwkv_kernel(z_ref, mu_ref, w0_ref, wup_ref, a0_ref, aup_ref, gup_ref, kk_ref, ka_ref, rk_ref,
                 lnw_ref, lnb_ref, o_ref, tail_ref, st_ref, y_ref, *, tc):
    L = RWKV_CHUNK
    W = MIX_W

    @pl.when(pl.program_id(1) == 0)
    def _():
        tail_ref[...] = jnp.zeros_like(tail_ref)
        st_ref[...] = jnp.zeros_like(st_ref)

    z = z_ref[...]
    zz = z + mu_ref[...] * (_shift_rows(z, tail_ref[...], 1) - z)
    tail_ref[...] = z[tc - SUBLANES:, :]

    r = zz[:, 0:W]
    k = zz[:, W:2 * W]
    v = zz[:, 2 * W:3 * W]
    o1 = 3 * W
    wd = zz[:, o1:o1 + RWKV_DECAY_RANK]
    ad = zz[:, o1 + RWKV_DECAY_RANK:o1 + RWKV_DECAY_RANK + RWKV_AAA_RANK]
    gd = zz[:, o1 + RWKV_DECAY_RANK + RWKV_AAA_RANK:]

    lw = -jnp.exp(-_softplus(-(w0_ref[...] + _bdot(jnp.tanh(wd), wup_ref[...]))) - 0.5)
    alpha = _sigmoid(a0_ref[...] + _bdot(ad, aup_ref[...]))
    g = _bdot(_sigmoid(gd), gup_ref[...])

    hsum = _head_ones(W, HEAD_DIM)
    kk = k * kk_ref[...]
    kk = kk / jnp.maximum(jnp.sqrt(_dot_split(kk * kk, hsum, 2)), 1e-12)
    k = k * (1.0 + (alpha - 1.0) * ka_ref[...])

    rt = lax.broadcasted_iota(jnp.int32, (tc, tc), 0)
    ct = lax.broadcasted_iota(jnp.int32, (tc, tc), 1)
    tri = ((rt // L == ct // L) & (ct <= rt)).astype(BF16)
    cum = _split_dot(tri, lw, 3)
    e_in = jnp.exp(cum)
    e_out = jnp.exp(-cum)
    r_t = r * e_in
    a_t = -kk * jnp.exp(cum - lw)
    b_t = kk * alpha * e_out
    k_t = k * e_out

    lane_head = lax.broadcasted_iota(jnp.int32, (1, W), 1) // HEAD_DIM

    def stack(xc):
        return jnp.concatenate([jnp.where(lane_head == h, xc, 0.0) for h in range(N_HEADS)], axis=0)

    n = N_HEADS * L
    rr = lax.broadcasted_iota(jnp.int32, (n, n), 0)
    cc = lax.broadcasted_iota(jnp.int32, (n, n), 1)
    strict = (rr % L) > (cc % L)
    incl = (rr % L) >= (cc % L)
    eye = (rr == cc).astype(F32)

    st = st_ref[...]
    for c in range(tc // L):
        sl = slice(c * L, (c + 1) * L)
        a_s, r_s, b_s, k_s, v_s = (stack(u[sl]).astype(BF16) for u in (a_t, r_t, b_t, k_t, v))
        a_ab = jnp.where(strict, _dot_nt(a_s, b_s), 0.0)
        a_ak = jnp.where(strict, _dot_nt(a_s, k_s), 0.0).astype(BF16)
        a_rb = jnp.where(incl, _dot_nt(r_s, b_s), 0.0).astype(BF16)
        a_rk = jnp.where(incl, _dot_nt(r_s, k_s), 0.0).astype(BF16)
        inv = eye + a_ab
        pw = a_ab.astype(BF16)
        for _ in range(int(math.log2(L)) - 1):
            pw = _dot(pw, pw).astype(BF16)
            inv = inv + _dot(inv.astype(BF16), pw)
        inv = inv.astype(BF16)
        w_m = _dot(inv, a_s).astype(BF16)
        u2 = _dot(inv, _dot(a_ak, v_s).astype(BF16))
        st_b = st.astype(BF16)
        u = _dot_nt(w_m, st_b) + u2
        u_b = u.astype(BF16)
        y_s = _dot_nt(r_s, st_b) + _dot(a_rb, u_b) + _dot(a_rk, v_s)
        st = (st + _dot_tn(u_b, b_s) + _dot_tn(v_s, k_s)) * e_in[(c + 1) * L - 1:(c + 1) * L, :]
        y_c = y_s[0:L]
        for h in range(1, N_HEADS):
            y_c = y_c + y_s[h * L:(h + 1) * L]
        y_ref[sl, :] = y_c
    st_ref[...] = st

    y = y_ref[...]
    inv_n = 1.0 / HEAD_DIM
    mean = _dot_split(y, hsum, 2) * inv_n
    yc = y - mean
    var = _dot_split(yc * yc, hsum, 2) * inv_n
    y = yc * lax.rsqrt(var + RWKV_LN_EPS) * lnw_ref[...] + lnb_ref[...]
    bonus = _dot_split(r * k * rk_ref[...], hsum, 2) * v
    o_ref[...] = (y + bonus) * g


def _rwkv(za, p, bsz, t, tc):
    nt = t // tc
    full = lambda a: pl.BlockSpec(a.shape, lambda b, i: (0,) * a.ndim)
    params = [p["mu"], p["w0"], p["w_up"], p["a0"], p["a_up"], p["g_up"], p["k_k"], p["k_a"], p["r_k"],
              p["ln_w"], p["ln_b"]]
    return pl.pallas_call(
        functools.partial(_rwkv_kernel, tc=tc),
        grid=(bsz, nt),
        in_specs=[pl.BlockSpec((tc, ZA_W), lambda b, i: (b * nt + i, 0))] + [full(a) for a in params],
        out_specs=pl.BlockSpec((tc, MIX_W), lambda b, i: (b * nt + i, 0)),
        out_shape=jax.ShapeDtypeStruct((bsz * t, MIX_W), F32),
        scratch_shapes=[
            pltpu.VMEM((SUBLANES, ZA_W), F32),
            pltpu.VMEM((MIX_W, MIX_W), F32),
            pltpu.VMEM((tc, MIX_W), F32),
        ],
        compiler_params=pltpu.CompilerParams(dimension_semantics=("arbitrary", "arbitrary")),
        name="rwkv7",
    )(za, *params)


def _mlstm_kernel(z_ref, cw_ref, cb_ref, gb_ref, ng_ref, o_ref, tail_ref, c_ref, n_ref, m_ref, h_ref, *, tc):
    L = MLSTM_CHUNK
    W = MIX_W
    DK = MLSTM_DK
    DV = HEAD_DIM

    @pl.when(pl.program_id(1) == 0)
    def _():
        tail_ref[...] = jnp.zeros_like(tail_ref)
        c_ref[...] = jnp.zeros_like(c_ref)
        n_ref[...] = jnp.zeros_like(n_ref)
        m_ref[...] = jnp.zeros_like(m_ref)

    qk_in = z_ref[:, 0:W]
    v = z_ref[:, W:2 * W]
    og = z_ref[:, 2 * W:3 * W]
    gates = z_ref[:, 3 * W:3 * W + LANES]

    tail = tail_ref[...]
    conv = cb_ref[...] + cw_ref[MLSTM_CONV - 1:MLSTM_CONV, :] * qk_in
    for d in range(1, MLSTM_CONV):
        conv = conv + cw_ref[MLSTM_CONV - 1 - d:MLSTM_CONV - d, :] * _shift_rows(qk_in, tail, d)
    tail_ref[...] = qk_in[tc - SUBLANES:, :]
    qk = conv * _sigmoid(conv)
    q = qk[:, 0:N_HEADS * DK] * (DK ** -0.5)
    k = qk[:, N_HEADS * DK:]

    capped = GATE_SOFTCAP * jnp.tanh((gates + gb_ref[...]) / GATE_SOFTCAP)
    lf = _log_sigmoid(capped)

    gate_col = lax.broadcasted_iota(jnp.int32, (LANES, W), 0)
    lane_head = lax.broadcasted_iota(jnp.int32, (LANES, W), 1) // DV
    pick_i = (gate_col == lane_head).astype(BF16)
    pick_f = (gate_col == N_HEADS + lane_head).astype(BF16)
    li_e = _dot_split(capped, pick_i, 3)
    rt = lax.broadcasted_iota(jnp.int32, (tc, tc), 0)
    ct = lax.broadcasted_iota(jnp.int32, (tc, tc), 1)
    tri = ((rt // L == ct // L) & (ct <= rt)).astype(BF16)
    b_e = sum(_dot(tri, _dot(piece, pick_f).astype(BF16)) for piece in _split_bf16(lf, 3))

    key = lax.broadcasted_iota(jnp.int32, (L, W), 0)
    query = lax.broadcasted_iota(jnp.int32, (L, W), 1) % L
    on_diag = key == query
    causal_t = key <= query
    head_ones = _head_ones(W, DV)
    wide = lambda m: jnp.concatenate([m, m], axis=1)
    same_head_q = wide(lax.broadcasted_iota(jnp.int32, (N_HEADS * L, W), 0) // L
                       == lax.broadcasted_iota(jnp.int32, (N_HEADS * L, W), 1) // DV)
    same_head_k = wide(lax.broadcasted_iota(jnp.int32, (N_HEADS * DK, W), 0) // DK
                       == lax.broadcasted_iota(jnp.int32, (N_HEADS * DK, W), 1) // DV)
    q_lane_head = lax.broadcasted_iota(jnp.int32, (1, N_HEADS * DK), 1) // DK
    ones_b = jnp.ones((L, W), BF16)

    c_st = c_ref[...]
    n_st = n_ref[...]
    m_st = m_ref[0:1, :]
    for c in range(tc // L):
        sl = slice(c * L, (c + 1) * L)
        bc, lic, vc = b_e[sl], li_e[sl], v[sl]
        qc = q[sl]
        kc = k[sl].astype(BF16)
        q_stack = jnp.concatenate([jnp.where(q_lane_head == h, qc, 0.0) for h in range(N_HEADS)],
                                  axis=0).astype(BF16)
        b_q = jnp.sum(jnp.where(on_diag, bc, 0.0), axis=0, keepdims=True)
        dmat = jnp.where(causal_t, b_q - bc + lic, -jnp.inf)
        m_t = jnp.maximum(b_q + m_st, jnp.max(dmat, axis=0, keepdims=True))
        s_t = (_dot_nt(kc, q_stack) * jnp.exp(dmat - m_t)).astype(BF16)
        m_e = sum(_dot(jnp.where(on_diag, piece.astype(F32), 0.0).astype(BF16), head_ones)
                  for piece in _split_bf16(m_t, 3))
        inter = jnp.exp(bc + m_st - m_e)
        nv = _dot_tn(s_t, jnp.concatenate([vc.astype(BF16), ones_b], axis=1))
        nv = jnp.where(same_head_q, nv, 0.0)
        nv = sum(nv[h * L:(h + 1) * L] for h in range(N_HEADS))
        qcn = _dot(qc.astype(BF16), jnp.concatenate([c_st, n_st], axis=1).astype(BF16))
        num = nv[:, :W] + inter * qcn[:, :W]
        den = nv[:, W:] + inter * qcn[:, W:]
        h_ref[sl, :] = num / jnp.maximum(jnp.abs(den), jnp.exp(-m_e))

        b_last = bc[L - 1:L, :]
        gexp = b_last - bc + lic
        m_new = jnp.maximum(b_last + m_st, jnp.max(gexp, axis=0, keepdims=True))
        wts = jnp.exp(gexp - m_new)
        dec = jnp.exp(b_last + m_st - m_new)
        upd = _dot_tn(kc, jnp.concatenate([wts * vc, wts], axis=1).astype(BF16))
        upd = jnp.where(same_head_k, upd, 0.0)
        c_st = dec * c_st + upd[:, :W]
        n_st = dec * n_st + upd[:, W:]
        m_st = m_new
    c_ref[...] = c_st
    n_ref[...] = n_st
    m_ref[0:1, :] = m_st

    hv = h_ref[...]
    ms = _dot_split(hv * hv, _head_ones(W, DV), 2) * (1.0 / DV)
    o_ref[...] = hv * lax.rsqrt(ms + NORM_EPS) * ng_ref[...] * _sigmoid(og)


def _mlstm(zb, p, bsz, t, tc):
    nt = t // tc
    full = lambda a: pl.BlockSpec(a.shape, lambda b, i: (0,) * a.ndim)
    params = [p["conv_w"], p["conv_b"], p["gate_b"], p["norm_g"]]
    return pl.pallas_call(
        functools.partial(_mlstm_kernel, tc=tc),
        grid=(bsz, nt),
        in_specs=[pl.BlockSpec((tc, ZB_W), lambda b, i: (b * nt + i, 0))] + [full(a) for a in params],
        out_specs=pl.BlockSpec((tc, MIX_W), lambda b, i: (b * nt + i, 0)),
        out_shape=jax.ShapeDtypeStruct((bsz * t, MIX_W), F32),
        scratch_shapes=[
            pltpu.VMEM((SUBLANES, MIX_W), F32),
            pltpu.VMEM((N_HEADS * MLSTM_DK, MIX_W), F32),
            pltpu.VMEM((N_HEADS * MLSTM_DK, MIX_W), F32),
            pltpu.VMEM((SUBLANES, MIX_W), F32),
            pltpu.VMEM((tc, MIX_W), F32),
        ],
        compiler_params=pltpu.CompilerParams(dimension_semantics=("arbitrary", "arbitrary")),
        name="mlstm",
    )(zb, *params)


def _t5_bucket(dist):
    max_exact = REL_BUCKETS // 2
    d = np.maximum(dist, 1).astype(np.float32)
    large = max_exact + (np.log(d / max_exact) / math.log(REL_MAX_DIST / max_exact)
                         * (REL_BUCKETS - max_exact)).astype(np.int32)
    large = np.minimum(large, REL_BUCKETS - 1)
    return np.where(dist < max_exact, dist, large).astype(np.int32)


def _swa_kernel(rb_ref, sink_ref, bucket_ref, q_ref, kp_ref, kc_ref, vp_ref, vc_ref, o_ref, bias_ref):
    blk = ATTN_BLOCK
    first = (pl.program_id(0) == 0) & (pl.program_id(1) == 0)

    @pl.when(first)
    def _():
        bucket = bucket_ref[...]
        for h in range(N_HEADS):
            acc = jnp.zeros((blk, 2 * blk), F32)
            for bk in range(REL_BUCKETS):
                acc = jnp.where(bucket == bk, rb_ref[bk, h], acc)
            bias_ref[h] = acc

    tq = lax.broadcasted_iota(jnp.int32, (blk, 2 * blk), 0)
    sk = lax.broadcasted_iota(jnp.int32, (blk, 2 * blk), 1)
    dist = tq + blk - sk
    mask = (dist >= 0) & (dist < SWA_WINDOW) & ((sk >= blk) | (pl.program_id(1) > 0))
    grp = N_HEADS // SWA_KV_HEADS
    for h in range(N_HEADS):
        kvh = h // grp
        hs = slice(h * HEAD_DIM, (h + 1) * HEAD_DIM)
        ks = slice(kvh * HEAD_DIM, (kvh + 1) * HEAD_DIM)
        kw = jnp.concatenate([kp_ref[:, ks], kc_ref[:, ks]], axis=0)
        vw = jnp.concatenate([vp_ref[:, ks], vc_ref[:, ks]], axis=0)
        logits = _bdot_nt(q_ref[:, hs], kw) * (HEAD_DIM ** -0.5) + bias_ref[h]
        logits = jnp.where(mask, logits, -jnp.inf)
        sink = sink_ref[h]
        m = jnp.maximum(jnp.max(logits, axis=-1, keepdims=True), sink)
        p = jnp.exp(logits - m)
        denom = jnp.sum(p, axis=-1, keepdims=True) + jnp.exp(sink - m)
        o_ref[:, hs] = _bdot(p / denom, vw)


def _swa(zc, rel_bias, sinks, bsz, t):
    blk = ATTN_BLOCK
    nb = t // blk
    dist = np.arange(blk)[:, None] + blk - np.arange(2 * blk)[None, :]
    bucket = jnp.asarray(_t5_bucket(np.clip(dist, 0, SWA_WINDOW - 1)))
    kvw = SWA_KV_HEADS * HEAD_DIM
    kcol = MIX_W // kvw
    cur = lambda b, i: b * nb + i
    prev = lambda b, i: b * nb + jnp.maximum(i - 1, 0)
    return pl.pallas_call(
        _swa_kernel,
        grid=(bsz, nb),
        in_specs=[
            pl.BlockSpec(memory_space=pltpu.SMEM),
            pl.BlockSpec(memory_space=pltpu.SMEM),
            pl.BlockSpec((blk, 2 * blk), lambda b, i: (0, 0)),
            pl.BlockSpec((blk, MIX_W), lambda b, i: (cur(b, i), 0)),
            pl.BlockSpec((blk, kvw), lambda b, i: (prev(b, i), kcol)),
            pl.BlockSpec((blk, kvw), lambda b, i: (cur(b, i), kcol)),
            pl.BlockSpec((blk, kvw), lambda b, i: (prev(b, i), kcol + 1)),
            pl.BlockSpec((blk, kvw), lambda b, i: (cur(b, i), kcol + 1)),
        ],
        out_specs=pl.BlockSpec((blk, MIX_W), lambda b, i: (cur(b, i), 0)),
        out_shape=jax.ShapeDtypeStruct((bsz * t, MIX_W), F32),
        scratch_shapes=[pltpu.VMEM((N_HEADS, blk, 2 * blk), F32)],
        compiler_params=pltpu.CompilerParams(dimension_semantics=("arbitrary", "arbitrary")),
        name="swa",
    )(rel_bias, sinks, bucket, zc, zc, zc, zc, zc)


FOX_BLOCK = 256
FOX_AUG = LANES
FOX_PARTS = 3


def _fox_placements():
    wide = N_HEADS * FOX_AUG
    pk = np.zeros((MIX_W, wide), np.float32)
    pck = np.zeros((LANES, wide), np.float32)
    ones_k = np.zeros((SUBLANES, wide), np.float32)
    pq_t = np.zeros((wide, MIX_W), np.float32)
    pv_t = np.zeros((wide, MIX_W), np.float32)
    pcq_t = np.zeros((wide, LANES), np.float32)
    ones_t = np.zeros((2, wide, LANES), np.float32)
    for h in range(N_HEADS):
        base = h * FOX_AUG
        for d in range(HEAD_DIM):
            pk[h * HEAD_DIM + d, base + d] = 1.0
            pq_t[base + d, h * HEAD_DIM + d] = HEAD_DIM ** -0.5
            pv_t[base + d, h * HEAD_DIM + d] = 1.0
        for n in range(FOX_PARTS):
            pcq_t[base + HEAD_DIM + n, n * N_HEADS + h] = 1.0
            pck[n * N_HEADS + h, base + HEAD_DIM + FOX_PARTS + n] = -1.0
            ones_t[0, base + HEAD_DIM + FOX_PARTS + n, :] = 1.0
            ones_k[0, base + HEAD_DIM + n] = 1.0
        ones_t[1, base + HEAD_DIM, :] = 1.0
    bf = lambda a: jnp.asarray(a, BF16)
    return bf(pk), bf(pck), jnp.asarray(ones_k), bf(np.stack([pq_t, pv_t])), bf(pcq_t), jnp.asarray(ones_t)


def _fox_kernel(q_ref, k_ref, v_ref, f_ref, fb_ref, pk_ref, pck_ref, onesk_ref, pqv_ref, pcq_ref, onest_ref,
                o_ref, kaug_ref, vaug_ref, m_ref, acc_ref, clast_ref, s_ref, p_ref):
    blk = FOX_BLOCK
    i = pl.program_id(1)

    @pl.when(i == 0)
    def _():
        clast_ref[...] = jnp.zeros_like(clast_ref)

    ls = _log_sigmoid(f_ref[...] + fb_ref[...])
    row = lax.broadcasted_iota(jnp.int32, (blk, blk), 0)
    col = lax.broadcasted_iota(jnp.int32, (blk, blk), 1)
    cq = _split_dot((col <= row).astype(BF16), ls, FOX_PARTS) + clast_ref[0:1, :]
    clast_ref[0:1, :] = cq[blk - 1:blk, :]

    widen = lambda a: jnp.concatenate([a] * (blk // LANES), axis=1)
    qa = _dot_nt(pqv_ref[0], q_ref[...].astype(BF16)) + widen(onest_ref[0])
    va = _dot_nt(pqv_ref[1], v_ref[...].astype(BF16)) + widen(onest_ref[1])
    ka = _dot(k_ref[...].astype(BF16), pk_ref[...]) + onesk_ref[0:1, :]
    lane = lax.broadcasted_iota(jnp.int32, (1, LANES), 1)
    pieces = _split_bf16(jnp.where(lane < N_HEADS, cq, 0.0), FOX_PARTS)
    packed = pieces[0].astype(F32)
    for n in range(1, FOX_PARTS):
        packed = packed + pltpu.roll(pieces[n].astype(F32), n * N_HEADS, axis=1)
    packed = packed.astype(BF16)
    qa = (qa + _dot_nt(pcq_ref[...], packed)).astype(BF16)
    ka = ka + _dot(packed, pck_ref[...])
    row0 = pl.multiple_of(i * blk, blk)
    for h in range(N_HEADS):
        hs = slice(h * FOX_AUG, (h + 1) * FOX_AUG)
        kaug_ref[h, pl.ds(row0, blk), :] = ka[:, hs].astype(BF16)
        vaug_ref[h, :, pl.ds(row0, blk)] = va[hs, :].astype(BF16)

    key_le_query = row <= col

    def attend(off, diagonal):
        for h in range(N_HEADS):
            s = _dot(kaug_ref[h, pl.ds(off, blk), :], qa[h * FOX_AUG:(h + 1) * FOX_AUG, :])
            s_ref[h] = jnp.where(key_le_query, s, -jnp.inf) if diagonal else s
        m_new = [jnp.maximum(m_ref[h], jnp.max(s_ref[h], axis=0, keepdims=True)) for h in range(N_HEADS)]
        for h in range(N_HEADS):
            p_ref[h] = jnp.exp(s_ref[h] - m_new[h]).astype(BF16)
        for h in range(N_HEADS):
            pv = _dot(vaug_ref[h, :, pl.ds(off, blk)], p_ref[h])
            acc_ref[h] = jnp.exp(m_ref[h] - m_new[h]) * acc_ref[h] + pv
            m_ref[h] = m_new[h]

    m_ref[...] = jnp.full_like(m_ref, -jnp.inf)
    acc_ref[...] = jnp.zeros_like(acc_ref)
    attend(row0, True)

    def body(j, carry):
        attend(pl.multiple_of(j * blk, blk), False)
        return carry

    lax.fori_loop(0, i, body, 0)
    outs = []
    for h in range(N_HEADS):
        acc = acc_ref[h]
        outs.append(acc[0:HEAD_DIM, :] / acc[HEAD_DIM:HEAD_DIM + 1, :])
    o_ref[...] = jnp.concatenate(outs, axis=0).T


def _fox(zd, fb, bsz, t):
    blk = FOX_BLOCK
    assert t % blk == 0
    nb = t // blk
    consts = _fox_placements()
    const = lambda a: pl.BlockSpec(a.shape, lambda b, i: (0,) * a.ndim)
    cur = lambda c: (lambda b, i: (b * nb + i, c))
    return pl.pallas_call(
        _fox_kernel,
        grid=(bsz, nb),
        in_specs=[
            pl.BlockSpec((blk, MIX_W), cur(0)),
            pl.BlockSpec((blk, MIX_W), cur(1)),
            pl.BlockSpec((blk, MIX_W), cur(2)),
            pl.BlockSpec((blk, LANES), cur(3 * MIX_W // LANES)),
            const(fb)] + [const(a) for a in consts],
        out_specs=pl.BlockSpec((blk, MIX_W), cur(0)),
        out_shape=jax.ShapeDtypeStruct((bsz * t, MIX_W), F32),
        scratch_shapes=[
            pltpu.VMEM((N_HEADS, t, FOX_AUG), BF16),
            pltpu.VMEM((N_HEADS, FOX_AUG, t), BF16),
            pltpu.VMEM((N_HEADS, 1, blk), F32),
            pltpu.VMEM((N_HEADS, FOX_AUG, blk), F32),
            pltpu.VMEM((SUBLANES, LANES), F32),
            pltpu.VMEM((N_HEADS, blk, blk), F32),
            pltpu.VMEM((N_HEADS, blk, blk), BF16),
        ],
        compiler_params=pltpu.CompilerParams(dimension_semantics=("arbitrary", "arbitrary")),
        name="fox",
    )(zd, zd, zd, zd, fb, *consts)


def _outproj_kernel(x_ref, ya_ref, yb_ref, yc_ref, yd_ref, w_ref, g_ref, o_ref):
    acc = None
    for n, ref in enumerate((ya_ref, yb_ref, yc_ref, yd_ref)):
        part = _dot(ref[...].astype(BF16), w_ref[n * MIX_W:(n + 1) * MIX_W, :])
        acc = part if acc is None else acc + part
    o_ref[...] = x_ref[...] + acc * _rms_scale(acc) * g_ref[...]


def _outproj(x2, ys, w, g, tm):
    rows = x2.shape[0]
    return pl.pallas_call(
        _outproj_kernel,
        grid=(rows // tm,),
        in_specs=[pl.BlockSpec((tm, D_MODEL), lambda i: (i, 0))]
        + [pl.BlockSpec((tm, MIX_W), lambda i: (i, 0))] * 4
        + [pl.BlockSpec((N_HEADS * MIX_W, D_MODEL), lambda i: (0, 0)),
           pl.BlockSpec((1, D_MODEL), lambda i: (0, 0))],
        out_specs=pl.BlockSpec((tm, D_MODEL), lambda i: (i, 0)),
        out_shape=jax.ShapeDtypeStruct((rows, D_MODEL), F32),
        compiler_params=pltpu.CompilerParams(dimension_semantics=("arbitrary",)),
        name="outproj",
    )(x2, *ys, w, g)


FFN_COL_CHUNK = D_FF // 2


def _ffn_kernel(x_ref, xh_ref, gpre_ref, wg_ref, wu_ref, cw_ref, cb_ref, wd_ref, gpost_ref, o_ref, *, tm, t):
    x = x_ref[...]
    h = (x * _rms_scale(x) * gpre_ref[...]).astype(BF16)
    xh = xh_ref[...]
    hh = (xh * _rms_scale(xh) * gpre_ref[...]).astype(BF16)
    seq_start = (pl.program_id(0) * tm) % t == 0
    acc = None
    for c0 in range(0, D_FF, FFN_COL_CHUNK):
        cs = slice(c0, c0 + FFN_COL_CHUNK)
        gate = _dot(h, wg_ref[:, cs])
        halo = jnp.where(seq_start, 0.0, _dot(hh, wg_ref[:, cs]))
        conv = cb_ref[:, cs] + cw_ref[FFN_CONV - 1:FFN_CONV, cs] * gate
        for d in range(1, FFN_CONV):
            conv = conv + cw_ref[FFN_CONV - 1 - d:FFN_CONV - d, cs] * _shift_rows(gate, halo, d)
        f = jax.nn.gelu(conv, approximate=True) * _dot(h, wu_ref[:, cs])
        part = _dot(f.astype(BF16), wd_ref[cs, :])
        acc = part if acc is None else acc + part
    o_ref[...] = x + acc * _rms_scale(acc) * gpost_ref[...]


def _ffn(x2, p, t, tm):
    rows = x2.shape[0]
    hb = tm // SUBLANES
    const = lambda a: pl.BlockSpec(a.shape, lambda i: (0, 0), pipeline_mode=pl.Buffered(1))
    params = [p["g_pre"], p["w_gate"], p["w_up"], p["conv_w"], p["conv_b"], p["w_down"], p["g_post"]]
    return pl.pallas_call(
        functools.partial(_ffn_kernel, tm=tm, t=t),
        grid=(rows // tm,),
        in_specs=[pl.BlockSpec((tm, D_MODEL), lambda i: (i, 0)),
                  pl.BlockSpec((SUBLANES, D_MODEL), lambda i: (jnp.maximum(i * hb - 1, 0), 0))]
        + [const(a) for a in params],
        out_specs=pl.BlockSpec((tm, D_MODEL), lambda i: (i, 0)),
        out_shape=jax.ShapeDtypeStruct((rows, D_MODEL), F32),
        compiler_params=pltpu.CompilerParams(dimension_semantics=("arbitrary",),
                                             vmem_limit_bytes=56 * 1024 * 1024),
        name="ffn",
    )(x2, x2, *params)


def _pad_cols(a, width):
    return jnp.pad(a, ((0, 0), (0, width - a.shape[1])))


def _row(a):
    return a.reshape(1, -1).astype(F32)


def _layer_params(l, w_in, w_out, norm_mix_pre, norm_mix_post, norm_ffn_pre, norm_ffn_post,
                  rwkv_mu, rwkv_w0, rwkv_w_up, rwkv_a0, rwkv_a_up, rwkv_g_up, rwkv_k_k, rwkv_k_a,
                  rwkv_r_k, rwkv_ln_w, rwkv_ln_b, mlstm_conv_w, mlstm_conv_b, mlstm_b_i, mlstm_b_f,
                  mlstm_norm, swa_sinks, fox_b_f, ffn_w_up, ffn_conv_w, ffn_conv_b, ffn_w_down):
    wa_w = ZA_W
    wb_w = 3 * MIX_W + 2 * N_HEADS
    wc_w = ZC_W
    wi = jnp.transpose(w_in, (2, 0, 1))[:, l, :].astype(BF16)
    pad_rows = lambda a, n: jnp.pad(a, ((0, n - a.shape[0]), (0, 0)))
    groups = [wi[:wa_w],
              pad_rows(wi[wa_w:wa_w + wb_w], ZB_W),
              wi[wa_w + wb_w:wa_w + wb_w + wc_w],
              pad_rows(wi[wa_w + wb_w + wc_w:], ZD_W)]
    gate_b = _pad_cols(jnp.concatenate([mlstm_b_i[l], mlstm_b_f[l]]).reshape(1, -1), LANES)
    return {
        "w_in": jnp.concatenate(groups, axis=0),
        "g_mix_pre": _row(norm_mix_pre[l]),
        "g_mix_post": _row(norm_mix_post[l]),
        "w_out": w_out[l].astype(BF16),
        "rwkv": {"mu": _row(rwkv_mu[l]), "w0": _row(rwkv_w0[l]), "w_up": rwkv_w_up[l].astype(BF16),
                 "a0": _row(rwkv_a0[l]), "a_up": rwkv_a_up[l].astype(BF16), "g_up": rwkv_g_up[l].astype(BF16),
                 "k_k": _row(rwkv_k_k[l]), "k_a": _row(rwkv_k_a[l]), "r_k": _row(rwkv_r_k[l]),
                 "ln_w": _row(rwkv_ln_w[l]), "ln_b": _row(rwkv_ln_b[l])},
        "mlstm": {"conv_w": mlstm_conv_w[l], "conv_b": _row(mlstm_conv_b[l]), "gate_b": gate_b,
                  "norm_g": _row(mlstm_norm[l])},
        "swa_sinks": swa_sinks[l],
        "fox_b": _pad_cols(fox_b_f[l].reshape(1, -1), LANES),
        "ffn": {"g_pre": _row(norm_ffn_pre[l]), "w_gate": ffn_w_up[l][:, :D_FF].astype(BF16),
                "w_up": ffn_w_up[l][:, D_FF:].astype(BF16), "conv_w": ffn_conv_w[l],
                "conv_b": _row(ffn_conv_b[l]), "w_down": ffn_w_down[l].astype(BF16),
                "g_post": _row(norm_ffn_post[l])},
    }


def _tiles(t):
    return min(512, t), min(256, t), min(256, t)


def kernel(x, w_in, w_out, norm_mix_pre, norm_mix_post, norm_ffn_pre, norm_ffn_post, rwkv_mu, rwkv_w0, rwkv_w_up, rwkv_a0, rwkv_a_up, rwkv_g_up, rwkv_k_k, rwkv_k_a, rwkv_r_k, rwkv_ln_w, rwkv_ln_b, mlstm_conv_w, mlstm_conv_b, mlstm_b_i, mlstm_b_f, mlstm_norm, swa_sinks, fox_b_f, rel_bias, ffn_w_up, ffn_conv_w, ffn_conv_b, ffn_w_down):
    bsz, t, d = x.shape
    assert d == D_MODEL and t % ATTN_BLOCK == 0
    tm, tc, tf = _tiles(t)
    x2 = x.reshape(bsz * t, d)
    for l in range(w_in.shape[0]):
        p = _layer_params(l, w_in, w_out, norm_mix_pre, norm_mix_post, norm_ffn_pre, norm_ffn_post,
                          rwkv_mu, rwkv_w0, rwkv_w_up, rwkv_a0, rwkv_a_up, rwkv_g_up, rwkv_k_k, rwkv_k_a,
                          rwkv_r_k, rwkv_ln_w, rwkv_ln_b, mlstm_conv_w, mlstm_conv_b, mlstm_b_i, mlstm_b_f,
                          mlstm_norm, swa_sinks, fox_b_f, ffn_w_up, ffn_conv_w, ffn_conv_b, ffn_w_down)
        za, zb, zc, zd = _inproj(x2, p["g_mix_pre"], p["w_in"], tm)
        ya = _rwkv(za, p["rwkv"], bsz, t, tc)
        yb = _mlstm(zb, p["mlstm"], bsz, t, tc)
        yc = _swa(zc, rel_bias, p["swa_sinks"], bsz, t)
        yd = _fox(zd, p["fox_b"], bsz, t)
        x2 = _outproj(x2, (ya, yb, yc, yd), p["w_out"], p["g_mix_post"], tm)
        x2 = _ffn(x2, p["ffn"], t, tf)
    return x2.reshape(bsz, t, d)
```

```python
import functools
import math

import jax
import jax.numpy as jnp
import numpy as np
from jax import lax
from jax.experimental import pallas as pl
from jax.experimental.pallas import tpu as pltpu

F32 = jnp.float32
BF16 = jnp.bfloat16

D_MODEL = 1024
HEAD_DIM = 64
N_HEADS = 4
MIX_W = N_HEADS * HEAD_DIM
RWKV_DECAY_RANK = 64
RWKV_AAA_RANK = 64
RWKV_GATE_RANK = 128
RWKV_LN_EPS = 64e-5
RWKV_CHUNK = 64
MLSTM_DK = 32
MLSTM_CONV = 4
MLSTM_CHUNK = 64
GATE_SOFTCAP = 15.0
SWA_KV_HEADS = 2
SWA_WINDOW = 128
ATTN_BLOCK = 128
REL_BUCKETS = 32
REL_MAX_DIST = 128
D_FF = 2816
FFN_CONV = 3
NORM_EPS = 1e-6

LANES = 128
SUBLANES = 8

ZA_W = 3 * MIX_W + RWKV_DECAY_RANK + RWKV_AAA_RANK + RWKV_GATE_RANK
ZB_W = 3 * MIX_W + LANES
ZC_W = MIX_W + 2 * SWA_KV_HEADS * HEAD_DIM
ZD_W = 3 * MIX_W + LANES
Z_W = ZA_W + ZB_W + ZC_W + ZD_W

def _split_bf16(x, parts):
    out = []
    for n in range(parts):
        piece = x.astype(BF16)
        out.append(piece)
        if n + 1 < parts:
            x = x - piece.astype(F32)
    return out


def _dot(a, b, precision=None):
    return jnp.dot(a, b, preferred_element_type=F32, precision=precision)


def _dot_nt(a, b, precision=None):
    return lax.dot_general(a, b, (((1,), (1,)), ((), ())), preferred_element_type=F32, precision=precision)


def _dot_tn(a, b, precision=None):
    return lax.dot_general(a, b, (((0,), (0,)), ((), ())), preferred_element_type=F32, precision=precision)


def _dot_split(x, ones, parts):
    return sum(_dot(piece, ones) for piece in _split_bf16(x, parts))


def _split_dot(ones, x, parts):
    return sum(_dot(ones, piece) for piece in _split_bf16(x, parts))


def _bdot(a, b):
    return _dot(a.astype(BF16), b.astype(BF16))


def _bdot_nt(a, b):
    return _dot_nt(a.astype(BF16), b.astype(BF16))


def _bdot_tn(a, b):
    return _dot_tn(a.astype(BF16), b.astype(BF16))


def _sigmoid(x):
    return 1.0 / (1.0 + jnp.exp(-x))


def _log_sigmoid(x):
    return jnp.minimum(x, 0.0) - jnp.log1p(jnp.exp(-jnp.abs(x)))


def _softplus(x):
    return jnp.maximum(x, 0.0) + jnp.log1p(jnp.exp(-jnp.abs(x)))


def _rms_scale(x):
    return lax.rsqrt(jnp.mean(x * x, axis=-1, keepdims=True) + NORM_EPS)


def _shift_rows(x, prev_tail, d):
    rolled = pltpu.roll(x, d, axis=0)
    head_rows = lax.broadcasted_iota(jnp.int32, (SUBLANES, 1), 0)
    head = jnp.where(head_rows < d, pltpu.roll(prev_tail, d, axis=0), rolled[:SUBLANES])
    return jnp.concatenate([head, rolled[SUBLANES:]], axis=0)


def _head_ones(width, head):
    r = lax.broadcasted_iota(jnp.int32, (width, width), 0) // head
    c = lax.broadcasted_iota(jnp.int32, (width, width), 1) // head
    return (r == c).astype(BF16)


def _inproj_kernel(x_ref, g_ref, w_ref, za_ref, zb_ref, zc_ref, zd_ref):
    x = x_ref[...]
    h = (x * _rms_scale(x) * g_ref[...]).astype(BF16)
    z = _dot_nt(h, w_ref[...])
    off = 0
    for ref in (za_ref, zb_ref, zc_ref, zd_ref):
        w = ref.shape[1]
        ref[...] = z[:, off:off + w]
        off += w


def _inproj(x2, g, w, tm):
    rows = x2.shape[0]
    return pl.pallas_call(
        _inproj_kernel,
        grid=(rows // tm,),
        in_specs=[
            pl.BlockSpec((tm, D_MODEL), lambda i: (i, 0)),
            pl.BlockSpec((1, D_MODEL), lambda i: (0, 0)),
            pl.BlockSpec((Z_W, D_MODEL), lambda i: (0, 0)),
        ],
        out_specs=[pl.BlockSpec((tm, w_), lambda i: (i, 0)) for w_ in (ZA_W, ZB_W, ZC_W, ZD_W)],
        out_shape=[jax.ShapeDtypeStruct((rows, w_), F32) for w_ in (ZA_W, ZB_W, ZC_W, ZD_W)],
        compiler_params=pltpu.CompilerParams(dimension_semantics=("arbitrary",)),
        name="inproj",
    )(x2, g, w)


def _rwkv_kernel(z_ref, mu_ref, w0_ref, wup_ref, a0_ref, aup_ref, gup_ref, kk_ref, ka_ref, rk_ref,
                 lnw_ref, lnb_ref, o_ref, tail_ref, st_ref, y_ref, *, tc):
    L = RWKV_CHUNK
    W = MIX_W

    @pl.when(pl.program_id(1) == 0)
    def _():
        tail_ref[...] = jnp.zeros_like(tail_ref)
        st_ref[...] = jnp.zeros_like(st_ref)

    z = z_ref[...]
    zz = z + mu_ref[...] * (_shift_rows(z, tail_ref[...], 1) - z)
    tail_ref[...] = z[tc - SUBLANES:, :]

    r = zz[:, 0:W]
    k = zz[:, W:2 * W]
    v = zz[:, 2 * W:3 * W]
    o1 = 3 * W
    wd = zz[:, o1:o1 + RWKV_DECAY_RANK]
    ad = zz[:, o1 + RWKV_DECAY_RANK:o1 + RWKV_DECAY_RANK + RWKV_AAA_RANK]
    gd = zz[:, o1 + RWKV_DECAY_RANK + RWKV_AAA_RANK:]

    lw = -jnp.exp(-_softplus(-(w0_ref[...] + _bdot(jnp.tanh(wd), wup_ref[...]))) - 0.5)
    alpha = _sigmoid(a0_ref[...] + _bdot(ad, aup_ref[...]))
    g = _bdot(_sigmoid(gd), gup_ref[...])

    hsum = _head_ones(W, HEAD_DIM)
    kk = k * kk_ref[...]
    kk = kk / jnp.maximum(jnp.sqrt(_dot_split(kk * kk, hsum, 2)), 1e-12)
    k = k * (1.0 + (alpha - 1.0) * ka_ref[...])

    rt = lax.broadcasted_iota(jnp.int32, (tc, tc), 0)
    ct = lax.broadcasted_iota(jnp.int32, (tc, tc), 1)
    tri = ((rt // L == ct // L) & (ct <= rt)).astype(BF16)
    cum = _split_dot(tri, lw, 3)
    e_in = jnp.exp(cum)
    e_out = jnp.exp(-cum)
    r_t = r * e_in
    a_t = -kk * jnp.exp(cum - lw)
    b_t = kk * alpha * e_out
    k_t = k * e_out

    lane_head = lax.broadcasted_iota(jnp.int32, (1, W), 1) // HEAD_DIM

    def stack(xc):
        return jnp.concatenate([jnp.where(lane_head == h, xc, 0.0) for h in range(N_HEADS)], axis=0)

    n = N_HEADS * L
    rr = lax.broadcasted_iota(jnp.int32, (n, n), 0)
    cc = lax.broadcasted_iota(jnp.int32, (n, n), 1)
    strict = (rr % L) > (cc % L)
    incl = (rr % L) >= (cc % L)
    eye = (rr == cc).astype(F32)

    chunks = [slice(c * L, (c + 1) * L) for c in range(tc // L)]
    stk = [tuple(stack(u[sl]).astype(BF16) for u in (a_t, r_t, b_t, k_t, v)) for sl in chunks]
    a_ab = [jnp.where(strict, _dot_nt(a_s, b_s), 0.0) for a_s, r_s, b_s, k_s, v_s in stk]
    a_ak = [jnp.where(strict, _dot_nt(a_s, k_s), 0.0).astype(BF16) for a_s, r_s, b_s, k_s, v_s in stk]
    a_rb = [jnp.where(incl, _dot_nt(r_s, b_s), 0.0).astype(BF16) for a_s, r_s, b_s, k_s, v_s in stk]
    a_rk = [jnp.where(incl, _dot_nt(r_s, k_s), 0.0).astype(BF16) for a_s, r_s, b_s, k_s, v_s in stk]
    inv = [eye + m for m in a_ab]
    pw = [m.astype(BF16) for m in a_ab]
    for _ in range(int(math.log2(L)) - 1):
        pw = [_dot(m, m).astype(BF16) for m in pw]
        inv = [t_ + _dot(t_.astype(BF16), m) for t_, m in zip(inv, pw)]
    inv = [t_.astype(BF16) for t_ in inv]
    akv = [_dot(m, s_[4]).astype(BF16) for m, s_ in zip(a_ak, stk)]
    wu = [_dot(t_, jnp.concatenate([s_[0], x], axis=1)) for t_, s_, x in zip(inv, stk, akv)]
    w_m = [x[:, :W].astype(BF16) for x in wu]
    u2 = [x[:, W:].astype(BF16) for x in wu]
    r_eff = [(s_[1].astype(F32) + _dot(m, w_)).astype(BF16) for s_, m, w_ in zip(stk, a_rb, w_m)]
    y_own = [_dot(m, u_) + _dot(n_, s_[4]) for m, u_, n_, s_ in zip(a_rb, u2, a_rk, stk)]
    st_mix = [_dot_tn(w_, s_[2]).astype(BF16) for w_, s_ in zip(w_m, stk)]
    st_own = [_dot_tn(u_, s_[2]) + _dot_tn(s_[4], s_[3]) for u_, s_ in zip(u2, stk)]

    st = st_ref[...]
    for c, sl in enumerate(chunks):
        st_b = st.astype(BF16)
        y_s = _dot_nt(r_eff[c], st_b) + y_own[c]
        st = (st + _dot(st_b, st_mix[c]) + st_own[c]) * e_in[(c + 1) * L - 1:(c + 1) * L, :]
        y_c = y_s[0:L]
        for h in range(1, N_HEADS):
            y_c = y_c + y_s[h * L:(h + 1) * L]
        y_ref[sl, :] = y_c
    st_ref[...] = st

    y = y_ref[...]
    inv_n = 1.0 / HEAD_DIM
    mean = _dot_split(y, hsum, 2) * inv_n
    yc = y - mean
    var = _dot_split(yc * yc, hsum, 2) * inv_n
    y = yc * lax.rsqrt(var + RWKV_LN_EPS) * lnw_ref[...] + lnb_ref[...]
    bonus = _dot_split(r * k * rk_ref[...], hsum, 2) * v
    o_ref[...] = (y + bonus) * g


def _rwkv(za, p, bsz, t, tc):
    nt = t // tc
    full = lambda a: pl.BlockSpec(a.shape, lambda b, i: (0,) * a.ndim)
    params = [p["mu"], p["w0"], p["w_up"], p["a0"], p["a_up"], p["g_up"], p["k_k"], p["k_a"], p["r_k"],
              p["ln_w"], p["ln_b"]]
    return pl.pallas_call(
        functools.partial(_rwkv_kernel, tc=tc),
        grid=(bsz, nt),
        in_specs=[pl.BlockSpec((tc, ZA_W), lambda b, i: (b * nt + i, 0))] + [full(a) for a in params],
        out_specs=pl.BlockSpec((tc, MIX_W), lambda b, i: (b * nt + i, 0)),
        out_shape=jax.ShapeDtypeStruct((bsz * t, MIX_W), F32),
        scratch_shapes=[
            pltpu.VMEM((SUBLANES, ZA_W), F32),
            pltpu.VMEM((MIX_W, MIX_W), F32),
            pltpu.VMEM((tc, MIX_W), F32),
        ],
        compiler_params=pltpu.CompilerParams(dimension_semantics=("arbitrary", "arbitrary")),
        name="rwkv7",
    )(za, *params)


def _mlstm_kernel(z_ref, cw_ref, cb_ref, gb_ref, ng_ref, o_ref, tail_ref, c_ref, n_ref, m_ref, h_ref, *, tc):
    L = MLSTM_CHUNK
    W = MIX_W
    DK = MLSTM_DK
    DV = HEAD_DIM

    @pl.when(pl.program_id(1) == 0)
    def _():
        tail_ref[...] = jnp.zeros_like(tail_ref)
        c_ref[...] = jnp.zeros_like(c_ref)
        n_ref[...] = jnp.zeros_like(n_ref)
        m_ref[...] = jnp.zeros_like(m_ref)

    qk_in = z_ref[:, 0:W]
    v = z_ref[:, W:2 * W]
    og = z_ref[:, 2 * W:3 * W]
    gates = z_ref[:, 3 * W:3 * W + LANES]

    tail = tail_ref[...]
    conv = cb_ref[...] + cw_ref[MLSTM_CONV - 1:MLSTM_CONV, :] * qk_in
    for d in range(1, MLSTM_CONV):
        conv = conv + cw_ref[MLSTM_CONV - 1 - d:MLSTM_CONV - d, :] * _shift_rows(qk_in, tail, d)
    tail_ref[...] = qk_in[tc - SUBLANES:, :]
    qk = conv * _sigmoid(conv)
    q = qk[:, 0:N_HEADS * DK] * (DK ** -0.5)
    k = qk[:, N_HEADS * DK:]

    capped = GATE_SOFTCAP * jnp.tanh((gates + gb_ref[...]) / GATE_SOFTCAP)
    lf = _log_sigmoid(capped)

    gate_col = lax.broadcasted_iota(jnp.int32, (LANES, W), 0)
    lane_head = lax.broadcasted_iota(jnp.int32, (LANES, W), 1) // DV
    pick_i = (gate_col == lane_head).astype(BF16)
    pick_f = (gate_col == N_HEADS + lane_head).astype(BF16)
    li_e = _dot_split(capped, pick_i, 3)
    rt = lax.broadcasted_iota(jnp.int32, (tc, tc), 0)
    ct = lax.broadcasted_iota(jnp.int32, (tc, tc), 1)
    tri = ((rt // L == ct // L) & (ct <= rt)).astype(BF16)
    b_e = sum(_dot(tri, _dot(piece, pick_f).astype(BF16)) for piece in _split_bf16(lf, 3))

    key = lax.broadcasted_iota(jnp.int32, (L, W), 0)
    query = lax.broadcasted_iota(jnp.int32, (L, W), 1) % L
    on_diag = key == query
    causal_t = key <= query
    head_ones = _head_ones(W, DV)
    wide = lambda m: jnp.concatenate([m, m], axis=1)
    same_head_q = wide(lax.broadcasted_iota(jnp.int32, (N_HEADS * L, W), 0) // L
                       == lax.broadcasted_iota(jnp.int32, (N_HEADS * L, W), 1) // DV)
    same_head_k = wide(lax.broadcasted_iota(jnp.int32, (N_HEADS * DK, W), 0) // DK
                       == lax.broadcasted_iota(jnp.int32, (N_HEADS * DK, W), 1) // DV)
    q_lane_head = lax.broadcasted_iota(jnp.int32, (1, N_HEADS * DK), 1) // DK
    ones_b = jnp.ones((L, W), BF16)

    c_st = c_ref[...]
    n_st = n_ref[...]
    m_st = m_ref[0:1, :]
    for c in range(tc // L):
        sl = slice(c * L, (c + 1) * L)
        bc, lic, vc = b_e[sl], li_e[sl], v[sl]
        qc = q[sl]
        kc = k[sl].astype(BF16)
        q_stack = jnp.concatenate([jnp.where(q_lane_head == h, qc, 0.0) for h in range(N_HEADS)],
                                  axis=0).astype(BF16)
        b_q = jnp.sum(jnp.where(on_diag, bc, 0.0), axis=0, keepdims=True)
        dmat = jnp.where(causal_t, b_q - bc + lic, -jnp.inf)
        m_t = jnp.maximum(b_q + m_st, jnp.max(dmat, axis=0, keepdims=True))
        s_t = (_dot_nt(kc, q_stack) * jnp.exp(dmat - m_t)).astype(BF16)
        m_e = sum(_dot(jnp.where(on_diag, piece.astype(F32), 0.0).astype(BF16), head_ones)
                  for piece in _split_bf16(m_t, 3))
        inter = jnp.exp(bc + m_st - m_e)
        nv = _dot_tn(s_t, jnp.concatenate([vc.astype(BF16), ones_b], axis=1))
        nv = jnp.where(same_head_q, nv, 0.0)
        nv = sum(nv[h * L:(h + 1) * L] for h in range(N_HEADS))
        qcn = _dot(qc.astype(BF16), jnp.concatenate([c_st, n_st], axis=1).astype(BF16))
        num = nv[:, :W] + inter * qcn[:, :W]
        den = nv[:, W:] + inter * qcn[:, W:]
        h_ref[sl, :] = num / jnp.maximum(jnp.abs(den), jnp.exp(-m_e))

        b_last = bc[L - 1:L, :]
        gexp = b_last - bc + lic
        m_new = jnp.maximum(b_last + m_st, jnp.max(gexp, axis=0, keepdims=True))
        wts = jnp.exp(gexp - m_new)
        dec = jnp.exp(b_last + m_st - m_new)
        upd = _dot_tn(kc, jnp.concatenate([wts * vc, wts], axis=1).astype(BF16))
        upd = jnp.where(same_head_k, upd, 0.0)
        c_st = dec * c_st + upd[:, :W]
        n_st = dec * n_st + upd[:, W:]
        m_st = m_new
    c_ref[...] = c_st
    n_ref[...] = n_st
    m_ref[0:1, :] = m_st

    hv = h_ref[...]
    ms = _dot_split(hv * hv, _head_ones(W, DV), 2) * (1.0 / DV)
    o_ref[...] = hv * lax.rsqrt(ms + NORM_EPS) * ng_ref[...] * _sigmoid(og)


def _mlstm(zb, p, bsz, t, tc):
    nt = t // tc
    full = lambda a: pl.BlockSpec(a.shape, lambda b, i: (0,) * a.ndim)
    params = [p["conv_w"], p["conv_b"], p["gate_b"], p["norm_g"]]
    return pl.pallas_call(
        functools.partial(_mlstm_kernel, tc=tc),
        grid=(bsz, nt),
        in_specs=[pl.BlockSpec((tc, ZB_W), lambda b, i: (b * nt + i, 0))] + [full(a) for a in params],
        out_specs=pl.BlockSpec((tc, MIX_W), lambda b, i: (b * nt + i, 0)),
        out_shape=jax.ShapeDtypeStruct((bsz * t, MIX_W), F32),
        scratch_shapes=[
            pltpu.VMEM((SUBLANES, MIX_W), F32),
            pltpu.VMEM((N_HEADS * MLSTM_DK, MIX_W), F32),
            pltpu.VMEM((N_HEADS * MLSTM_DK, MIX_W), F32),
            pltpu.VMEM((SUBLANES, MIX_W), F32),
            pltpu.VMEM((tc, MIX_W), F32),
        ],
        compiler_params=pltpu.CompilerParams(dimension_semantics=("arbitrary", "arbitrary")),
        name="mlstm",
    )(zb, *params)


SWA_SUB_BLOCKS = 4


def _t5_bucket(dist):
    max_exact = REL_BUCKETS // 2
    d = np.maximum(dist, 1).astype(np.float32)
    large = max_exact + (np.log(d / max_exact) / math.log(REL_MAX_DIST / max_exact)
                         * (REL_BUCKETS - max_exact)).astype(np.int32)
    large = np.minimum(large, REL_BUCKETS - 1)
    return np.where(dist < max_exact, dist, large).astype(np.int32)


def _swa_kernel(rb_ref, sink_ref, bucket_ref, q_ref, kp_ref, kc_ref, vp_ref, vc_ref, o_ref, bias_ref):
    blk = ATTN_BLOCK
    grp = N_HEADS // SWA_KV_HEADS
    kvw = SWA_KV_HEADS * HEAD_DIM
    first = (pl.program_id(0) == 0) & (pl.program_id(1) == 0)

    @pl.when(first)
    def _():
        bucket = bucket_ref[...]
        for h in range(N_HEADS):
            acc = jnp.full((2 * blk, blk), -jnp.inf, F32)
            for bk in range(REL_BUCKETS):
                acc = jnp.where(bucket == bk, rb_ref[bk, h], acc)
            bias_ref[h // grp, :, (h % grp) * blk:(h % grp + 1) * blk] = acc

    key = lax.broadcasted_iota(jnp.int32, (2 * blk, grp * blk), 0)
    live = (key >= blk) | (pl.program_id(1) > 0)
    member = lax.broadcasted_iota(jnp.int32, (1, grp * blk), 1) // blk
    kw = jnp.concatenate([kp_ref[...], kc_ref[...]], axis=0).astype(BF16)
    vw = jnp.concatenate([vp_ref[...], vc_ref[...]], axis=0).astype(BF16)
    lane_member = lax.broadcasted_iota(jnp.int32, (1, grp * HEAD_DIM), 1) // HEAD_DIM
    rr = lax.broadcasted_iota(jnp.int32, (kvw, grp * HEAD_DIM), 0)
    cc = lax.broadcasted_iota(jnp.int32, (kvw, grp * HEAD_DIM), 1)
    vr = lax.broadcasted_iota(jnp.int32, (HEAD_DIM, kvw), 0)
    vc_ = lax.broadcasted_iota(jnp.int32, (HEAD_DIM, kvw), 1)
    n_sub = q_ref.shape[0] // blk
    k_rep, v_t, sinks = [], [], []
    for j in range(SWA_KV_HEADS):
        spread = ((rr // HEAD_DIM == j) & (rr % HEAD_DIM == cc % HEAD_DIM)).astype(BF16)
        pick = ((vc_ // HEAD_DIM == j) & (vc_ % HEAD_DIM == vr)).astype(BF16)
        k_rep.append(_dot(kw, spread).astype(BF16))
        v_t.append(_dot_nt(pick, vw).astype(BF16))
        sinks.append(jnp.where(member == 0, sink_ref[j * grp], sink_ref[j * grp + 1]))
    pairs = [(n, j) for n in range(n_sub) for j in range(SWA_KV_HEADS)]
    scores = []
    for n, j in pairs:
        qp = q_ref[n * blk:(n + 1) * blk, j * grp * HEAD_DIM:(j + 1) * grp * HEAD_DIM] * (HEAD_DIM ** -0.5)
        q_stack = jnp.concatenate([jnp.where(lane_member == g, qp, 0.0) for g in range(grp)],
                                  axis=0).astype(BF16)
        s = _dot_nt(k_rep[j][n * blk:(n + 2) * blk], q_stack) + bias_ref[j]
        scores.append(jnp.where(live, s, -jnp.inf) if n == 0 else s)
    probs = []
    for (n, j), s in zip(pairs, scores):
        m = jnp.maximum(jnp.max(s, axis=0, keepdims=True), sinks[j])
        p = jnp.exp(s - m)
        denom = jnp.sum(p, axis=0, keepdims=True) + jnp.exp(sinks[j] - m)
        probs.append((p / denom).astype(BF16))
    outs = [_dot(v_t[j][:, n * blk:(n + 2) * blk], p) for (n, j), p in zip(pairs, probs)]
    for n in range(n_sub):
        heads = [outs[n * SWA_KV_HEADS + j][:, g * blk:(g + 1) * blk]
                 for j in range(SWA_KV_HEADS) for g in range(grp)]
        o_ref[n * blk:(n + 1) * blk, :] = jnp.concatenate(heads, axis=0).T


def _swa(zc, rel_bias, sinks, bsz, t):
    blk = ATTN_BLOCK
    nb = t // blk
    assert N_HEADS // SWA_KV_HEADS == 2
    dist = np.arange(blk)[None, :] + blk - np.arange(2 * blk)[:, None]
    bucket = jnp.asarray(np.where((dist >= 0) & (dist < SWA_WINDOW),
                                  _t5_bucket(np.clip(dist, 0, SWA_WINDOW - 1)), -1).astype(np.int32))
    kvw = SWA_KV_HEADS * HEAD_DIM
    kcol = MIX_W // kvw
    n_sub = min(SWA_SUB_BLOCKS, nb)
    assert nb % n_sub == 0
    ns = nb // n_sub
    rows = n_sub * blk
    cur = lambda b, i: b * ns + i
    prev = lambda b, i: b * nb + jnp.maximum(i * n_sub - 1, 0)
    return pl.pallas_call(
        _swa_kernel,
        grid=(bsz, ns),
        in_specs=[
            pl.BlockSpec(memory_space=pltpu.SMEM),
            pl.BlockSpec(memory_space=pltpu.SMEM),
            pl.BlockSpec((2 * blk, blk), lambda b, i: (0, 0)),
            pl.BlockSpec((rows, MIX_W), lambda b, i: (cur(b, i), 0)),
            pl.BlockSpec((blk, kvw), lambda b, i: (prev(b, i), kcol)),
            pl.BlockSpec((rows, kvw), lambda b, i: (cur(b, i), kcol)),
            pl.BlockSpec((blk, kvw), lambda b, i: (prev(b, i), kcol + 1)),
            pl.BlockSpec((rows, kvw), lambda b, i: (cur(b, i), kcol + 1)),
        ],
        out_specs=pl.BlockSpec((rows, MIX_W), lambda b, i: (cur(b, i), 0)),
        out_shape=jax.ShapeDtypeStruct((bsz * t, MIX_W), F32),
        scratch_shapes=[pltpu.VMEM((SWA_KV_HEADS, 2 * blk, 2 * blk), F32)],
        compiler_params=pltpu.CompilerParams(dimension_semantics=("arbitrary", "arbitrary")),
        name="swa",
    )(rel_bias, sinks, bucket, zc, zc, zc, zc, zc)


FOX_BLOCK = 256
FOX_AUG = LANES
FOX_PARTS = 3


def _fox_placements():
    wide = N_HEADS * FOX_AUG
    pk = np.zeros((MIX_W, wide), np.float32)
    pck = np.zeros((LANES, wide), np.float32)
    ones_k = np.zeros((SUBLANES, wide), np.float32)
    pq_t = np.zeros((wide, MIX_W), np.float32)
    pv_t = np.zeros((wide, MIX_W), np.float32)
    pcq_t = np.zeros((wide, LANES), np.float32)
    ones_t = np.zeros((2, wide, LANES), np.float32)
    for h in range(N_HEADS):
        base = h * FOX_AUG
        for d in range(HEAD_DIM):
            pk[h * HEAD_DIM + d, base + d] = 1.0
            pq_t[base + d, h * HEAD_DIM + d] = HEAD_DIM ** -0.5
            pv_t[base + d, h * HEAD_DIM + d] = 1.0
        for n in range(FOX_PARTS):
            pcq_t[base + HEAD_DIM + n, n * N_HEADS + h] = 1.0
            pck[n * N_HEADS + h, base + HEAD_DIM + FOX_PARTS + n] = -1.0
            ones_t[0, base + HEAD_DIM + FOX_PARTS + n, :] = 1.0
            ones_k[0, base + HEAD_DIM + n] = 1.0
        ones_t[1, base + HEAD_DIM, :] = 1.0
    bf = lambda a: jnp.asarray(a, BF16)
    return bf(pk), bf(pck), jnp.asarray(ones_k), bf(np.stack([pq_t, pv_t])), bf(pcq_t), jnp.asarray(ones_t)


def _fox_kernel(q_ref, k_ref, v_ref, f_ref, fb_ref, pk_ref, pck_ref, onesk_ref, pqv_ref, pcq_ref, onest_ref,
                o_ref, kaug_ref, vaug_ref, m_ref, acc_ref, clast_ref, s_ref, p_ref):
    blk = FOX_BLOCK
    i = pl.program_id(1)

    @pl.when(i == 0)
    def _():
        clast_ref[...] = jnp.zeros_like(clast_ref)

    ls = _log_sigmoid(f_ref[...] + fb_ref[...])
    row = lax.broadcasted_iota(jnp.int32, (blk, blk), 0)
    col = lax.broadcasted_iota(jnp.int32, (blk, blk), 1)
    cq = _split_dot((col <= row).astype(BF16), ls, FOX_PARTS) + clast_ref[0:1, :]
    clast_ref[0:1, :] = cq[blk - 1:blk, :]

    widen = lambda a: jnp.concatenate([a] * (blk // LANES), axis=1)
    qa = _dot_nt(pqv_ref[0], q_ref[...].astype(BF16)) + widen(onest_ref[0])
    va = _dot_nt(pqv_ref[1], v_ref[...].astype(BF16)) + widen(onest_ref[1])
    ka = _dot(k_ref[...].astype(BF16), pk_ref[...]) + onesk_ref[0:1, :]
    lane = lax.broadcasted_iota(jnp.int32, (1, LANES), 1)
    pieces = _split_bf16(jnp.where(lane < N_HEADS, cq, 0.0), FOX_PARTS)
    packed = pieces[0].astype(F32)
    for n in range(1, FOX_PARTS):
        packed = packed + pltpu.roll(pieces[n].astype(F32), n * N_HEADS, axis=1)
    packed = packed.astype(BF16)
    qa = (qa + _dot_nt(pcq_ref[...], packed)).astype(BF16)
    ka = ka + _dot(packed, pck_ref[...])
    row0 = pl.multiple_of(i * blk, blk)
    for h in range(N_HEADS):
        hs = slice(h * FOX_AUG, (h + 1) * FOX_AUG)
        kaug_ref[h, pl.ds(row0, blk), :] = ka[:, hs].astype(BF16)
        vaug_ref[h, :, pl.ds(row0, blk)] = va[hs, :].astype(BF16)

    key_le_query = row <= col

    def attend(off, diagonal):
        for h in range(N_HEADS):
            s = _dot(kaug_ref[h, pl.ds(off, blk), :], qa[h * FOX_AUG:(h + 1) * FOX_AUG, :])
            s_ref[h] = jnp.where(key_le_query, s, -jnp.inf) if diagonal else s
        m_new = [jnp.maximum(m_ref[h], jnp.max(s_ref[h], axis=0, keepdims=True)) for h in range(N_HEADS)]
        for h in range(N_HEADS):
            p_ref[h] = jnp.exp(s_ref[h] - m_new[h]).astype(BF16)
        for h in range(N_HEADS):
            pv = _dot(vaug_ref[h, :, pl.ds(off, blk)], p_ref[h])
            acc_ref[h] = jnp.exp(m_ref[h] - m_new[h]) * acc_ref[h] + pv
            m_ref[h] = m_new[h]

    m_ref[...] = jnp.full_like(m_ref, -jnp.inf)
    acc_ref[...] = jnp.zeros_like(acc_ref)
    attend(row0, True)

    def body(j, carry):
        attend(pl.multiple_of(j * blk, blk), False)
        return carry

    lax.fori_loop(0, i, body, 0)
    outs = []
    for h in range(N_HEADS):
        acc = acc_ref[h]
        outs.append(acc[0:HEAD_DIM, :] / acc[HEAD_DIM:HEAD_DIM + 1, :])
    o_ref[...] = jnp.concatenate(outs, axis=0).T


def _fox(zd, fb, bsz, t):
    blk = FOX_BLOCK
    assert t % blk == 0
    nb = t // blk
    consts = _fox_placements()
    const = lambda a: pl.BlockSpec(a.shape, lambda b, i: (0,) * a.ndim)
    cur = lambda c: (lambda b, i: (b * nb + i, c))
    return pl.pallas_call(
        _fox_kernel,
        grid=(bsz, nb),
        in_specs=[
            pl.BlockSpec((blk, MIX_W), cur(0)),
            pl.BlockSpec((blk, MIX_W), cur(1)),
            pl.BlockSpec((blk, MIX_W), cur(2)),
            pl.BlockSpec((blk, LANES), cur(3 * MIX_W // LANES)),
            const(fb)] + [const(a) for a in consts],
        out_specs=pl.BlockSpec((blk, MIX_W), cur(0)),
        out_shape=jax.ShapeDtypeStruct((bsz * t, MIX_W), F32),
        scratch_shapes=[
            pltpu.VMEM((N_HEADS, t, FOX_AUG), BF16),
            pltpu.VMEM((N_HEADS, FOX_AUG, t), BF16),
            pltpu.VMEM((N_HEADS, 1, blk), F32),
            pltpu.VMEM((N_HEADS, FOX_AUG, blk), F32),
            pltpu.VMEM((SUBLANES, LANES), F32),
            pltpu.VMEM((N_HEADS, blk, blk), F32),
            pltpu.VMEM((N_HEADS, blk, blk), BF16),
        ],
        compiler_params=pltpu.CompilerParams(dimension_semantics=("arbitrary", "arbitrary")),
        name="fox",
    )(zd, zd, zd, zd, fb, *consts)


def _outproj_kernel(x_ref, ya_ref, yb_ref, yc_ref, yd_ref, w_ref, g_ref, o_ref):
    acc = None
    for n, ref in enumerate((ya_ref, yb_ref, yc_ref, yd_ref)):
        part = _dot(ref[...].astype(BF16), w_ref[n * MIX_W:(n + 1) * MIX_W, :])
        acc = part if acc is None else acc + part
    o_ref[...] = x_ref[...] + acc * _rms_scale(acc) * g_ref[...]


def _outproj(x2, ys, w, g, tm):
    rows = x2.shape[0]
    return pl.pallas_call(
        _outproj_kernel,
        grid=(rows // tm,),
        in_specs=[pl.BlockSpec((tm, D_MODEL), lambda i: (i, 0))]
        + [pl.BlockSpec((tm, MIX_W), lambda i: (i, 0))] * 4
        + [pl.BlockSpec((N_HEADS * MIX_W, D_MODEL), lambda i: (0, 0)),
           pl.BlockSpec((1, D_MODEL), lambda i: (0, 0))],
        out_specs=pl.BlockSpec((tm, D_MODEL), lambda i: (i, 0)),
        out_shape=jax.ShapeDtypeStruct((rows, D_MODEL), F32),
        compiler_params=pltpu.CompilerParams(dimension_semantics=("arbitrary",)),
        name="outproj",
    )(x2, *ys, w, g)


MXU_COLS = 256
FFN_COL_CHUNK = 6 * MXU_COLS


def _ffn_kernel(x_ref, gpre_ref, wg_ref, wu_ref, cw_ref, cb_ref, wd_ref, gpost_ref, o_ref, tail_ref, *, tm, t):
    x = x_ref[...]
    h = (x * _rms_scale(x) * gpre_ref[...]).astype(BF16)
    seq_start = (pl.program_id(0) * tm) % t == 0

    @pl.when(pl.program_id(0) == 0)
    def _():
        tail_ref[...] = jnp.zeros_like(tail_ref)

    acc = None
    for c0 in range(0, D_FF, FFN_COL_CHUNK):
        cs = slice(c0, min(c0 + FFN_COL_CHUNK, D_FF))
        gate = _dot(h, wg_ref[:, cs])
        halo = jnp.where(seq_start, 0.0, tail_ref[:, cs])
        tail_ref[:, cs] = gate[tm - SUBLANES:, :]
        conv = cb_ref[:, cs] + cw_ref[FFN_CONV - 1:FFN_CONV, cs] * gate
        for d in range(1, FFN_CONV):
            conv = conv + cw_ref[FFN_CONV - 1 - d:FFN_CONV - d, cs] * _shift_rows(gate, halo, d)
        f = jax.nn.gelu(conv, approximate=True) * _dot(h, wu_ref[:, cs])
        part = _dot(f.astype(BF16), wd_ref[cs, :])
        acc = part if acc is None else acc + part
    o_ref[...] = x + acc * _rms_scale(acc) * gpost_ref[...]


def _ffn(x2, p, t, tm):
    rows = x2.shape[0]
    const = lambda a: pl.BlockSpec(a.shape, lambda i: (0, 0), pipeline_mode=pl.Buffered(1))
    params = [p["g_pre"], p["w_gate"], p["w_up"], p["conv_w"], p["conv_b"], p["w_down"], p["g_post"]]
    return pl.pallas_call(
        functools.partial(_ffn_kernel, tm=tm, t=t),
        grid=(rows // tm,),
        in_specs=[pl.BlockSpec((tm, D_MODEL), lambda i: (i, 0))] + [const(a) for a in params],
        out_specs=pl.BlockSpec((tm, D_MODEL), lambda i: (i, 0)),
        out_shape=jax.ShapeDtypeStruct((rows, D_MODEL), F32),
        scratch_shapes=[pltpu.VMEM((SUBLANES, D_FF), F32)],
        compiler_params=pltpu.CompilerParams(dimension_semantics=("arbitrary",),
                                             vmem_limit_bytes=56 * 1024 * 1024),
        name="ffn",
    )(x2, *params)


def _pad_cols(a, width):
    return jnp.pad(a, ((0, 0), (0, width - a.shape[1])))


def _row(a):
    return a.reshape(1, -1).astype(F32)


def _layer_params(l, w_in, w_out, norm_mix_pre, norm_mix_post, norm_ffn_pre, norm_ffn_post,
                  rwkv_mu, rwkv_w0, rwkv_w_up, rwkv_a0, rwkv_a_up, rwkv_g_up, rwkv_k_k, rwkv_k_a,
                  rwkv_r_k, rwkv_ln_w, rwkv_ln_b, mlstm_conv_w, mlstm_conv_b, mlstm_b_i, mlstm_b_f,
                  mlstm_norm, swa_sinks, fox_b_f, ffn_w_up, ffn_conv_w, ffn_conv_b, ffn_w_down):
    wa_w = ZA_W
    wb_w = 3 * MIX_W + 2 * N_HEADS
    wc_w = ZC_W
    wi = jnp.transpose(w_in, (2, 0, 1))[:, l, :].astype(BF16)
    pad_rows = lambda a, n: jnp.pad(a, ((0, n - a.shape[0]), (0, 0)))
    groups = [wi[:wa_w],
              pad_rows(wi[wa_w:wa_w + wb_w], ZB_W),
              wi[wa_w + wb_w:wa_w + wb_w + wc_w],
              pad_rows(wi[wa_w + wb_w + wc_w:], ZD_W)]
    gate_b = _pad_cols(jnp.concatenate([mlstm_b_i[l], mlstm_b_f[l]]).reshape(1, -1), LANES)
    return {
        "w_in": jnp.concatenate(groups, axis=0),
        "g_mix_pre": _row(norm_mix_pre[l]),
        "g_mix_post": _row(norm_mix_post[l]),
        "w_out": w_out[l].astype(BF16),
        "rwkv": {"mu": _row(rwkv_mu[l]), "w0": _row(rwkv_w0[l]), "w_up": rwkv_w_up[l].astype(BF16),
                 "a0": _row(rwkv_a0[l]), "a_up": rwkv_a_up[l].astype(BF16), "g_up": rwkv_g_up[l].astype(BF16),
                 "k_k": _row(rwkv_k_k[l]), "k_a": _row(rwkv_k_a[l]), "r_k": _row(rwkv_r_k[l]),
                 "ln_w": _row(rwkv_ln_w[l]), "ln_b": _row(rwkv_ln_b[l])},
        "mlstm": {"conv_w": mlstm_conv_w[l], "conv_b": _row(mlstm_conv_b[l]), "gate_b": gate_b,
                  "norm_g": _row(mlstm_norm[l])},
        "swa_sinks": swa_sinks[l],
        "fox_b": _pad_cols(fox_b_f[l].reshape(1, -1), LANES),
        "ffn": {"g_pre": _row(norm_ffn_pre[l]), "w_gate": ffn_w_up[l][:, :D_FF].astype(BF16),
                "w_up": ffn_w_up[l][:, D_FF:].astype(BF16), "conv_w": ffn_conv_w[l],
                "conv_b": _row(ffn_conv_b[l]), "w_down": ffn_w_down[l].astype(BF16),
                "g_post": _row(norm_ffn_post[l])},
    }


def _tiles(t):
    return min(512, t), min(256, t), min(512, t)


def kernel(x, w_in, w_out, norm_mix_pre, norm_mix_post, norm_ffn_pre, norm_ffn_post, rwkv_mu, rwkv_w0, rwkv_w_up, rwkv_a0, rwkv_a_up, rwkv_g_up, rwkv_k_k, rwkv_k_a, rwkv_r_k, rwkv_ln_w, rwkv_ln_b, mlstm_conv_w, mlstm_conv_b, mlstm_b_i, mlstm_b_f, mlstm_norm, swa_sinks, fox_b_f, rel_bias, ffn_w_up, ffn_conv_w, ffn_conv_b, ffn_w_down):
    bsz, t, d = x.shape
    assert d == D_MODEL and t % ATTN_BLOCK == 0
    tm, tc, tf = _tiles(t)
    x2 = x.reshape(bsz * t, d)
    for l in range(w_in.shape[0]):
        p = _layer_params(l, w_in, w_out, norm_mix_pre, norm_mix_post, norm_ffn_pre, norm_ffn_post,
                          rwkv_mu, rwkv_w0, rwkv_w_up, rwkv_a0, rwkv_a_up, rwkv_g_up, rwkv_k_k, rwkv_k_a,
                          rwkv_r_k, rwkv_ln_w, rwkv_ln_b, mlstm_conv_w, mlstm_conv_b, mlstm_b_i, mlstm_b_f,
                          mlstm_norm, swa_sinks, fox_b_f, ffn_w_up, ffn_conv_w, ffn_conv_b, ffn_w_down)
        za, zb, zc, zd = _inproj(x2, p["g_mix_pre"], p["w_in"], tm)
        ya = _rwkv(za, p["rwkv"], bsz, t, tc)
        yb = _mlstm(zb, p["mlstm"], bsz, t, tc)
        yc = _swa(zc, rel_bias, p["swa_sinks"], bsz, t)
        yd = _fox(zd, p["fox_b"], bsz, t)
        x2 = _outproj(x2, (ya, yb, yc, yd), p["w_out"], p["g_mix_post"], tm)
        x2 = _ffn(x2, p["ffn"], t, tf)
    return x2.reshape(bsz, t, d)
```

```python
import functools
import math

import jax
import jax.numpy as jnp
import numpy as np
from jax import lax
from jax.experimental import pallas as pl
from jax.experimental.pallas import tpu as pltpu

F32 = jnp.float32
BF16 = jnp.bfloat16

D_MODEL = 1024
HEAD_DIM = 64
N_HEADS = 4
MIX_W = N_HEADS * HEAD_DIM
RWKV_DECAY_RANK = 64
RWKV_AAA_RANK = 64
RWKV_GATE_RANK = 128
RWKV_LN_EPS = 64e-5
RWKV_CHUNK = 64
MLSTM_DK = 32
MLSTM_CONV = 4
MLSTM_CHUNK = 64
GATE_SOFTCAP = 15.0
SWA_KV_HEADS = 2
SWA_WINDOW = 128
ATTN_BLOCK = 128
REL_BUCKETS = 32
REL_MAX_DIST = 128
D_FF = 2816
FFN_CONV = 3
NORM_EPS = 1e-6

LANES = 128
SUBLANES = 8
MXU_COLS = 256

ZA_W = 3 * MIX_W + RWKV_DECAY_RANK + RWKV_AAA_RANK + RWKV_GATE_RANK
ZB_W = 3 * MIX_W + LANES
ZC_W = MIX_W + 2 * SWA_KV_HEADS * HEAD_DIM
ZD_W = 3 * MIX_W + LANES
Z_W = ZA_W + ZB_W + ZC_W + ZD_W

def _split_bf16(x, parts):
    out = []
    for n in range(parts):
        piece = x.astype(BF16)
        out.append(piece)
        if n + 1 < parts:
            x = x - piece.astype(F32)
    return out


def _dot(a, b, precision=None):
    return jnp.dot(a, b, preferred_element_type=F32, precision=precision)


def _dot_nt(a, b, precision=None):
    return lax.dot_general(a, b, (((1,), (1,)), ((), ())), preferred_element_type=F32, precision=precision)


def _dot_tn(a, b, precision=None):
    return lax.dot_general(a, b, (((0,), (0,)), ((), ())), preferred_element_type=F32, precision=precision)


def _dot_split(x, ones, parts):
    return sum(_dot(piece, ones) for piece in _split_bf16(x, parts))


def _split_dot(ones, x, parts):
    return sum(_dot(ones, piece) for piece in _split_bf16(x, parts))


def _bdot(a, b):
    return _dot(a.astype(BF16), b.astype(BF16))


def _bdot_nt(a, b):
    return _dot_nt(a.astype(BF16), b.astype(BF16))


def _bdot_tn(a, b):
    return _dot_tn(a.astype(BF16), b.astype(BF16))


def _sigmoid(x):
    return 1.0 / (1.0 + jnp.exp(-x))


def _log_sigmoid(x):
    return jnp.minimum(x, 0.0) - jnp.log(1.0 + jnp.exp(-jnp.abs(x)))


def _softplus(x):
    return jnp.maximum(x, 0.0) + jnp.log(1.0 + jnp.exp(-jnp.abs(x)))


def _rms_scale(x):
    return lax.rsqrt(jnp.mean(x * x, axis=-1, keepdims=True) + NORM_EPS)


def _shift_rows(x, prev_tail, d):
    rolled = pltpu.roll(x, d, axis=0)
    head_rows = lax.broadcasted_iota(jnp.int32, (SUBLANES, 1), 0)
    head = jnp.where(head_rows < d, pltpu.roll(prev_tail, d, axis=0), rolled[:SUBLANES])
    return jnp.concatenate([head, rolled[SUBLANES:]], axis=0)


def _idiv(x, n):
    assert n & (n - 1) == 0
    return lax.shift_right_logical(x, jnp.int32(n.bit_length() - 1))


def _imod(x, n):
    assert n & (n - 1) == 0
    return lax.bitwise_and(x, jnp.int32(n - 1))


def _head_ones(width, head):
    r = _idiv(lax.broadcasted_iota(jnp.int32, (width, width), 0), head)
    c = _idiv(lax.broadcasted_iota(jnp.int32, (width, width), 1), head)
    return (r == c).astype(BF16)


def _inproj_kernel(x_ref, g_ref, w_ref, za_ref, zb_ref, zc_ref, zd_ref):
    x = x_ref[...]
    h = (x * _rms_scale(x) * g_ref[...]).astype(BF16)
    z = _dot_nt(h, w_ref[...])
    off = 0
    for ref in (za_ref, zb_ref, zc_ref, zd_ref):
        w = ref.shape[1]
        ref[...] = z[:, off:off + w]
        off += w


def _inproj(x2, g, w, tm):
    rows = x2.shape[0]
    return pl.pallas_call(
        _inproj_kernel,
        grid=(rows // tm,),
        in_specs=[
            pl.BlockSpec((tm, D_MODEL), lambda i: (i, 0)),
            pl.BlockSpec((1, D_MODEL), lambda i: (0, 0)),
            pl.BlockSpec((Z_W, D_MODEL), lambda i: (0, 0)),
        ],
        out_specs=[pl.BlockSpec((tm, w_), lambda i: (i, 0)) for w_ in (ZA_W, ZB_W, ZC_W, ZD_W)],
        out_shape=[jax.ShapeDtypeStruct((rows, w_), F32) for w_ in (ZA_W, ZB_W, ZC_W, ZD_W)],
        compiler_params=pltpu.CompilerParams(dimension_semantics=("arbitrary",)),
        name="inproj",
    )(x2, g, w)


def _rwkv_kernel(z_ref, mu_ref, w0_ref, wup_ref, a0_ref, aup_ref, gup_ref, kk_ref, ka_ref, rk_ref,
                 lnw_ref, lnb_ref, o_ref, tail_ref, st_ref, y_ref, *, tc):
    L = RWKV_CHUNK
    W = MIX_W

    @pl.when(pl.program_id(1) == 0)
    def _():
        tail_ref[...] = jnp.zeros_like(tail_ref)
        st_ref[...] = jnp.zeros_like(st_ref)

    z = z_ref[...]
    zz = z + mu_ref[...] * (_shift_rows(z, tail_ref[...], 1) - z)
    tail_ref[...] = z[tc - SUBLANES:, :]

    r = zz[:, 0:W]
    k = zz[:, W:2 * W]
    v = zz[:, 2 * W:3 * W]
    o1 = 3 * W
    wd = zz[:, o1:o1 + RWKV_DECAY_RANK]
    ad = zz[:, o1 + RWKV_DECAY_RANK:o1 + RWKV_DECAY_RANK + RWKV_AAA_RANK]
    gd = zz[:, o1 + RWKV_DECAY_RANK + RWKV_AAA_RANK:]

    lw = -jnp.exp(-_softplus(-(w0_ref[...] + _bdot(jnp.tanh(wd), wup_ref[...]))) - 0.5)
    alpha = _sigmoid(a0_ref[...] + _bdot(ad, aup_ref[...]))
    g = _bdot(_sigmoid(gd), gup_ref[...])

    hsum = _head_ones(W, HEAD_DIM)
    kk = k * kk_ref[...]
    kk = kk * lax.rsqrt(jnp.maximum(_dot_split(kk * kk, hsum, 2), 1e-24))
    k = k * (1.0 + (alpha - 1.0) * ka_ref[...])

    span = min(tc, MXU_COLS)
    rt = lax.broadcasted_iota(jnp.int32, (span, span), 0)
    ct = lax.broadcasted_iota(jnp.int32, (span, span), 1)
    tri = ((_idiv(rt, L) == _idiv(ct, L)) & (ct <= rt)).astype(BF16)
    cum = jnp.concatenate([_split_dot(tri, lw[r0:r0 + span], 3) for r0 in range(0, tc, span)], axis=0)
    e_in = jnp.exp(cum)
    e_out = jnp.exp(-cum)
    r_t = r * e_in
    a_t = -kk * jnp.exp(cum - lw)
    b_t = kk * alpha * e_out
    k_t = k * e_out

    lane_head = lax.broadcasted_iota(jnp.int32, (1, W), 1) // HEAD_DIM

    def stack(xc):
        return jnp.concatenate([jnp.where(lane_head == h, xc, 0.0) for h in range(N_HEADS)], axis=0)

    n = N_HEADS * L
    rr = lax.broadcasted_iota(jnp.int32, (n, n), 0)
    cc = lax.broadcasted_iota(jnp.int32, (n, n), 1)
    own_head = _idiv(rr, L) == _idiv(cc, L)
    strict = own_head & (_imod(rr, L) > _imod(cc, L))
    incl = own_head & (_imod(rr, L) >= _imod(cc, L))
    eye = (rr == cc).astype(F32)

    chunks = [slice(c * L, (c + 1) * L) for c in range(tc // L)]
    stk = [tuple(stack(u[sl]).astype(BF16) for u in (a_t, r_t, b_t, k_t, v)) for sl in chunks]
    bk_rep = [jnp.concatenate([b_t[sl]] * 2 + [k_t[sl]] * 2, axis=0).astype(BF16) for sl in chunks]
    ar = [_dot_nt(jnp.concatenate([s_[0], s_[1]], axis=0), x) for s_, x in zip(stk, bk_rep)]
    over_heads = lambda x: jnp.concatenate([x, x], axis=1)
    a_ab = [jnp.where(strict, over_heads(x[:n, :2 * L]), 0.0) for x in ar]
    a_ak = [jnp.where(strict, over_heads(x[:n, 2 * L:]), 0.0).astype(BF16) for x in ar]
    a_rb = [jnp.where(incl, over_heads(x[n:, :2 * L]), 0.0).astype(BF16) for x in ar]
    a_rk = [jnp.where(incl, over_heads(x[n:, 2 * L:]), 0.0).astype(BF16) for x in ar]
    inv = [eye + m for m in a_ab]
    pw = [m.astype(BF16) for m in a_ab]
    for _ in range(int(math.log2(L)) - 1):
        pw = [_dot(m, m).astype(BF16) for m in pw]
        inv = [t_ + _dot(t_.astype(BF16), m) for t_, m in zip(inv, pw)]
    inv = [t_.astype(BF16) for t_ in inv]
    akv = [_dot(m, s_[4]).astype(BF16) for m, s_ in zip(a_ak, stk)]
    wu = [_dot(t_, jnp.concatenate([s_[0], x], axis=1)) for t_, s_, x in zip(inv, stk, akv)]
    w_m = [x[:, :W].astype(BF16) for x in wu]
    u2 = [x[:, W:].astype(BF16) for x in wu]
    r_eff = [(s_[1].astype(F32) + _dot(m, w_)).astype(BF16) for s_, m, w_ in zip(stk, a_rb, w_m)]
    y_own = [_dot(m, u_) + _dot(n_, s_[4]) for m, u_, n_, s_ in zip(a_rb, u2, a_rk, stk)]
    st_mix = [_dot_tn(w_, s_[2]).astype(BF16) for w_, s_ in zip(w_m, stk)]
    st_own = [_dot_tn(u_, s_[2]) + _dot_tn(s_[4], s_[3]) for u_, s_ in zip(u2, stk)]

    st = st_ref[...]
    for c, sl in enumerate(chunks):
        st_b = st.astype(BF16)
        y_s = _dot_nt(r_eff[c], st_b) + y_own[c]
        st = (st + _dot(st_b, st_mix[c]) + st_own[c]) * e_in[(c + 1) * L - 1:(c + 1) * L, :]
        y_c = y_s[0:L]
        for h in range(1, N_HEADS):
            y_c = y_c + y_s[h * L:(h + 1) * L]
        y_ref[sl, :] = y_c
    st_ref[...] = st

    y = y_ref[...]
    inv_n = 1.0 / HEAD_DIM
    mean = _dot_split(y, hsum, 2) * inv_n
    yc = y - mean
    var = _dot_split(yc * yc, hsum, 2) * inv_n
    y = yc * lax.rsqrt(var + RWKV_LN_EPS) * lnw_ref[...] + lnb_ref[...]
    bonus = _dot_split(r * k * rk_ref[...], hsum, 2) * v
    o_ref[...] = (y + bonus) * g


def _rwkv(za, p, bsz, t, tc):
    nt = t // tc
    full = lambda a: pl.BlockSpec(a.shape, lambda b, i: (0,) * a.ndim)
    params = [p["mu"], p["w0"], p["w_up"], p["a0"], p["a_up"], p["g_up"], p["k_k"], p["k_a"], p["r_k"],
              p["ln_w"], p["ln_b"]]
    return pl.pallas_call(
        functools.partial(_rwkv_kernel, tc=tc),
        grid=(bsz, nt),
        in_specs=[pl.BlockSpec((tc, ZA_W), lambda b, i: (b * nt + i, 0))] + [full(a) for a in params],
        out_specs=pl.BlockSpec((tc, MIX_W), lambda b, i: (b * nt + i, 0)),
        out_shape=jax.ShapeDtypeStruct((bsz * t, MIX_W), F32),
        scratch_shapes=[
            pltpu.VMEM((SUBLANES, ZA_W), F32),
            pltpu.VMEM((MIX_W, MIX_W), F32),
            pltpu.VMEM((tc, MIX_W), F32),
        ],
        compiler_params=pltpu.CompilerParams(dimension_semantics=("arbitrary", "arbitrary")),
        name="rwkv7",
    )(za, *params)


def _mlstm_kernel(z_ref, cw_ref, cb_ref, gb_ref, ng_ref, o_ref, tail_ref, c_ref, n_ref, m_ref, h_ref, *, tc):
    L = MLSTM_CHUNK
    W = MIX_W
    DK = MLSTM_DK
    DV = HEAD_DIM

    @pl.when(pl.program_id(1) == 0)
    def _():
        tail_ref[...] = jnp.zeros_like(tail_ref)
        c_ref[...] = jnp.zeros_like(c_ref)
        n_ref[...] = jnp.zeros_like(n_ref)
        m_ref[...] = jnp.zeros_like(m_ref)

    qk_in = z_ref[:, 0:W]
    v = z_ref[:, W:2 * W]
    og = z_ref[:, 2 * W:3 * W]
    gates = z_ref[:, 3 * W:3 * W + LANES]

    tail = tail_ref[...]
    conv = cb_ref[...] + cw_ref[MLSTM_CONV - 1:MLSTM_CONV, :] * qk_in
    for d in range(1, MLSTM_CONV):
        conv = conv + cw_ref[MLSTM_CONV - 1 - d:MLSTM_CONV - d, :] * _shift_rows(qk_in, tail, d)
    tail_ref[...] = qk_in[tc - SUBLANES:, :]
    qk = conv * _sigmoid(conv)
    q = qk[:, 0:N_HEADS * DK] * (DK ** -0.5)
    k = qk[:, N_HEADS * DK:]

    capped = GATE_SOFTCAP * jnp.tanh((gates + gb_ref[...]) / GATE_SOFTCAP)
    lf = _log_sigmoid(capped)

    gate_col = lax.broadcasted_iota(jnp.int32, (LANES, W), 0)
    lane_head = _idiv(lax.broadcasted_iota(jnp.int32, (LANES, W), 1), DV)
    pick_i = (gate_col == lane_head).astype(BF16)
    pick_f = (gate_col == N_HEADS + lane_head).astype(BF16)
    li_e = _dot_split(capped, pick_i, 3)
    rt = lax.broadcasted_iota(jnp.int32, (tc, tc), 0)
    ct = lax.broadcasted_iota(jnp.int32, (tc, tc), 1)
    tri = ((_idiv(rt, L) == _idiv(ct, L)) & (ct <= rt)).astype(BF16)
    b_e = sum(_dot(tri, _dot(piece, pick_f).astype(BF16)) for piece in _split_bf16(lf, 3))

    key = lax.broadcasted_iota(jnp.int32, (L, W), 0)
    query = _imod(lax.broadcasted_iota(jnp.int32, (L, W), 1), L)
    on_diag = key == query
    causal_t = key <= query
    head_ones = _head_ones(W, DV)
    wide = lambda m: jnp.concatenate([m, m], axis=1)
    wide_lane_head = wide(lax.broadcasted_iota(jnp.int32, (1, W), 1) // DV)
    same_head_k = wide(_idiv(lax.broadcasted_iota(jnp.int32, (N_HEADS * DK, W), 0), DK)
                       == _idiv(lax.broadcasted_iota(jnp.int32, (N_HEADS * DK, W), 1), DV))
    q_lane_head = lax.broadcasted_iota(jnp.int32, (1, N_HEADS * DK), 1) // DK
    ones_b = jnp.ones((L, W), BF16)

    c_st = c_ref[...]
    n_st = n_ref[...]
    m_st = m_ref[0:1, :]
    for c in range(tc // L):
        sl = slice(c * L, (c + 1) * L)
        bc, lic, vc = b_e[sl], li_e[sl], v[sl]
        qc = q[sl]
        kc = k[sl].astype(BF16)
        q_stack = jnp.concatenate([jnp.where(q_lane_head == h, qc, 0.0) for h in range(N_HEADS)],
                                  axis=0).astype(BF16)
        b_q = jnp.sum(jnp.where(on_diag, bc, 0.0), axis=0, keepdims=True)
        dmat = jnp.where(causal_t, b_q - bc + lic, -jnp.inf)
        m_t = jnp.maximum(b_q + m_st, jnp.max(dmat, axis=0, keepdims=True))
        s_t = (_dot_nt(kc, q_stack) * jnp.exp(dmat - m_t)).astype(BF16)
        m_e = sum(_dot(jnp.where(on_diag, piece.astype(F32), 0.0).astype(BF16), head_ones)
                  for piece in _split_bf16(m_t, 3))
        inter = jnp.exp(bc + m_st - m_e)
        nv = _dot_tn(s_t, jnp.concatenate([vc.astype(BF16), ones_b], axis=1))
        nv = functools.reduce(lambda lo, h: jnp.where(wide_lane_head == h, nv[h * L:(h + 1) * L], lo),
                              range(1, N_HEADS), nv[0:L])
        qcn = _dot(qc.astype(BF16), jnp.concatenate([c_st, n_st], axis=1).astype(BF16))
        num = nv[:, :W] + inter * qcn[:, :W]
        den = nv[:, W:] + inter * qcn[:, W:]
        h_ref[sl, :] = num / jnp.maximum(jnp.abs(den), jnp.exp(-m_e))

        b_last = bc[L - 1:L, :]
        gexp = b_last - bc + lic
        m_new = jnp.maximum(b_last + m_st, jnp.max(gexp, axis=0, keepdims=True))
        wts = jnp.exp(gexp - m_new)
        dec = jnp.exp(b_last + m_st - m_new)
        upd = _dot_tn(kc, jnp.concatenate([wts * vc, wts], axis=1).astype(BF16))
        upd = jnp.where(same_head_k, upd, 0.0)
        c_st = dec * c_st + upd[:, :W]
        n_st = dec * n_st + upd[:, W:]
        m_st = m_new
    c_ref[...] = c_st
    n_ref[...] = n_st
    m_ref[0:1, :] = m_st

    hv = h_ref[...]
    ms = _dot_split(hv * hv, _head_ones(W, DV), 2) * (1.0 / DV)
    o_ref[...] = hv * lax.rsqrt(ms + NORM_EPS) * ng_ref[...] * _sigmoid(og)


def _mlstm(zb, p, bsz, t, tc):
    nt = t // tc
    full = lambda a: pl.BlockSpec(a.shape, lambda b, i: (0,) * a.ndim)
    params = [p["conv_w"], p["conv_b"], p["gate_b"], p["norm_g"]]
    return pl.pallas_call(
        functools.partial(_mlstm_kernel, tc=tc),
        grid=(bsz, nt),
        in_specs=[pl.BlockSpec((tc, ZB_W), lambda b, i: (b * nt + i, 0))] + [full(a) for a in params],
        out_specs=pl.BlockSpec((tc, MIX_W), lambda b, i: (b * nt + i, 0)),
        out_shape=jax.ShapeDtypeStruct((bsz * t, MIX_W), F32),
        scratch_shapes=[
            pltpu.VMEM((SUBLANES, MIX_W), F32),
            pltpu.VMEM((N_HEADS * MLSTM_DK, MIX_W), F32),
            pltpu.VMEM((N_HEADS * MLSTM_DK, MIX_W), F32),
            pltpu.VMEM((SUBLANES, MIX_W), F32),
            pltpu.VMEM((tc, MIX_W), F32),
        ],
        compiler_params=pltpu.CompilerParams(dimension_semantics=("arbitrary", "arbitrary")),
        name="mlstm",
    )(zb, *params)


SWA_SUB_BLOCKS = 4


def _t5_bucket(dist):
    max_exact = REL_BUCKETS // 2
    d = np.maximum(dist, 1).astype(np.float32)
    large = max_exact + (np.log(d / max_exact) / math.log(REL_MAX_DIST / max_exact)
                         * (REL_BUCKETS - max_exact)).astype(np.int32)
    large = np.minimum(large, REL_BUCKETS - 1)
    return np.where(dist < max_exact, dist, large).astype(np.int32)


def _swa_kernel(rb_ref, sink_ref, bucket_ref, q_ref, kp_ref, kc_ref, vp_ref, vc_ref, o_ref, bias_ref):
    blk = ATTN_BLOCK
    grp = N_HEADS // SWA_KV_HEADS
    kvw = SWA_KV_HEADS * HEAD_DIM
    first = (pl.program_id(0) == 0) & (pl.program_id(1) == 0)

    @pl.when(first)
    def _():
        bucket = bucket_ref[...]
        for h in range(N_HEADS):
            acc = jnp.full((2 * blk, blk), -jnp.inf, F32)
            for bk in range(REL_BUCKETS):
                acc = jnp.where(bucket == bk, rb_ref[bk, h], acc)
            bias_ref[h // grp, :, (h % grp) * blk:(h % grp + 1) * blk] = acc

    key = lax.broadcasted_iota(jnp.int32, (2 * blk, grp * blk), 0)
    live = (key >= blk) | (pl.program_id(1) > 0)
    member = lax.broadcasted_iota(jnp.int32, (1, grp * blk), 1) // blk
    kw = jnp.concatenate([kp_ref[...], kc_ref[...]], axis=0).astype(BF16)
    vw = jnp.concatenate([vp_ref[...], vc_ref[...]], axis=0).astype(BF16)
    lane_member = lax.broadcasted_iota(jnp.int32, (1, grp * HEAD_DIM), 1) // HEAD_DIM
    rr = lax.broadcasted_iota(jnp.int32, (kvw, grp * HEAD_DIM), 0)
    cc = lax.broadcasted_iota(jnp.int32, (kvw, grp * HEAD_DIM), 1)
    vr = lax.broadcasted_iota(jnp.int32, (HEAD_DIM, kvw), 0)
    vc_ = lax.broadcasted_iota(jnp.int32, (HEAD_DIM, kvw), 1)
    n_sub = q_ref.shape[0] // blk
    k_rep, v_t, sinks = [], [], []
    for j in range(SWA_KV_HEADS):
        spread = ((_idiv(rr, HEAD_DIM) == j) & (_imod(rr, HEAD_DIM) == _imod(cc, HEAD_DIM))).astype(BF16)
        pick = ((_idiv(vc_, HEAD_DIM) == j) & (_imod(vc_, HEAD_DIM) == vr)).astype(BF16)
        k_rep.append(_dot(kw, spread).astype(BF16))
        v_t.append(_dot_nt(pick, vw).astype(BF16))
        sinks.append(jnp.where(member == 0, sink_ref[j * grp], sink_ref[j * grp + 1]))
    pairs = [(n, j) for n in range(n_sub) for j in range(SWA_KV_HEADS)]
    scores = []
    for n, j in pairs:
        qp = q_ref[n * blk:(n + 1) * blk, j * grp * HEAD_DIM:(j + 1) * grp * HEAD_DIM] * (HEAD_DIM ** -0.5)
        q_stack = jnp.concatenate([jnp.where(lane_member == g, qp, 0.0) for g in range(grp)],
                                  axis=0).astype(BF16)
        s = _dot_nt(k_rep[j][n * blk:(n + 2) * blk], q_stack) + bias_ref[j]
        scores.append(jnp.where(live, s, -jnp.inf) if n == 0 else s)
    probs = []
    for (n, j), s in zip(pairs, scores):
        m = jnp.maximum(jnp.max(s, axis=0, keepdims=True), sinks[j])
        p = jnp.exp(s - m)
        denom = jnp.sum(p, axis=0, keepdims=True) + jnp.exp(sinks[j] - m)
        probs.append((p / denom).astype(BF16))
    outs = [_dot(v_t[j][:, n * blk:(n + 2) * blk], p) for (n, j), p in zip(pairs, probs)]
    for n in range(n_sub):
        heads = [outs[n * SWA_KV_HEADS + j][:, g * blk:(g + 1) * blk]
                 for j in range(SWA_KV_HEADS) for g in range(grp)]
        o_ref[n * blk:(n + 1) * blk, :] = jnp.concatenate(heads, axis=0).T


def _swa(zc, rel_bias, sinks, bsz, t):
    blk = ATTN_BLOCK
    nb = t // blk
    assert N_HEADS // SWA_KV_HEADS == 2
    dist = np.arange(blk)[None, :] + blk - np.arange(2 * blk)[:, None]
    bucket = jnp.asarray(np.where((dist >= 0) & (dist < SWA_WINDOW),
                                  _t5_bucket(np.clip(dist, 0, SWA_WINDOW - 1)), -1).astype(np.int32))
    kvw = SWA_KV_HEADS * HEAD_DIM
    kcol = MIX_W // kvw
    n_sub = min(SWA_SUB_BLOCKS, nb)
    assert nb % n_sub == 0
    ns = nb // n_sub
    rows = n_sub * blk
    cur = lambda b, i: b * ns + i
    prev = lambda b, i: b * nb + jnp.maximum(i * n_sub - 1, 0)
    return pl.pallas_call(
        _swa_kernel,
        grid=(bsz, ns),
        in_specs=[
            pl.BlockSpec(memory_space=pltpu.SMEM),
            pl.BlockSpec(memory_space=pltpu.SMEM),
            pl.BlockSpec((2 * blk, blk), lambda b, i: (0, 0)),
            pl.BlockSpec((rows, MIX_W), lambda b, i: (cur(b, i), 0)),
            pl.BlockSpec((blk, kvw), lambda b, i: (prev(b, i), kcol)),
            pl.BlockSpec((rows, kvw), lambda b, i: (cur(b, i), kcol)),
            pl.BlockSpec((blk, kvw), lambda b, i: (prev(b, i), kcol + 1)),
            pl.BlockSpec((rows, kvw), lambda b, i: (cur(b, i), kcol + 1)),
        ],
        out_specs=pl.BlockSpec((rows, MIX_W), lambda b, i: (cur(b, i), 0)),
        out_shape=jax.ShapeDtypeStruct((bsz * t, MIX_W), F32),
        scratch_shapes=[pltpu.VMEM((SWA_KV_HEADS, 2 * blk, 2 * blk), F32)],
        compiler_params=pltpu.CompilerParams(dimension_semantics=("arbitrary", "arbitrary")),
        name="swa",
    )(rel_bias, sinks, bucket, zc, zc, zc, zc, zc)


FOX_BLOCK = 256
FOX_AUG = LANES
FOX_PARTS = 3
FOX_KEY_SPAN = 2


def _fox_placements():
    wide = N_HEADS * FOX_AUG
    pk = np.zeros((MIX_W, wide), np.float32)
    pck = np.zeros((LANES, wide), np.float32)
    ones_k = np.zeros((SUBLANES, wide), np.float32)
    pq_t = np.zeros((wide, MIX_W), np.float32)
    pv_t = np.zeros((wide, MIX_W), np.float32)
    pcq_t = np.zeros((wide, LANES), np.float32)
    ones_t = np.zeros((2, wide, LANES), np.float32)
    for h in range(N_HEADS):
        base = h * FOX_AUG
        for d in range(HEAD_DIM):
            pk[h * HEAD_DIM + d, base + d] = 1.0
            pq_t[base + d, h * HEAD_DIM + d] = HEAD_DIM ** -0.5
            pv_t[base + d, h * HEAD_DIM + d] = 1.0
        for n in range(FOX_PARTS):
            pcq_t[base + HEAD_DIM + n, n * N_HEADS + h] = 1.0
            pck[n * N_HEADS + h, base + HEAD_DIM + FOX_PARTS + n] = -1.0
            ones_t[0, base + HEAD_DIM + FOX_PARTS + n, :] = 1.0
            ones_k[0, base + HEAD_DIM + n] = 1.0
        ones_t[1, base + HEAD_DIM, :] = 1.0
    bf = lambda a: jnp.asarray(a, BF16)
    return bf(pk), bf(pck), jnp.asarray(ones_k), bf(np.stack([pq_t, pv_t])), bf(pcq_t), jnp.asarray(ones_t)


def _fox_kernel(q_ref, k_ref, v_ref, f_ref, fb_ref, pk_ref, pck_ref, onesk_ref, pqv_ref, pcq_ref, onest_ref,
                o_ref, kaug_ref, vaug_ref, m_ref, acc_ref, clast_ref, s_ref, p_ref):
    blk = FOX_BLOCK
    i = pl.program_id(1)

    @pl.when(i == 0)
    def _():
        clast_ref[...] = jnp.zeros_like(clast_ref)

    ls = _log_sigmoid(f_ref[...] + fb_ref[...])
    row = lax.broadcasted_iota(jnp.int32, (blk, blk), 0)
    col = lax.broadcasted_iota(jnp.int32, (blk, blk), 1)
    cq = _split_dot((col <= row).astype(BF16), ls, FOX_PARTS) + clast_ref[0:1, :]
    clast_ref[0:1, :] = cq[blk - 1:blk, :]

    widen = lambda a: jnp.concatenate([a] * (blk // LANES), axis=1)
    qa = _dot_nt(pqv_ref[0], q_ref[...].astype(BF16)) + widen(onest_ref[0])
    va = _dot_nt(pqv_ref[1], v_ref[...].astype(BF16)) + widen(onest_ref[1])
    ka = _dot(k_ref[...].astype(BF16), pk_ref[...]) + onesk_ref[0:1, :]
    lane = lax.broadcasted_iota(jnp.int32, (1, LANES), 1)
    pieces = _split_bf16(jnp.where(lane < N_HEADS, cq, 0.0), FOX_PARTS)
    packed = pieces[0].astype(F32)
    for n in range(1, FOX_PARTS):
        packed = packed + pltpu.roll(pieces[n].astype(F32), n * N_HEADS, axis=1)
    packed = packed.astype(BF16)
    qa = (qa + _dot_nt(pcq_ref[...], packed)).astype(BF16)
    ka = ka + _dot(packed, pck_ref[...])
    row0 = pl.multiple_of(i * blk, blk)
    for h in range(N_HEADS):
        hs = slice(h * FOX_AUG, (h + 1) * FOX_AUG)
        kaug_ref[h, pl.ds(row0, blk), :] = ka[:, hs].astype(BF16)
        vaug_ref[h, :, pl.ds(row0, blk)] = va[hs, :].astype(BF16)

    key_le_query = row <= col

    def attend(off, nk, diagonal):
        m_new = []
        for h in range(N_HEADS):
            s = _dot(kaug_ref[h, pl.ds(off, nk), :], qa[h * FOX_AUG:(h + 1) * FOX_AUG, :])
            s = jnp.where(key_le_query, s, -jnp.inf) if diagonal else s
            s_ref[h, 0:nk, :] = s
            m_new.append(jnp.maximum(m_ref[h], jnp.max(s, axis=0, keepdims=True)))
        for h in range(N_HEADS):
            p_ref[h, 0:nk, :] = jnp.exp(s_ref[h, 0:nk, :] - m_new[h]).astype(BF16)
        for h in range(N_HEADS):
            pv = _dot(vaug_ref[h, :, pl.ds(off, nk)], p_ref[h, 0:nk, :])
            acc_ref[h] = jnp.exp(m_ref[h] - m_new[h]) * acc_ref[h] + pv
            m_ref[h] = m_new[h]

    m_ref[...] = jnp.full_like(m_ref, -jnp.inf)
    acc_ref[...] = jnp.zeros_like(acc_ref)
    attend(row0, blk, True)

    def body(j, carry):
        attend(pl.multiple_of(j * FOX_KEY_SPAN * blk, FOX_KEY_SPAN * blk), FOX_KEY_SPAN * blk, False)
        return carry

    lax.fori_loop(0, i // FOX_KEY_SPAN, body, 0)
    for r in range(1, FOX_KEY_SPAN):
        @pl.when(i % FOX_KEY_SPAN >= r)
        def _():
            attend(pl.multiple_of((i - r) * blk, blk), blk, False)
    outs = []
    for h in range(N_HEADS):
        acc = acc_ref[h]
        outs.append(acc[0:HEAD_DIM, :] / acc[HEAD_DIM:HEAD_DIM + 1, :])
    o_ref[...] = jnp.concatenate(outs, axis=0).T


def _fox(zd, fb, bsz, t):
    blk = FOX_BLOCK
    assert t % blk == 0
    nb = t // blk
    consts = _fox_placements()
    const = lambda a: pl.BlockSpec(a.shape, lambda b, i: (0,) * a.ndim)
    cur = lambda c: (lambda b, i: (b * nb + i, c))
    return pl.pallas_call(
        _fox_kernel,
        grid=(bsz, nb),
        in_specs=[
            pl.BlockSpec((blk, MIX_W), cur(0)),
            pl.BlockSpec((blk, MIX_W), cur(1)),
            pl.BlockSpec((blk, MIX_W), cur(2)),
            pl.BlockSpec((blk, LANES), cur(3 * MIX_W // LANES)),
            const(fb)] + [const(a) for a in consts],
        out_specs=pl.BlockSpec((blk, MIX_W), cur(0)),
        out_shape=jax.ShapeDtypeStruct((bsz * t, MIX_W), F32),
        scratch_shapes=[
            pltpu.VMEM((N_HEADS, t, FOX_AUG), BF16),
            pltpu.VMEM((N_HEADS, FOX_AUG, t), BF16),
            pltpu.VMEM((N_HEADS, 1, blk), F32),
            pltpu.VMEM((N_HEADS, FOX_AUG, blk), F32),
            pltpu.VMEM((SUBLANES, LANES), F32),
            pltpu.VMEM((N_HEADS, FOX_KEY_SPAN * blk, blk), F32),
            pltpu.VMEM((N_HEADS, FOX_KEY_SPAN * blk, blk), BF16),
        ],
        compiler_params=pltpu.CompilerParams(dimension_semantics=("arbitrary", "arbitrary")),
        name="fox",
    )(zd, zd, zd, zd, fb, *consts)


def _outproj_kernel(x_ref, ya_ref, yb_ref, yc_ref, yd_ref, w_ref, g_ref, o_ref):
    acc = None
    for n, ref in enumerate((ya_ref, yb_ref, yc_ref, yd_ref)):
        part = _dot(ref[...].astype(BF16), w_ref[n * MIX_W:(n + 1) * MIX_W, :])
        acc = part if acc is None else acc + part
    o_ref[...] = x_ref[...] + acc * _rms_scale(acc) * g_ref[...]


def _outproj(x2, ys, w, g, tm):
    rows = x2.shape[0]
    return pl.pallas_call(
        _outproj_kernel,
        grid=(rows // tm,),
        in_specs=[pl.BlockSpec((tm, D_MODEL), lambda i: (i, 0))]
        + [pl.BlockSpec((tm, MIX_W), lambda i: (i, 0))] * 4
        + [pl.BlockSpec((N_HEADS * MIX_W, D_MODEL), lambda i: (0, 0)),
           pl.BlockSpec((1, D_MODEL), lambda i: (0, 0))],
        out_specs=pl.BlockSpec((tm, D_MODEL), lambda i: (i, 0)),
        out_shape=jax.ShapeDtypeStruct((rows, D_MODEL), F32),
        compiler_params=pltpu.CompilerParams(dimension_semantics=("arbitrary",)),
        name="outproj",
    )(x2, *ys, w, g)


FFN_COL_CHUNK = 6 * MXU_COLS


def _ffn_kernel(x_ref, gpre_ref, wg_ref, wu_ref, cw_ref, cb_ref, wd_ref, gpost_ref, o_ref, tail_ref, *, tm, t):
    x = x_ref[...]
    h = (x * _rms_scale(x) * gpre_ref[...]).astype(BF16)
    seq_start = (pl.program_id(0) * tm) % t == 0

    @pl.when(pl.program_id(0) == 0)
    def _():
        tail_ref[...] = jnp.zeros_like(tail_ref)

    acc = None
    for c0 in range(0, D_FF, FFN_COL_CHUNK):
        cs = slice(c0, min(c0 + FFN_COL_CHUNK, D_FF))
        gate = _dot(h, wg_ref[:, cs])
        halo = jnp.where(seq_start, 0.0, tail_ref[:, cs])
        tail_ref[:, cs] = gate[tm - SUBLANES:, :]
        conv = cb_ref[:, cs] + cw_ref[FFN_CONV - 1:FFN_CONV, cs] * gate
        for d in range(1, FFN_CONV):
            conv = conv + cw_ref[FFN_CONV - 1 - d:FFN_CONV - d, cs] * _shift_rows(gate, halo, d)
        f = jax.nn.gelu(conv, approximate=True) * _dot(h, wu_ref[:, cs])
        part = _dot(f.astype(BF16), wd_ref[cs, :])
        acc = part if acc is None else acc + part
    o_ref[...] = x + acc * _rms_scale(acc) * gpost_ref[...]


def _ffn(x2, p, t, tm):
    rows = x2.shape[0]
    const = lambda a: pl.BlockSpec(a.shape, lambda i: (0, 0), pipeline_mode=pl.Buffered(1))
    params = [p["g_pre"], p["w_gate"], p["w_up"], p["conv_w"], p["conv_b"], p["w_down"], p["g_post"]]
    return pl.pallas_call(
        functools.partial(_ffn_kernel, tm=tm, t=t),
        grid=(rows // tm,),
        in_specs=[pl.BlockSpec((tm, D_MODEL), lambda i: (i, 0))] + [const(a) for a in params],
        out_specs=pl.BlockSpec((tm, D_MODEL), lambda i: (i, 0)),
        out_shape=jax.ShapeDtypeStruct((rows, D_MODEL), F32),
        scratch_shapes=[pltpu.VMEM((SUBLANES, D_FF), F32)],
        compiler_params=pltpu.CompilerParams(dimension_semantics=("arbitrary",),
                                             vmem_limit_bytes=56 * 1024 * 1024),
        name="ffn",
    )(x2, *params)


def _pad_cols(a, width):
    return jnp.pad(a, ((0, 0), (0, width - a.shape[1])))


def _row(a):
    return a.reshape(1, -1).astype(F32)


def _layer_params(l, w_in, w_out, norm_mix_pre, norm_mix_post, norm_ffn_pre, norm_ffn_post,
                  rwkv_mu, rwkv_w0, rwkv_w_up, rwkv_a0, rwkv_a_up, rwkv_g_up, rwkv_k_k, rwkv_k_a,
                  rwkv_r_k, rwkv_ln_w, rwkv_ln_b, mlstm_conv_w, mlstm_conv_b, mlstm_b_i, mlstm_b_f,
                  mlstm_norm, swa_sinks, fox_b_f, ffn_w_up, ffn_conv_w, ffn_conv_b, ffn_w_down):
    wa_w = ZA_W
    wb_w = 3 * MIX_W + 2 * N_HEADS
    wc_w = ZC_W
    wi = jnp.transpose(w_in, (2, 0, 1))[:, l, :].astype(BF16)
    pad_rows = lambda a, n: jnp.pad(a, ((0, n - a.shape[0]), (0, 0)))
    groups = [wi[:wa_w],
              pad_rows(wi[wa_w:wa_w + wb_w], ZB_W),
              wi[wa_w + wb_w:wa_w + wb_w + wc_w],
              pad_rows(wi[wa_w + wb_w + wc_w:], ZD_W)]
    gate_b = _pad_cols(jnp.concatenate([mlstm_b_i[l], mlstm_b_f[l]]).reshape(1, -1), LANES)
    return {
        "w_in": jnp.concatenate(groups, axis=0),
        "g_mix_pre": _row(norm_mix_pre[l]),
        "g_mix_post": _row(norm_mix_post[l]),
        "w_out": w_out[l].astype(BF16),
        "rwkv": {"mu": _row(rwkv_mu[l]), "w0": _row(rwkv_w0[l]), "w_up": rwkv_w_up[l].astype(BF16),
                 "a0": _row(rwkv_a0[l]), "a_up": rwkv_a_up[l].astype(BF16), "g_up": rwkv_g_up[l].astype(BF16),
                 "k_k": _row(rwkv_k_k[l]), "k_a": _row(rwkv_k_a[l]), "r_k": _row(rwkv_r_k[l]),
                 "ln_w": _row(rwkv_ln_w[l]), "ln_b": _row(rwkv_ln_b[l])},
        "mlstm": {"conv_w": mlstm_conv_w[l], "conv_b": _row(mlstm_conv_b[l]), "gate_b": gate_b,
                  "norm_g": _row(mlstm_norm[l])},
        "swa_sinks": swa_sinks[l],
        "fox_b": _pad_cols(fox_b_f[l].reshape(1, -1), LANES),
        "ffn": {"g_pre": _row(norm_ffn_pre[l]), "w_gate": ffn_w_up[l][:, :D_FF].astype(BF16),
                "w_up": ffn_w_up[l][:, D_FF:].astype(BF16), "conv_w": ffn_conv_w[l],
                "conv_b": _row(ffn_conv_b[l]), "w_down": ffn_w_down[l].astype(BF16),
                "g_post": _row(norm_ffn_post[l])},
    }


def _tiles(t):
    return min(512, t), min(256, t), min(512, t)


def kernel(x, w_in, w_out, norm_mix_pre, norm_mix_post, norm_ffn_pre, norm_ffn_post, rwkv_mu, rwkv_w0, rwkv_w_up, rwkv_a0, rwkv_a_up, rwkv_g_up, rwkv_k_k, rwkv_k_a, rwkv_r_k, rwkv_ln_w, rwkv_ln_b, mlstm_conv_w, mlstm_conv_b, mlstm_b_i, mlstm_b_f, mlstm_norm, swa_sinks, fox_b_f, rel_bias, ffn_w_up, ffn_conv_w, ffn_conv_b, ffn_w_down):
    bsz, t, d = x.shape
    assert d == D_MODEL and t % ATTN_BLOCK == 0
    tm, tc, tf = _tiles(t)
    x2 = x.reshape(bsz * t, d)
    for l in range(w_in.shape[0]):
        p = _layer_params(l, w_in, w_out, norm_mix_pre, norm_mix_post, norm_ffn_pre, norm_ffn_post,
                          rwkv_mu, rwkv_w0, rwkv_w_up, rwkv_a0, rwkv_a_up, rwkv_g_up, rwkv_k_k, rwkv_k_a,
                          rwkv_r_k, rwkv_ln_w, rwkv_ln_b, mlstm_conv_w, mlstm_conv_b, mlstm_b_i, mlstm_b_f,
                          mlstm_norm, swa_sinks, fox_b_f, ffn_w_up, ffn_conv_w, ffn_conv_b, ffn_w_down)
        za, zb, zc, zd = _inproj(x2, p["g_mix_pre"], p["w_in"], tm)
        ya = _rwkv(za, p["rwkv"], bsz, t, min(2 * tc, t))
        yb = _mlstm(zb, p["mlstm"], bsz, t, tc)
        yc = _swa(zc, rel_bias, p["swa_sinks"], bsz, t)
        yd = _fox(zd, p["fox_b"], bsz, t)
        x2 = _outproj(x2, (ya, yb, yc, yd), p["w_out"], p["g_mix_post"], tm)
        x2 = _ffn(x2, p["ffn"], t, tf)
    return x2.reshape(bsz, t, d)
```

```python
import functools
import math

import jax
import jax.numpy as jnp
import numpy as np
from jax import lax
from jax.experimental import pallas as pl
from jax.experimental.pallas import tpu as pltpu

F32 = jnp.float32
BF16 = jnp.bfloat16

D_MODEL = 1024
HEAD_DIM = 64
N_HEADS = 4
MIX_W = N_HEADS * HEAD_DIM
RWKV_DECAY_RANK = 64
RWKV_AAA_RANK = 64
RWKV_GATE_RANK = 128
RWKV_LN_EPS = 64e-5
RWKV_CHUNK = 64
MLSTM_DK = 32
MLSTM_CONV = 4
MLSTM_CHUNK = 64
GATE_SOFTCAP = 15.0
SWA_KV_HEADS = 2
SWA_WINDOW = 128
ATTN_BLOCK = 128
REL_BUCKETS = 32
REL_MAX_DIST = 128
D_FF = 2816
FFN_CONV = 3
NORM_EPS = 1e-6

LANES = 128
SUBLANES = 8
MXU_COLS = 256

ZA_W = 3 * MIX_W + RWKV_DECAY_RANK + RWKV_AAA_RANK + RWKV_GATE_RANK
ZB_W = 3 * MIX_W + LANES
ZC_W = MIX_W + 2 * SWA_KV_HEADS * HEAD_DIM
ZD_W = 3 * MIX_W + LANES
Z_W = ZA_W + ZB_W + ZC_W + ZD_W

def _split_bf16(x, parts):
    out = []
    for n in range(parts):
        piece = x.astype(BF16)
        out.append(piece)
        if n + 1 < parts:
            x = x - piece.astype(F32)
    return out


def _dot(a, b, precision=None):
    return jnp.dot(a, b, preferred_element_type=F32, precision=precision)


def _dot_nt(a, b, precision=None):
    return lax.dot_general(a, b, (((1,), (1,)), ((), ())), preferred_element_type=F32, precision=precision)


def _dot_tn(a, b, precision=None):
    return lax.dot_general(a, b, (((0,), (0,)), ((), ())), preferred_element_type=F32, precision=precision)


def _dot_split(x, ones, parts):
    return sum(_dot(piece, ones) for piece in _split_bf16(x, parts))


def _split_dot(ones, x, parts):
    return sum(_dot(ones, piece) for piece in _split_bf16(x, parts))


def _bdot(a, b):
    return _dot(a.astype(BF16), b.astype(BF16))


def _bdot_nt(a, b):
    return _dot_nt(a.astype(BF16), b.astype(BF16))


def _bdot_tn(a, b):
    return _dot_tn(a.astype(BF16), b.astype(BF16))


def _sigmoid(x):
    return 1.0 / (1.0 + jnp.exp(-x))


def _log_sigmoid(x):
    return jnp.minimum(x, 0.0) - jnp.log(1.0 + jnp.exp(-jnp.abs(x)))


def _softplus(x):
    return jnp.maximum(x, 0.0) + jnp.log(1.0 + jnp.exp(-jnp.abs(x)))


def _rms_scale(x):
    return lax.rsqrt(jnp.mean(x * x, axis=-1, keepdims=True) + NORM_EPS)


def _shift_rows(x, prev_tail, d):
    rolled = pltpu.roll(x, d, axis=0)
    head_rows = lax.broadcasted_iota(jnp.int32, (SUBLANES, 1), 0)
    head = jnp.where(head_rows < d, pltpu.roll(prev_tail, d, axis=0), rolled[:SUBLANES])
    return jnp.concatenate([head, rolled[SUBLANES:]], axis=0)


def _idiv(x, n):
    assert n & (n - 1) == 0
    return lax.shift_right_logical(x, jnp.int32(n.bit_length() - 1))


def _imod(x, n):
    assert n & (n - 1) == 0
    return lax.bitwise_and(x, jnp.int32(n - 1))


def _head_ones(width, head):
    r = _idiv(lax.broadcasted_iota(jnp.int32, (width, width), 0), head)
    c = _idiv(lax.broadcasted_iota(jnp.int32, (width, width), 1), head)
    return (r == c).astype(BF16)


def _inproj_kernel(x_ref, g_ref, w_ref, za_ref, zb_ref, zc_ref, zd_ref):
    x = x_ref[...]
    h = (x * _rms_scale(x) * g_ref[...]).astype(BF16)
    z = _dot_nt(h, w_ref[...])
    off = 0
    for ref in (za_ref, zb_ref, zc_ref, zd_ref):
        w = ref.shape[1]
        ref[...] = z[:, off:off + w]
        off += w


def _inproj(x2, g, w, tm):
    rows = x2.shape[0]
    return pl.pallas_call(
        _inproj_kernel,
        grid=(rows // tm,),
        in_specs=[
            pl.BlockSpec((tm, D_MODEL), lambda i: (i, 0)),
            pl.BlockSpec((1, D_MODEL), lambda i: (0, 0)),
            pl.BlockSpec((Z_W, D_MODEL), lambda i: (0, 0)),
        ],
        out_specs=[pl.BlockSpec((tm, w_), lambda i: (i, 0)) for w_ in (ZA_W, ZB_W, ZC_W, ZD_W)],
        out_shape=[jax.ShapeDtypeStruct((rows, w_), F32) for w_ in (ZA_W, ZB_W, ZC_W, ZD_W)],
        compiler_params=pltpu.CompilerParams(dimension_semantics=("arbitrary",)),
        name="inproj",
    )(x2, g, w)


def _rwkv_kernel(z_ref, mu_ref, w0_ref, wup_ref, a0_ref, aup_ref, gup_ref, kk_ref, ka_ref, rk_ref,
                 lnw_ref, lnb_ref, o_ref, tail_ref, st_ref, y_ref, *, tc):
    L = RWKV_CHUNK
    W = MIX_W

    z = z_ref[...]
    zz = z + mu_ref[...] * (_shift_rows(z, tail_ref[...], 1) - z)
    tail_ref[...] = z[tc - SUBLANES:, :]

    r = zz[:, 0:W]
    k = zz[:, W:2 * W]
    v = zz[:, 2 * W:3 * W]
    o1 = 3 * W
    wd = zz[:, o1:o1 + RWKV_DECAY_RANK]
    ad = zz[:, o1 + RWKV_DECAY_RANK:o1 + RWKV_DECAY_RANK + RWKV_AAA_RANK]
    gd = zz[:, o1 + RWKV_DECAY_RANK + RWKV_AAA_RANK:]

    lw = -jnp.exp(-_softplus(-(w0_ref[...] + _bdot(jnp.tanh(wd), wup_ref[...]))) - 0.5)
    alpha = _sigmoid(a0_ref[...] + _bdot(ad, aup_ref[...]))
    g = _bdot(_sigmoid(gd), gup_ref[...])

    hsum = _head_ones(W, HEAD_DIM)
    kk = k * kk_ref[...]
    kk = kk * lax.rsqrt(jnp.maximum(_dot_split(kk * kk, hsum, 2), 1e-24))
    k = k * (1.0 + (alpha - 1.0) * ka_ref[...])

    span = min(tc, MXU_COLS)
    rt = lax.broadcasted_iota(jnp.int32, (span, span), 0)
    ct = lax.broadcasted_iota(jnp.int32, (span, span), 1)
    tri = ((_idiv(rt, L) == _idiv(ct, L)) & (ct <= rt)).astype(BF16)
    cum = jnp.concatenate([_split_dot(tri, lw[r0:r0 + span], 3) for r0 in range(0, tc, span)], axis=0)
    e_in = jnp.exp(cum)
    e_out = jnp.exp(-cum)
    r_t = r * e_in
    a_t = -kk * jnp.exp(cum - lw)
    b_t = kk * alpha * e_out
    k_t = k * e_out

    lane_head = lax.broadcasted_iota(jnp.int32, (1, W), 1) // HEAD_DIM

    def stack(xc):
        return jnp.concatenate([jnp.where(lane_head == h, xc, 0.0) for h in range(N_HEADS)], axis=0)

    n = N_HEADS * L
    rr = lax.broadcasted_iota(jnp.int32, (n, n), 0)
    cc = lax.broadcasted_iota(jnp.int32, (n, n), 1)
    own_head = _idiv(rr, L) == _idiv(cc, L)
    strict = own_head & (_imod(rr, L) > _imod(cc, L))
    incl = own_head & (_imod(rr, L) >= _imod(cc, L))
    eye = (rr == cc).astype(F32)

    chunks = [slice(c * L, (c + 1) * L) for c in range(tc // L)]
    stk = [tuple(stack(u[sl]).astype(BF16) for u in (a_t, r_t, b_t, k_t, v)) for sl in chunks]
    bk_rep = [jnp.concatenate([b_t[sl]] * 2 + [k_t[sl]] * 2, axis=0).astype(BF16) for sl in chunks]
    ar = [_dot_nt(jnp.concatenate([s_[0], s_[1]], axis=0), x) for s_, x in zip(stk, bk_rep)]
    over_heads = lambda x: jnp.concatenate([x, x], axis=1)
    a_ab = [jnp.where(strict, over_heads(x[:n, :2 * L]), 0.0) for x in ar]
    a_ak = [jnp.where(strict, over_heads(x[:n, 2 * L:]), 0.0).astype(BF16) for x in ar]
    a_rb = [jnp.where(incl, over_heads(x[n:, :2 * L]), 0.0).astype(BF16) for x in ar]
    a_rk = [jnp.where(incl, over_heads(x[n:, 2 * L:]), 0.0).astype(BF16) for x in ar]
    inv = [eye + m for m in a_ab]
    pw = [m.astype(BF16) for m in a_ab]
    for _ in range(int(math.log2(L)) - 1):
        pw = [_dot(m, m).astype(BF16) for m in pw]
        inv = [t_ + _dot(t_.astype(BF16), m) for t_, m in zip(inv, pw)]
    inv = [t_.astype(BF16) for t_ in inv]
    akv = [_dot(m, s_[4]).astype(BF16) for m, s_ in zip(a_ak, stk)]
    wu = [_dot(t_, jnp.concatenate([s_[0], x], axis=1)) for t_, s_, x in zip(inv, stk, akv)]
    w_m = [x[:, :W].astype(BF16) for x in wu]
    u2 = [x[:, W:].astype(BF16) for x in wu]
    r_eff = [(s_[1].astype(F32) + _dot(m, w_)).astype(BF16) for s_, m, w_ in zip(stk, a_rb, w_m)]
    y_own = [_dot(m, u_) + _dot(n_, s_[4]) for m, u_, n_, s_ in zip(a_rb, u2, a_rk, stk)]
    st_mix = [_dot_tn(w_, s_[2]).astype(BF16) for w_, s_ in zip(w_m, stk)]
    st_own = [_dot_tn(u_, s_[2]) + _dot_tn(s_[4], s_[3]) for u_, s_ in zip(u2, stk)]

    st = st_ref[...]
    for c, sl in enumerate(chunks):
        st_b = st.astype(BF16)
        y_s = _dot_nt(r_eff[c], st_b) + y_own[c]
        st = (st + _dot(st_b, st_mix[c]) + st_own[c]) * e_in[(c + 1) * L - 1:(c + 1) * L, :]
        y_c = y_s[0:L]
        for h in range(1, N_HEADS):
            y_c = y_c + y_s[h * L:(h + 1) * L]
        y_ref[sl, :] = y_c
    st_ref[...] = st

    y = y_ref[...]
    inv_n = 1.0 / HEAD_DIM
    mean = _dot_split(y, hsum, 2) * inv_n
    yc = y - mean
    var = _dot_split(yc * yc, hsum, 2) * inv_n
    y = yc * lax.rsqrt(var + RWKV_LN_EPS) * lnw_ref[...] + lnb_ref[...]
    bonus = _dot_split(r * k * rk_ref[...], hsum, 2) * v
    o_ref[...] = (y + bonus) * g


def _mlstm_kernel(z_ref, cw_ref, cb_ref, gb_ref, ng_ref, o_ref, tail_ref, c_ref, n_ref, m_ref, h_ref, *, tc):
    L = MLSTM_CHUNK
    W = MIX_W
    DK = MLSTM_DK
    DV = HEAD_DIM

    qk_in = z_ref[:, 0:W]
    v = z_ref[:, W:2 * W]
    og = z_ref[:, 2 * W:3 * W]
    gates = z_ref[:, 3 * W:3 * W + LANES]

    tail = tail_ref[...]
    conv = cb_ref[...] + cw_ref[MLSTM_CONV - 1:MLSTM_CONV, :] * qk_in
    for d in range(1, MLSTM_CONV):
        conv = conv + cw_ref[MLSTM_CONV - 1 - d:MLSTM_CONV - d, :] * _shift_rows(qk_in, tail, d)
    tail_ref[...] = qk_in[tc - SUBLANES:, :]
    qk = conv * _sigmoid(conv)
    q = qk[:, 0:N_HEADS * DK] * (DK ** -0.5)
    k = qk[:, N_HEADS * DK:]

    capped = GATE_SOFTCAP * jnp.tanh((gates + gb_ref[...]) / GATE_SOFTCAP)
    lf = _log_sigmoid(capped)

    gate_col = lax.broadcasted_iota(jnp.int32, (LANES, W), 0)
    lane_head = _idiv(lax.broadcasted_iota(jnp.int32, (LANES, W), 1), DV)
    pick_i = (gate_col == lane_head).astype(BF16)
    pick_f = (gate_col == N_HEADS + lane_head).astype(BF16)
    li_e = _dot_split(capped, pick_i, 3)
    span = min(tc, MXU_COLS)
    rt = lax.broadcasted_iota(jnp.int32, (span, span), 0)
    ct = lax.broadcasted_iota(jnp.int32, (span, span), 1)
    tri = ((_idiv(rt, L) == _idiv(ct, L)) & (ct <= rt)).astype(BF16)
    lf_e = [_dot(piece, pick_f).astype(BF16) for piece in _split_bf16(lf, 3)]
    b_e = jnp.concatenate([sum(_dot(tri, piece[r0:r0 + span]) for piece in lf_e)
                           for r0 in range(0, tc, span)], axis=0)

    key = lax.broadcasted_iota(jnp.int32, (L, W), 0)
    query = _imod(lax.broadcasted_iota(jnp.int32, (L, W), 1), L)
    on_diag = key == query
    causal_t = key <= query
    head_ones = _head_ones(W, DV)
    wide = lambda m: jnp.concatenate([m, m], axis=1)
    wide_lane_head = wide(lax.broadcasted_iota(jnp.int32, (1, W), 1) // DV)
    same_head_k = wide(_idiv(lax.broadcasted_iota(jnp.int32, (N_HEADS * DK, W), 0), DK)
                       == _idiv(lax.broadcasted_iota(jnp.int32, (N_HEADS * DK, W), 1), DV))
    q_lane_head = lax.broadcasted_iota(jnp.int32, (1, N_HEADS * DK), 1) // DK
    ones_b = jnp.ones((L, W), BF16)

    c_st = c_ref[...]
    n_st = n_ref[...]
    m_st = m_ref[0:1, :]
    for c in range(tc // L):
        sl = slice(c * L, (c + 1) * L)
        bc, lic, vc = b_e[sl], li_e[sl], v[sl]
        qc = q[sl]
        kc = k[sl].astype(BF16)
        q_stack = jnp.concatenate([jnp.where(q_lane_head == h, qc, 0.0) for h in range(N_HEADS)],
                                  axis=0).astype(BF16)
        b_q = jnp.sum(jnp.where(on_diag, bc, 0.0), axis=0, keepdims=True)
        dmat = jnp.where(causal_t, b_q - bc + lic, -jnp.inf)
        m_t = jnp.maximum(b_q + m_st, jnp.max(dmat, axis=0, keepdims=True))
        s_t = (_dot_nt(kc, q_stack) * jnp.exp(dmat - m_t)).astype(BF16)
        m_e = sum(_dot(jnp.where(on_diag, piece.astype(F32), 0.0).astype(BF16), head_ones)
                  for piece in _split_bf16(m_t, 3))
        inter = jnp.exp(bc + m_st - m_e)
        nv = _dot_tn(s_t, jnp.concatenate([vc.astype(BF16), ones_b], axis=1))
        nv = functools.reduce(lambda lo, h: jnp.where(wide_lane_head == h, nv[h * L:(h + 1) * L], lo),
                              range(1, N_HEADS), nv[0:L])
        qcn = _dot(qc.astype(BF16), jnp.concatenate([c_st, n_st], axis=1).astype(BF16))
        num = nv[:, :W] + inter * qcn[:, :W]
        den = nv[:, W:] + inter * qcn[:, W:]
        h_ref[sl, :] = num / jnp.maximum(jnp.abs(den), jnp.exp(-m_e))

        b_last = bc[L - 1:L, :]
        gexp = b_last - bc + lic
        m_new = jnp.maximum(b_last + m_st, jnp.max(gexp, axis=0, keepdims=True))
        wts = jnp.exp(gexp - m_new)
        dec = jnp.exp(b_last + m_st - m_new)
        upd = _dot_tn(kc, jnp.concatenate([wts * vc, wts], axis=1).astype(BF16))
        upd = jnp.where(same_head_k, upd, 0.0)
        c_st = dec * c_st + upd[:, :W]
        n_st = dec * n_st + upd[:, W:]
        m_st = m_new
    c_ref[...] = c_st
    n_ref[...] = n_st
    m_ref[0:1, :] = m_st

    hv = h_ref[...]
    ms = _dot_split(hv * hv, _head_ones(W, DV), 2) * (1.0 / DV)
    o_ref[...] = hv * lax.rsqrt(ms + NORM_EPS) * ng_ref[...] * _sigmoid(og)


N_RWKV_PARAMS = 11
N_MLSTM_PARAMS = 4
N_RWKV_SCRATCH = 3
N_MLSTM_SCRATCH = 5


def _recurrent_kernel(*refs, tc):
    it = iter(refs)
    take = lambda n: [next(it) for _ in range(n)]
    (za_ref,), rwkv_p = take(1), take(N_RWKV_PARAMS)
    (zb_ref,), mlstm_p = take(1), take(N_MLSTM_PARAMS)
    ya_ref, yb_ref = take(2)
    rwkv_s, mlstm_s = take(N_RWKV_SCRATCH), take(N_MLSTM_SCRATCH)

    @pl.when(pl.program_id(1) == 0)
    def _():
        for ref in rwkv_s[:2] + mlstm_s[:4]:
            ref[...] = jnp.zeros_like(ref)

    _rwkv_kernel(za_ref, *rwkv_p, ya_ref, *rwkv_s, tc=tc)
    _mlstm_kernel(zb_ref, *mlstm_p, yb_ref, *mlstm_s, tc=tc)


def _recurrent(za, zb, pr, pm, bsz, t, tc):
    nt = t // tc
    full = lambda a: pl.BlockSpec(a.shape, lambda b, i: (0,) * a.ndim)
    tile = lambda w: pl.BlockSpec((tc, w), lambda b, i: (b * nt + i, 0))
    rwkv_p = [pr["mu"], pr["w0"], pr["w_up"], pr["a0"], pr["a_up"], pr["g_up"], pr["k_k"], pr["k_a"], pr["r_k"],
              pr["ln_w"], pr["ln_b"]]
    mlstm_p = [pm["conv_w"], pm["conv_b"], pm["gate_b"], pm["norm_g"]]
    assert len(rwkv_p) == N_RWKV_PARAMS and len(mlstm_p) == N_MLSTM_PARAMS
    return pl.pallas_call(
        functools.partial(_recurrent_kernel, tc=tc),
        grid=(bsz, nt),
        in_specs=[tile(ZA_W)] + [full(a) for a in rwkv_p] + [tile(ZB_W)] + [full(a) for a in mlstm_p],
        out_specs=[tile(MIX_W), tile(MIX_W)],
        out_shape=[jax.ShapeDtypeStruct((bsz * t, MIX_W), F32)] * 2,
        scratch_shapes=[
            pltpu.VMEM((SUBLANES, ZA_W), F32),
            pltpu.VMEM((MIX_W, MIX_W), F32),
            pltpu.VMEM((tc, MIX_W), F32),
            pltpu.VMEM((SUBLANES, MIX_W), F32),
            pltpu.VMEM((N_HEADS * MLSTM_DK, MIX_W), F32),
            pltpu.VMEM((N_HEADS * MLSTM_DK, MIX_W), F32),
            pltpu.VMEM((SUBLANES, MIX_W), F32),
            pltpu.VMEM((tc, MIX_W), F32),
        ],
        compiler_params=pltpu.CompilerParams(dimension_semantics=("arbitrary", "arbitrary")),
        name="recurrent",
    )(za, *rwkv_p, zb, *mlstm_p)


SWA_SUB_BLOCKS = 4


def _t5_bucket(dist):
    max_exact = REL_BUCKETS // 2
    d = np.maximum(dist, 1).astype(np.float32)
    large = max_exact + (np.log(d / max_exact) / math.log(REL_MAX_DIST / max_exact)
                         * (REL_BUCKETS - max_exact)).astype(np.int32)
    large = np.minimum(large, REL_BUCKETS - 1)
    return np.where(dist < max_exact, dist, large).astype(np.int32)


def _swa_kernel(rb_ref, sink_ref, bucket_ref, q_ref, kp_ref, kc_ref, vp_ref, vc_ref, o_ref, bias_ref):
    blk = ATTN_BLOCK
    grp = N_HEADS // SWA_KV_HEADS
    kvw = SWA_KV_HEADS * HEAD_DIM
    first = (pl.program_id(0) == 0) & (pl.program_id(1) == 0)

    @pl.when(first)
    def _():
        bucket = bucket_ref[...]
        for h in range(N_HEADS):
            acc = jnp.full((2 * blk, blk), -jnp.inf, F32)
            for bk in range(REL_BUCKETS):
                acc = jnp.where(bucket == bk, rb_ref[bk, h], acc)
            bias_ref[h // grp, :, (h % grp) * blk:(h % grp + 1) * blk] = acc

    key = lax.broadcasted_iota(jnp.int32, (2 * blk, grp * blk), 0)
    live = (key >= blk) | (pl.program_id(1) > 0)
    member = lax.broadcasted_iota(jnp.int32, (1, grp * blk), 1) // blk
    kw = jnp.concatenate([kp_ref[...], kc_ref[...]], axis=0).astype(BF16)
    vw = jnp.concatenate([vp_ref[...], vc_ref[...]], axis=0).astype(BF16)
    lane_member = lax.broadcasted_iota(jnp.int32, (1, grp * HEAD_DIM), 1) // HEAD_DIM
    rr = lax.broadcasted_iota(jnp.int32, (kvw, grp * HEAD_DIM), 0)
    cc = lax.broadcasted_iota(jnp.int32, (kvw, grp * HEAD_DIM), 1)
    vr = lax.broadcasted_iota(jnp.int32, (HEAD_DIM, kvw), 0)
    vc_ = lax.broadcasted_iota(jnp.int32, (HEAD_DIM, kvw), 1)
    n_sub = q_ref.shape[0] // blk
    k_rep, v_t, sinks = [], [], []
    for j in range(SWA_KV_HEADS):
        spread = ((_idiv(rr, HEAD_DIM) == j) & (_imod(rr, HEAD_DIM) == _imod(cc, HEAD_DIM))).astype(BF16)
        pick = ((_idiv(vc_, HEAD_DIM) == j) & (_imod(vc_, HEAD_DIM) == vr)).astype(BF16)
        k_rep.append(_dot(kw, spread).astype(BF16))
        v_t.append(_dot_nt(pick, vw).astype(BF16))
        sinks.append(jnp.where(member == 0, sink_ref[j * grp], sink_ref[j * grp + 1]))
    pairs = [(n, j) for n in range(n_sub) for j in range(SWA_KV_HEADS)]
    scores = []
    for n, j in pairs:
        qp = q_ref[n * blk:(n + 1) * blk, j * grp * HEAD_DIM:(j + 1) * grp * HEAD_DIM] * (HEAD_DIM ** -0.5)
        q_stack = jnp.concatenate([jnp.where(lane_member == g, qp, 0.0) for g in range(grp)],
                                  axis=0).astype(BF16)
        s = _dot_nt(k_rep[j][n * blk:(n + 2) * blk], q_stack) + bias_ref[j]
        scores.append(jnp.where(live, s, -jnp.inf) if n == 0 else s)
    probs = []
    for (n, j), s in zip(pairs, scores):
        m = jnp.maximum(jnp.max(s, axis=0, keepdims=True), sinks[j])
        p = jnp.exp(s - m)
        denom = jnp.sum(p, axis=0, keepdims=True) + jnp.exp(sinks[j] - m)
        probs.append((p / denom).astype(BF16))
    outs = [_dot(v_t[j][:, n * blk:(n + 2) * blk], p) for (n, j), p in zip(pairs, probs)]
    for n in range(n_sub):
        heads = [outs[n * SWA_KV_HEADS + j][:, g * blk:(g + 1) * blk]
                 for j in range(SWA_KV_HEADS) for g in range(grp)]
        o_ref[n * blk:(n + 1) * blk, :] = jnp.concatenate(heads, axis=0).T


def _swa(zc, rel_bias, sinks, bsz, t):
    blk = ATTN_BLOCK
    nb = t // blk
    assert N_HEADS // SWA_KV_HEADS == 2
    dist = np.arange(blk)[None, :] + blk - np.arange(2 * blk)[:, None]
    bucket = jnp.asarray(np.where((dist >= 0) & (dist < SWA_WINDOW),
                                  _t5_bucket(np.clip(dist, 0, SWA_WINDOW - 1)), -1).astype(np.int32))
    kvw = SWA_KV_HEADS * HEAD_DIM
    kcol = MIX_W // kvw
    n_sub = min(SWA_SUB_BLOCKS, nb)
    assert nb % n_sub == 0
    ns = nb // n_sub
    rows = n_sub * blk
    cur = lambda b, i: b * ns + i
    prev = lambda b, i: b * nb + jnp.maximum(i * n_sub - 1, 0)
    return pl.pallas_call(
        _swa_kernel,
        grid=(bsz, ns),
        in_specs=[
            pl.BlockSpec(memory_space=pltpu.SMEM),
            pl.BlockSpec(memory_space=pltpu.SMEM),
            pl.BlockSpec((2 * blk, blk), lambda b, i: (0, 0)),
            pl.BlockSpec((rows, MIX_W), lambda b, i: (cur(b, i), 0)),
            pl.BlockSpec((blk, kvw), lambda b, i: (prev(b, i), kcol)),
            pl.BlockSpec((rows, kvw), lambda b, i: (cur(b, i), kcol)),
            pl.BlockSpec((blk, kvw), lambda b, i: (prev(b, i), kcol + 1)),
            pl.BlockSpec((rows, kvw), lambda b, i: (cur(b, i), kcol + 1)),
        ],
        out_specs=pl.BlockSpec((rows, MIX_W), lambda b, i: (cur(b, i), 0)),
        out_shape=jax.ShapeDtypeStruct((bsz * t, MIX_W), F32),
        scratch_shapes=[pltpu.VMEM((SWA_KV_HEADS, 2 * blk, 2 * blk), F32)],
        compiler_params=pltpu.CompilerParams(dimension_semantics=("arbitrary", "arbitrary")),
        name="swa",
    )(rel_bias, sinks, bucket, zc, zc, zc, zc, zc)


FOX_BLOCK = 256
FOX_AUG = LANES
FOX_PARTS = 3
FOX_KEY_SPAN = 2


def _fox_placements():
    wide = N_HEADS * FOX_AUG
    pk = np.zeros((MIX_W, wide), np.float32)
    pck = np.zeros((LANES, wide), np.float32)
    ones_k = np.zeros((SUBLANES, wide), np.float32)
    pq_t = np.zeros((wide, MIX_W), np.float32)
    pv_t = np.zeros((wide, MIX_W), np.float32)
    pcq_t = np.zeros((wide, LANES), np.float32)
    ones_t = np.zeros((2, wide, LANES), np.float32)
    for h in range(N_HEADS):
        base = h * FOX_AUG
        for d in range(HEAD_DIM):
            pk[h * HEAD_DIM + d, base + d] = 1.0
            pq_t[base + d, h * HEAD_DIM + d] = HEAD_DIM ** -0.5
            pv_t[base + d, h * HEAD_DIM + d] = 1.0
        for n in range(FOX_PARTS):
            pcq_t[base + HEAD_DIM + n, n * N_HEADS + h] = 1.0
            pck[n * N_HEADS + h, base + HEAD_DIM + FOX_PARTS + n] = -1.0
            ones_t[0, base + HEAD_DIM + FOX_PARTS + n, :] = 1.0
            ones_k[0, base + HEAD_DIM + n] = 1.0
        ones_t[1, base + HEAD_DIM, :] = 1.0
    bf = lambda a: jnp.asarray(a, BF16)
    return bf(pk), bf(pck), jnp.asarray(ones_k), bf(np.stack([pq_t, pv_t])), bf(pcq_t), jnp.asarray(ones_t)


def _fox_kernel(q_ref, k_ref, v_ref, f_ref, fb_ref, pk_ref, pck_ref, onesk_ref, pqv_ref, pcq_ref, onest_ref,
                o_ref, kaug_ref, vaug_ref, m_ref, acc_ref, clast_ref, s_ref, p_ref):
    blk = FOX_BLOCK
    i = pl.program_id(1)

    @pl.when(i == 0)
    def _():
        clast_ref[...] = jnp.zeros_like(clast_ref)

    ls = _log_sigmoid(f_ref[...] + fb_ref[...])
    row = lax.broadcasted_iota(jnp.int32, (blk, blk), 0)
    col = lax.broadcasted_iota(jnp.int32, (blk, blk), 1)
    cq = _split_dot((col <= row).astype(BF16), ls, FOX_PARTS) + clast_ref[0:1, :]
    clast_ref[0:1, :] = cq[blk - 1:blk, :]

    widen = lambda a: jnp.concatenate([a] * (blk // LANES), axis=1)
    qa = _dot_nt(pqv_ref[0], q_ref[...].astype(BF16)) + widen(onest_ref[0])
    va = _dot_nt(pqv_ref[1], v_ref[...].astype(BF16)) + widen(onest_ref[1])
    ka = _dot(k_ref[...].astype(BF16), pk_ref[...]) + onesk_ref[0:1, :]
    lane = lax.broadcasted_iota(jnp.int32, (1, LANES), 1)
    pieces = _split_bf16(jnp.where(lane < N_HEADS, cq, 0.0), FOX_PARTS)
    packed = pieces[0].astype(F32)
    for n in range(1, FOX_PARTS):
        packed = packed + pltpu.roll(pieces[n].astype(F32), n * N_HEADS, axis=1)
    packed = packed.astype(BF16)
    qa = (qa + _dot_nt(pcq_ref[...], packed)).astype(BF16)
    ka = ka + _dot(packed, pck_ref[...])
    row0 = pl.multiple_of(i * blk, blk)
    for h in range(N_HEADS):
        hs = slice(h * FOX_AUG, (h + 1) * FOX_AUG)
        kaug_ref[h, pl.ds(row0, blk), :] = ka[:, hs].astype(BF16)
        vaug_ref[h, :, pl.ds(row0, blk)] = va[hs, :].astype(BF16)

    key_le_query = row <= col

    def attend(off, nk, diagonal):
        m_new = []
        for h in range(N_HEADS):
            s = _dot(kaug_ref[h, pl.ds(off, nk), :], qa[h * FOX_AUG:(h + 1) * FOX_AUG, :])
            s = jnp.where(key_le_query, s, -jnp.inf) if diagonal else s
            s_ref[h, 0:nk, :] = s
            m_new.append(jnp.maximum(m_ref[h], jnp.max(s, axis=0, keepdims=True)))
        for h in range(N_HEADS):
            p_ref[h, 0:nk, :] = jnp.exp(s_ref[h, 0:nk, :] - m_new[h]).astype(BF16)
        for h in range(N_HEADS):
            pv = _dot(vaug_ref[h, :, pl.ds(off, nk)], p_ref[h, 0:nk, :])
            acc_ref[h] = jnp.exp(m_ref[h] - m_new[h]) * acc_ref[h] + pv
            m_ref[h] = m_new[h]

    m_ref[...] = jnp.full_like(m_ref, -jnp.inf)
    acc_ref[...] = jnp.zeros_like(acc_ref)
    attend(row0, blk, True)

    def body(j, carry):
        attend(pl.multiple_of(j * FOX_KEY_SPAN * blk, FOX_KEY_SPAN * blk), FOX_KEY_SPAN * blk, False)
        return carry

    lax.fori_loop(0, i // FOX_KEY_SPAN, body, 0)
    for r in range(1, FOX_KEY_SPAN):
        @pl.when(i % FOX_KEY_SPAN >= r)
        def _():
            attend(pl.multiple_of((i - r) * blk, blk), blk, False)
    outs = []
    for h in range(N_HEADS):
        acc = acc_ref[h]
        outs.append(acc[0:HEAD_DIM, :] / acc[HEAD_DIM:HEAD_DIM + 1, :])
    o_ref[...] = jnp.concatenate(outs, axis=0).T


def _fox(zd, fb, bsz, t):
    blk = FOX_BLOCK
    assert t % blk == 0
    nb = t // blk
    consts = _fox_placements()
    const = lambda a: pl.BlockSpec(a.shape, lambda b, i: (0,) * a.ndim)
    cur = lambda c: (lambda b, i: (b * nb + i, c))
    return pl.pallas_call(
        _fox_kernel,
        grid=(bsz, nb),
        in_specs=[
            pl.BlockSpec((blk, MIX_W), cur(0)),
            pl.BlockSpec((blk, MIX_W), cur(1)),
            pl.BlockSpec((blk, MIX_W), cur(2)),
            pl.BlockSpec((blk, LANES), cur(3 * MIX_W // LANES)),
            const(fb)] + [const(a) for a in consts],
        out_specs=pl.BlockSpec((blk, MIX_W), cur(0)),
        out_shape=jax.ShapeDtypeStruct((bsz * t, MIX_W), F32),
        scratch_shapes=[
            pltpu.VMEM((N_HEADS, t, FOX_AUG), BF16),
            pltpu.VMEM((N_HEADS, FOX_AUG, t), BF16),
            pltpu.VMEM((N_HEADS, 1, blk), F32),
            pltpu.VMEM((N_HEADS, FOX_AUG, blk), F32),
            pltpu.VMEM((SUBLANES, LANES), F32),
            pltpu.VMEM((N_HEADS, FOX_KEY_SPAN * blk, blk), F32),
            pltpu.VMEM((N_HEADS, FOX_KEY_SPAN * blk, blk), BF16),
        ],
        compiler_params=pltpu.CompilerParams(dimension_semantics=("arbitrary", "arbitrary")),
        name="fox",
    )(zd, zd, zd, zd, fb, *consts)


FFN_COL_CHUNK = 6 * MXU_COLS


def _outproj_ffn_kernel(x_ref, ya_ref, yb_ref, yc_ref, yd_ref, wo_ref, gmix_ref, gpre_ref, wg_ref, wu_ref, cw_ref,
                        cb_ref, wd_ref, gpost_ref, o_ref, tail_ref, *, tm, t):
    mixed = None
    for n, ref in enumerate((ya_ref, yb_ref, yc_ref, yd_ref)):
        part = _dot(ref[...].astype(BF16), wo_ref[n * MIX_W:(n + 1) * MIX_W, :])
        mixed = part if mixed is None else mixed + part
    x = x_ref[...] + mixed * _rms_scale(mixed) * gmix_ref[...]
    h = (x * _rms_scale(x) * gpre_ref[...]).astype(BF16)
    seq_start = (pl.program_id(0) * tm) % t == 0

    @pl.when(pl.program_id(0) == 0)
    def _():
        tail_ref[...] = jnp.zeros_like(tail_ref)

    acc = None
    for c0 in range(0, D_FF, FFN_COL_CHUNK):
        cs = slice(c0, min(c0 + FFN_COL_CHUNK, D_FF))
        gate = _dot(h, wg_ref[:, cs])
        halo = jnp.where(seq_start, 0.0, tail_ref[:, cs])
        tail_ref[:, cs] = gate[tm - SUBLANES:, :]
        conv = cb_ref[:, cs] + cw_ref[FFN_CONV - 1:FFN_CONV, cs] * gate
        for d in range(1, FFN_CONV):
            conv = conv + cw_ref[FFN_CONV - 1 - d:FFN_CONV - d, cs] * _shift_rows(gate, halo, d)
        f = jax.nn.gelu(conv, approximate=True) * _dot(h, wu_ref[:, cs])
        part = _dot(f.astype(BF16), wd_ref[cs, :])
        acc = part if acc is None else acc + part
    o_ref[...] = x + acc * _rms_scale(acc) * gpost_ref[...]


def _outproj_ffn(x2, ys, w_out, g_mix, p, t, tm):
    rows = x2.shape[0]
    const = lambda a: pl.BlockSpec(a.shape, lambda i: (0, 0), pipeline_mode=pl.Buffered(1))
    params = [w_out, g_mix, p["g_pre"], p["w_gate"], p["w_up"], p["conv_w"], p["conv_b"], p["w_down"], p["g_post"]]
    return pl.pallas_call(
        functools.partial(_outproj_ffn_kernel, tm=tm, t=t),
        grid=(rows // tm,),
        in_specs=[pl.BlockSpec((tm, D_MODEL), lambda i: (i, 0))]
        + [pl.BlockSpec((tm, MIX_W), lambda i: (i, 0))] * len(ys)
        + [const(a) for a in params],
        out_specs=pl.BlockSpec((tm, D_MODEL), lambda i: (i, 0)),
        out_shape=jax.ShapeDtypeStruct((rows, D_MODEL), F32),
        scratch_shapes=[pltpu.VMEM((SUBLANES, D_FF), F32)],
        compiler_params=pltpu.CompilerParams(dimension_semantics=("arbitrary",),
                                             vmem_limit_bytes=56 * 1024 * 1024),
        name="outproj_ffn",
    )(x2, *ys, *params)


def _pad_cols(a, width):
    return jnp.pad(a, ((0, 0), (0, width - a.shape[1])))


def _row(a):
    return a.reshape(1, -1).astype(F32)


def _layer_params(l, w_in, w_out, norm_mix_pre, norm_mix_post, norm_ffn_pre, norm_ffn_post,
                  rwkv_mu, rwkv_w0, rwkv_w_up, rwkv_a0, rwkv_a_up, rwkv_g_up, rwkv_k_k, rwkv_k_a,
                  rwkv_r_k, rwkv_ln_w, rwkv_ln_b, mlstm_conv_w, mlstm_conv_b, mlstm_b_i, mlstm_b_f,
                  mlstm_norm, swa_sinks, fox_b_f, ffn_w_up, ffn_conv_w, ffn_conv_b, ffn_w_down):
    wa_w = ZA_W
    wb_w = 3 * MIX_W + 2 * N_HEADS
    wc_w = ZC_W
    wi = jnp.transpose(w_in, (2, 0, 1))[:, l, :].astype(BF16)
    pad_rows = lambda a, n: jnp.pad(a, ((0, n - a.shape[0]), (0, 0)))
    groups = [wi[:wa_w],
              pad_rows(wi[wa_w:wa_w + wb_w], ZB_W),
              wi[wa_w + wb_w:wa_w + wb_w + wc_w],
              pad_rows(wi[wa_w + wb_w + wc_w:], ZD_W)]
    gate_b = _pad_cols(jnp.concatenate([mlstm_b_i[l], mlstm_b_f[l]]).reshape(1, -1), LANES)
    return {
        "w_in": jnp.concatenate(groups, axis=0),
        "g_mix_pre": _row(norm_mix_pre[l]),
        "g_mix_post": _row(norm_mix_post[l]),
        "w_out": w_out[l].astype(BF16),
        "rwkv": {"mu": _row(rwkv_mu[l]), "w0": _row(rwkv_w0[l]), "w_up": rwkv_w_up[l].astype(BF16),
                 "a0": _row(rwkv_a0[l]), "a_up": rwkv_a_up[l].astype(BF16), "g_up": rwkv_g_up[l].astype(BF16),
                 "k_k": _row(rwkv_k_k[l]), "k_a": _row(rwkv_k_a[l]), "r_k": _row(rwkv_r_k[l]),
                 "ln_w": _row(rwkv_ln_w[l]), "ln_b": _row(rwkv_ln_b[l])},
        "mlstm": {"conv_w": mlstm_conv_w[l], "conv_b": _row(mlstm_conv_b[l]), "gate_b": gate_b,
                  "norm_g": _row(mlstm_norm[l])},
        "swa_sinks": swa_sinks[l],
        "fox_b": _pad_cols(fox_b_f[l].reshape(1, -1), LANES),
        "ffn": {"g_pre": _row(norm_ffn_pre[l]), "w_gate": ffn_w_up[l][:, :D_FF].astype(BF16),
                "w_up": ffn_w_up[l][:, D_FF:].astype(BF16), "conv_w": ffn_conv_w[l],
                "conv_b": _row(ffn_conv_b[l]), "w_down": ffn_w_down[l].astype(BF16),
                "g_post": _row(norm_ffn_post[l])},
    }


def _tiles(t):
    return min(512, t), min(512, t), min(512, t)


def kernel(x, w_in, w_out, norm_mix_pre, norm_mix_post, norm_ffn_pre, norm_ffn_post, rwkv_mu, rwkv_w0, rwkv_w_up, rwkv_a0, rwkv_a_up, rwkv_g_up, rwkv_k_k, rwkv_k_a, rwkv_r_k, rwkv_ln_w, rwkv_ln_b, mlstm_conv_w, mlstm_conv_b, mlstm_b_i, mlstm_b_f, mlstm_norm, swa_sinks, fox_b_f, rel_bias, ffn_w_up, ffn_conv_w, ffn_conv_b, ffn_w_down):
    bsz, t, d = x.shape
    assert d == D_MODEL and t % ATTN_BLOCK == 0
    tm, tc, tf = _tiles(t)
    x2 = x.reshape(bsz * t, d)
    for l in range(w_in.shape[0]):
        p = _layer_params(l, w_in, w_out, norm_mix_pre, norm_mix_post, norm_ffn_pre, norm_ffn_post,
                          rwkv_mu, rwkv_w0, rwkv_w_up, rwkv_a0, rwkv_a_up, rwkv_g_up, rwkv_k_k, rwkv_k_a,
                          rwkv_r_k, rwkv_ln_w, rwkv_ln_b, mlstm_conv_w, mlstm_conv_b, mlstm_b_i, mlstm_b_f,
                          mlstm_norm, swa_sinks, fox_b_f, ffn_w_up, ffn_conv_w, ffn_conv_b, ffn_w_down)
        za, zb, zc, zd = _inproj(x2, p["g_mix_pre"], p["w_in"], tm)
        ya, yb = _recurrent(za, zb, p["rwkv"], p["mlstm"], bsz, t, tc)
        yc = _swa(zc, rel_bias, p["swa_sinks"], bsz, t)
        yd = _fox(zd, p["fox_b"], bsz, t)
        x2 = _outproj_ffn(x2, (ya, yb, yc, yd), p["w_out"], p["g_mix_post"], p["ffn"], t, tf)
    return x2.reshape(bsz, t, d)
```

```python
import functools
import math

import jax
import jax.numpy as jnp
import numpy as np
from jax import lax
from jax.experimental import pallas as pl
from jax.experimental.pallas import tpu as pltpu

F32 = jnp.float32
BF16 = jnp.bfloat16

D_MODEL = 1024
HEAD_DIM = 64
N_HEADS = 4
MIX_W = N_HEADS * HEAD_DIM
RWKV_DECAY_RANK = 64
RWKV_AAA_RANK = 64
RWKV_GATE_RANK = 128
RWKV_LN_EPS = 64e-5
RWKV_CHUNK = 64
MLSTM_DK = 32
MLSTM_CONV = 4
MLSTM_CHUNK = 64
GATE_SOFTCAP = 15.0
SWA_KV_HEADS = 2
SWA_WINDOW = 128
ATTN_BLOCK = 128
REL_BUCKETS = 32
REL_MAX_DIST = 128
D_FF = 2816
FFN_CONV = 3
NORM_EPS = 1e-6

LANES = 128
SUBLANES = 8
MXU_COLS = 256

ZA_W = 3 * MIX_W + RWKV_DECAY_RANK + RWKV_AAA_RANK + RWKV_GATE_RANK
ZB_W = 3 * MIX_W + LANES
ZC_W = MIX_W + 2 * SWA_KV_HEADS * HEAD_DIM
ZD_W = 3 * MIX_W + LANES
Z_W = ZA_W + ZB_W + ZC_W + ZD_W

def _split_bf16(x, parts):
    out = []
    for n in range(parts):
        piece = x.astype(BF16)
        out.append(piece)
        if n + 1 < parts:
            x = x - piece.astype(F32)
    return out


def _dot(a, b, precision=None):
    return jnp.dot(a, b, preferred_element_type=F32, precision=precision)


def _dot_nt(a, b, precision=None):
    return lax.dot_general(a, b, (((1,), (1,)), ((), ())), preferred_element_type=F32, precision=precision)


def _dot_tn(a, b, precision=None):
    return lax.dot_general(a, b, (((0,), (0,)), ((), ())), preferred_element_type=F32, precision=precision)


def _dot_split(x, ones, parts):
    return sum(_dot(piece, ones) for piece in _split_bf16(x, parts))


def _split_dot(ones, x, parts):
    return sum(_dot(ones, piece) for piece in _split_bf16(x, parts))


def _bdot(a, b):
    return _dot(a.astype(BF16), b.astype(BF16))


def _bdot_nt(a, b):
    return _dot_nt(a.astype(BF16), b.astype(BF16))


def _bdot_tn(a, b):
    return _dot_tn(a.astype(BF16), b.astype(BF16))


def _sigmoid(x):
    return 1.0 / (1.0 + jnp.exp(-x))


def _log_sigmoid(x):
    return jnp.minimum(x, 0.0) - jnp.log(1.0 + jnp.exp(-jnp.abs(x)))


def _softplus(x):
    return jnp.maximum(x, 0.0) + jnp.log(1.0 + jnp.exp(-jnp.abs(x)))


def _rms_scale(x):
    return lax.rsqrt(jnp.mean(x * x, axis=-1, keepdims=True) + NORM_EPS)


def _shift_rows(x, prev_tail, d):
    rolled = pltpu.roll(x, d, axis=0)
    head_rows = lax.broadcasted_iota(jnp.int32, (SUBLANES, 1), 0)
    head = jnp.where(head_rows < d, pltpu.roll(prev_tail, d, axis=0), rolled[:SUBLANES])
    return jnp.concatenate([head, rolled[SUBLANES:]], axis=0)


def _idiv(x, n):
    assert n & (n - 1) == 0
    return lax.shift_right_logical(x, jnp.int32(n.bit_length() - 1))


def _imod(x, n):
    assert n & (n - 1) == 0
    return lax.bitwise_and(x, jnp.int32(n - 1))


def _head_ones(width, head):
    r = _idiv(lax.broadcasted_iota(jnp.int32, (width, width), 0), head)
    c = _idiv(lax.broadcasted_iota(jnp.int32, (width, width), 1), head)
    return (r == c).astype(BF16)


def _inproj_kernel(x_ref, g_ref, w_ref, za_ref, zb_ref, zc_ref, zd_ref):
    x = x_ref[...]
    h = (x * _rms_scale(x) * g_ref[...]).astype(BF16)
    z = _dot_nt(h, w_ref[...])
    off = 0
    for ref in (za_ref, zb_ref, zc_ref, zd_ref):
        w = ref.shape[1]
        ref[...] = z[:, off:off + w]
        off += w


def _inproj(x2, g, w, tm):
    rows = x2.shape[0]
    return pl.pallas_call(
        _inproj_kernel,
        grid=(rows // tm,),
        in_specs=[
            pl.BlockSpec((tm, D_MODEL), lambda i: (i, 0)),
            pl.BlockSpec((1, D_MODEL), lambda i: (0, 0)),
            pl.BlockSpec((Z_W, D_MODEL), lambda i: (0, 0)),
        ],
        out_specs=[pl.BlockSpec((tm, w_), lambda i: (i, 0)) for w_ in (ZA_W, ZB_W, ZC_W, ZD_W)],
        out_shape=[jax.ShapeDtypeStruct((rows, w_), F32) for w_ in (ZA_W, ZB_W, ZC_W, ZD_W)],
        compiler_params=pltpu.CompilerParams(dimension_semantics=("arbitrary",)),
        name="inproj",
    )(x2, g, w)


def _rwkv_kernel(z_ref, mu_ref, w0_ref, wup_ref, a0_ref, aup_ref, gup_ref, kk_ref, ka_ref, rk_ref,
                 lnw_ref, lnb_ref, o_ref, tail_ref, st_ref, y_ref, *, tc):
    L = RWKV_CHUNK
    W = MIX_W

    z = z_ref[...]
    zz = z + mu_ref[...] * (_shift_rows(z, tail_ref[...], 1) - z)
    tail_ref[...] = z[tc - SUBLANES:, :]

    r = zz[:, 0:W]
    k = zz[:, W:2 * W]
    v = zz[:, 2 * W:3 * W]
    o1 = 3 * W
    wd = zz[:, o1:o1 + RWKV_DECAY_RANK]
    ad = zz[:, o1 + RWKV_DECAY_RANK:o1 + RWKV_DECAY_RANK + RWKV_AAA_RANK]
    gd = zz[:, o1 + RWKV_DECAY_RANK + RWKV_AAA_RANK:]

    lw = -jnp.exp(-_softplus(-(w0_ref[...] + _bdot(jnp.tanh(wd), wup_ref[...]))) - 0.5)
    alpha = _sigmoid(a0_ref[...] + _bdot(ad, aup_ref[...]))
    g = _bdot(_sigmoid(gd), gup_ref[...])

    hsum = _head_ones(W, HEAD_DIM)
    kk = k * kk_ref[...]
    kk = kk * lax.rsqrt(jnp.maximum(_dot_split(kk * kk, hsum, 2), 1e-24))
    k = k * (1.0 + (alpha - 1.0) * ka_ref[...])

    span = min(tc, MXU_COLS)
    rt = lax.broadcasted_iota(jnp.int32, (span, span), 0)
    ct = lax.broadcasted_iota(jnp.int32, (span, span), 1)
    tri = ((_idiv(rt, L) == _idiv(ct, L)) & (ct <= rt)).astype(BF16)
    cum = jnp.concatenate([_split_dot(tri, lw[r0:r0 + span], 3) for r0 in range(0, tc, span)], axis=0)
    e_in = jnp.exp(cum)
    e_out = jnp.exp(-cum)
    r_t = r * e_in
    a_t = -kk * jnp.exp(cum - lw)
    b_t = kk * alpha * e_out
    k_t = k * e_out

    lane_head = lax.broadcasted_iota(jnp.int32, (1, W), 1) // HEAD_DIM

    def stack(xc):
        return jnp.concatenate([jnp.where(lane_head == h, xc, 0.0) for h in range(N_HEADS)], axis=0)

    n = N_HEADS * L
    rr = lax.broadcasted_iota(jnp.int32, (n, n), 0)
    cc = lax.broadcasted_iota(jnp.int32, (n, n), 1)
    own_head = _idiv(rr, L) == _idiv(cc, L)
    strict = own_head & (_imod(rr, L) > _imod(cc, L))
    incl = own_head & (_imod(rr, L) >= _imod(cc, L))
    eye = (rr == cc).astype(F32)

    chunks = [slice(c * L, (c + 1) * L) for c in range(tc // L)]
    stk = [tuple(stack(u[sl]).astype(BF16) for u in (a_t, r_t, b_t, k_t, v)) for sl in chunks]
    bk_rep = [jnp.concatenate([b_t[sl]] * 2 + [k_t[sl]] * 2, axis=0).astype(BF16) for sl in chunks]
    ar = [_dot_nt(jnp.concatenate([s_[0], s_[1]], axis=0), x) for s_, x in zip(stk, bk_rep)]
    over_heads = lambda x: jnp.concatenate([x, x], axis=1)
    a_ab = [jnp.where(strict, over_heads(x[:n, :2 * L]), 0.0) for x in ar]
    a_ak = [jnp.where(strict, over_heads(x[:n, 2 * L:]), 0.0).astype(BF16) for x in ar]
    a_rb = [jnp.where(incl, over_heads(x[n:, :2 * L]), 0.0).astype(BF16) for x in ar]
    a_rk = [jnp.where(incl, over_heads(x[n:, 2 * L:]), 0.0).astype(BF16) for x in ar]
    inv = [eye + m for m in a_ab]
    pw = [m.astype(BF16) for m in a_ab]
    for _ in range(int(math.log2(L)) - 1):
        pw = [_dot(m, m).astype(BF16) for m in pw]
        inv = [t_ + _dot(t_.astype(BF16), m) for t_, m in zip(inv, pw)]
    inv = [t_.astype(BF16) for t_ in inv]
    akv = [_dot(m, s_[4]).astype(BF16) for m, s_ in zip(a_ak, stk)]
    wu = [_dot(t_, jnp.concatenate([s_[0], x], axis=1)) for t_, s_, x in zip(inv, stk, akv)]
    w_m = [x[:, :W].astype(BF16) for x in wu]
    u2 = [x[:, W:].astype(BF16) for x in wu]
    r_eff = [(s_[1].astype(F32) + _dot(m, w_)).astype(BF16) for s_, m, w_ in zip(stk, a_rb, w_m)]
    y_own = [_dot(m, u_) + _dot(n_, s_[4]) for m, u_, n_, s_ in zip(a_rb, u2, a_rk, stk)]
    st_mix = [_dot_tn(w_, s_[2]).astype(BF16) for w_, s_ in zip(w_m, stk)]
    st_own = [_dot_tn(u_, s_[2]) + _dot_tn(s_[4], s_[3]) for u_, s_ in zip(u2, stk)]

    st = st_ref[...]
    for c, sl in enumerate(chunks):
        st_b = st.astype(BF16)
        y_s = _dot_nt(r_eff[c], st_b) + y_own[c]
        st = (st + _dot(st_b, st_mix[c]) + st_own[c]) * e_in[(c + 1) * L - 1:(c + 1) * L, :]
        y_c = y_s[0:L]
        for h in range(1, N_HEADS):
            y_c = y_c + y_s[h * L:(h + 1) * L]
        y_ref[sl, :] = y_c
    st_ref[...] = st

    y = y_ref[...]
    inv_n = 1.0 / HEAD_DIM
    mean = _dot_split(y, hsum, 2) * inv_n
    yc = y - mean
    var = _dot_split(yc * yc, hsum, 2) * inv_n
    y = yc * lax.rsqrt(var + RWKV_LN_EPS) * lnw_ref[...] + lnb_ref[...]
    bonus = _dot_split(r * k * rk_ref[...], hsum, 2) * v
    o_ref[...] = (y + bonus) * g


def _mlstm_kernel(z_ref, cw_ref, cb_ref, gb_ref, ng_ref, o_ref, tail_ref, c_ref, n_ref, m_ref, h_ref, *, tc):
    L = MLSTM_CHUNK
    W = MIX_W
    DK = MLSTM_DK
    DV = HEAD_DIM

    qk_in = z_ref[:, 0:W]
    v = z_ref[:, W:2 * W]
    og = z_ref[:, 2 * W:3 * W]
    gates = z_ref[:, 3 * W:3 * W + LANES]

    tail = tail_ref[...]
    conv = cb_ref[...] + cw_ref[MLSTM_CONV - 1:MLSTM_CONV, :] * qk_in
    for d in range(1, MLSTM_CONV):
        conv = conv + cw_ref[MLSTM_CONV - 1 - d:MLSTM_CONV - d, :] * _shift_rows(qk_in, tail, d)
    tail_ref[...] = qk_in[tc - SUBLANES:, :]
    qk = conv * _sigmoid(conv)
    q = qk[:, 0:N_HEADS * DK] * (DK ** -0.5)
    k = qk[:, N_HEADS * DK:]

    capped = GATE_SOFTCAP * jnp.tanh((gates + gb_ref[...]) / GATE_SOFTCAP)
    lf = _log_sigmoid(capped)

    gate_col = lax.broadcasted_iota(jnp.int32, (LANES, W), 0)
    lane_head = _idiv(lax.broadcasted_iota(jnp.int32, (LANES, W), 1), DV)
    pick_i = (gate_col == lane_head).astype(BF16)
    pick_f = (gate_col == N_HEADS + lane_head).astype(BF16)
    li_e = _dot_split(capped, pick_i, 3)
    span = min(tc, MXU_COLS)
    rt = lax.broadcasted_iota(jnp.int32, (span, span), 0)
    ct = lax.broadcasted_iota(jnp.int32, (span, span), 1)
    tri = ((_idiv(rt, L) == _idiv(ct, L)) & (ct <= rt)).astype(BF16)
    lf_e = [_dot(piece, pick_f).astype(BF16) for piece in _split_bf16(lf, 3)]
    b_e = jnp.concatenate([sum(_dot(tri, piece[r0:r0 + span]) for piece in lf_e)
                           for r0 in range(0, tc, span)], axis=0)

    key = lax.broadcasted_iota(jnp.int32, (L, W), 0)
    query = _imod(lax.broadcasted_iota(jnp.int32, (L, W), 1), L)
    on_diag = key == query
    causal_t = key <= query
    head_ones = _head_ones(W, DV)
    wide = lambda m: jnp.concatenate([m, m], axis=1)
    wide_lane_head = wide(lax.broadcasted_iota(jnp.int32, (1, W), 1) // DV)
    same_head_k = wide(_idiv(lax.broadcasted_iota(jnp.int32, (N_HEADS * DK, W), 0), DK)
                       == _idiv(lax.broadcasted_iota(jnp.int32, (N_HEADS * DK, W), 1), DV))
    q_lane_head = lax.broadcasted_iota(jnp.int32, (1, N_HEADS * DK), 1) // DK
    ones_b = jnp.ones((L, W), BF16)

    c_st = c_ref[...]
    n_st = n_ref[...]
    m_st = m_ref[0:1, :]
    for c in range(tc // L):
        sl = slice(c * L, (c + 1) * L)
        bc, lic, vc = b_e[sl], li_e[sl], v[sl]
        qc = q[sl]
        kc = k[sl].astype(BF16)
        q_stack = jnp.concatenate([jnp.where(q_lane_head == h, qc, 0.0) for h in range(N_HEADS)],
                                  axis=0).astype(BF16)
        b_q = jnp.sum(jnp.where(on_diag, bc, 0.0), axis=0, keepdims=True)
        dmat = jnp.where(causal_t, b_q - bc + lic, -jnp.inf)
        m_t = jnp.maximum(b_q + m_st, jnp.max(dmat, axis=0, keepdims=True))
        s_t = (_dot_nt(kc, q_stack) * jnp.exp(dmat - m_t)).astype(BF16)
        m_e = sum(_dot(jnp.where(on_diag, piece.astype(F32), 0.0).astype(BF16), head_ones)
                  for piece in _split_bf16(m_t, 3))
        inter = jnp.exp(bc + m_st - m_e)
        nv = _dot_tn(s_t, jnp.concatenate([vc.astype(BF16), ones_b], axis=1))
        nv = functools.reduce(lambda lo, h: jnp.where(wide_lane_head == h, nv[h * L:(h + 1) * L], lo),
                              range(1, N_HEADS), nv[0:L])
        qcn = _dot(qc.astype(BF16), jnp.concatenate([c_st, n_st], axis=1).astype(BF16))
        num = nv[:, :W] + inter * qcn[:, :W]
        den = nv[:, W:] + inter * qcn[:, W:]
        h_ref[sl, :] = num / jnp.maximum(jnp.abs(den), jnp.exp(-m_e))

        b_last = bc[L - 1:L, :]
        gexp = b_last - bc + lic
        m_new = jnp.maximum(b_last + m_st, jnp.max(gexp, axis=0, keepdims=True))
        wts = jnp.exp(gexp - m_new)
        dec = jnp.exp(b_last + m_st - m_new)
        upd = _dot_tn(kc, jnp.concatenate([wts * vc, wts], axis=1).astype(BF16))
        upd = jnp.where(same_head_k, upd, 0.0)
        c_st = dec * c_st + upd[:, :W]
        n_st = dec * n_st + upd[:, W:]
        m_st = m_new
    c_ref[...] = c_st
    n_ref[...] = n_st
    m_ref[0:1, :] = m_st

    hv = h_ref[...]
    ms = _dot_split(hv * hv, _head_ones(W, DV), 2) * (1.0 / DV)
    o_ref[...] = hv * lax.rsqrt(ms + NORM_EPS) * ng_ref[...] * _sigmoid(og)


def _t5_bucket(dist):
    max_exact = REL_BUCKETS // 2
    d = np.maximum(dist, 1).astype(np.float32)
    large = max_exact + (np.log(d / max_exact) / math.log(REL_MAX_DIST / max_exact)
                         * (REL_BUCKETS - max_exact)).astype(np.int32)
    large = np.minimum(large, REL_BUCKETS - 1)
    return np.where(dist < max_exact, dist, large).astype(np.int32)


def _swa_bias_table(rb_ref, bucket_ref, bias_ref):
    blk = ATTN_BLOCK
    grp = N_HEADS // SWA_KV_HEADS
    bucket = bucket_ref[...]
    for h in range(N_HEADS):
        acc = jnp.full((2 * blk, blk), -jnp.inf, F32)
        for bk in range(REL_BUCKETS):
            acc = jnp.where(bucket == bk, rb_ref[bk, h], acc)
        bias_ref[h // grp, :, (h % grp) * blk:(h % grp + 1) * blk] = acc


def _swa_kernel(sink_ref, q_ref, kp_ref, kc_ref, vp_ref, vc_ref, o_ref, bias_ref):
    blk = ATTN_BLOCK
    grp = N_HEADS // SWA_KV_HEADS
    kvw = SWA_KV_HEADS * HEAD_DIM
    key = lax.broadcasted_iota(jnp.int32, (2 * blk, grp * blk), 0)
    live = (key >= blk) | (pl.program_id(1) > 0)
    member = lax.broadcasted_iota(jnp.int32, (1, grp * blk), 1) // blk
    kw = jnp.concatenate([kp_ref[...], kc_ref[...]], axis=0).astype(BF16)
    vw = jnp.concatenate([vp_ref[...], vc_ref[...]], axis=0).astype(BF16)
    lane_member = lax.broadcasted_iota(jnp.int32, (1, grp * HEAD_DIM), 1) // HEAD_DIM
    rr = lax.broadcasted_iota(jnp.int32, (kvw, grp * HEAD_DIM), 0)
    cc = lax.broadcasted_iota(jnp.int32, (kvw, grp * HEAD_DIM), 1)
    vr = lax.broadcasted_iota(jnp.int32, (HEAD_DIM, kvw), 0)
    vc_ = lax.broadcasted_iota(jnp.int32, (HEAD_DIM, kvw), 1)
    n_sub = q_ref.shape[0] // blk
    k_rep, v_t, sinks = [], [], []
    for j in range(SWA_KV_HEADS):
        spread = ((_idiv(rr, HEAD_DIM) == j) & (_imod(rr, HEAD_DIM) == _imod(cc, HEAD_DIM))).astype(BF16)
        pick = ((_idiv(vc_, HEAD_DIM) == j) & (_imod(vc_, HEAD_DIM) == vr)).astype(BF16)
        k_rep.append(_dot(kw, spread).astype(BF16))
        v_t.append(_dot_nt(pick, vw).astype(BF16))
        sinks.append(jnp.where(member == 0, sink_ref[j * grp], sink_ref[j * grp + 1]))
    pairs = [(n, j) for n in range(n_sub) for j in range(SWA_KV_HEADS)]
    scores = []
    for n, j in pairs:
        qp = q_ref[n * blk:(n + 1) * blk, j * grp * HEAD_DIM:(j + 1) * grp * HEAD_DIM] * (HEAD_DIM ** -0.5)
        q_stack = jnp.concatenate([jnp.where(lane_member == g, qp, 0.0) for g in range(grp)],
                                  axis=0).astype(BF16)
        s = _dot_nt(k_rep[j][n * blk:(n + 2) * blk], q_stack) + bias_ref[j]
        scores.append(jnp.where(live, s, -jnp.inf) if n == 0 else s)
    probs = []
    for (n, j), s in zip(pairs, scores):
        m = jnp.maximum(jnp.max(s, axis=0, keepdims=True), sinks[j])
        p = jnp.exp(s - m)
        denom = jnp.sum(p, axis=0, keepdims=True) + jnp.exp(sinks[j] - m)
        probs.append((p / denom).astype(BF16))
    outs = [_dot(v_t[j][:, n * blk:(n + 2) * blk], p) for (n, j), p in zip(pairs, probs)]
    for n in range(n_sub):
        heads = [outs[n * SWA_KV_HEADS + j][:, g * blk:(g + 1) * blk]
                 for j in range(SWA_KV_HEADS) for g in range(grp)]
        o_ref[n * blk:(n + 1) * blk, :] = jnp.concatenate(heads, axis=0).T


N_RWKV_PARAMS = 11
N_MLSTM_PARAMS = 4
N_RWKV_SCRATCH = 3
N_MLSTM_SCRATCH = 5


def _local_mixers_kernel(*refs, tc):
    it = iter(refs)
    take = lambda n: [next(it) for _ in range(n)]
    (za_ref,), rwkv_p = take(1), take(N_RWKV_PARAMS)
    (zb_ref,), mlstm_p = take(1), take(N_MLSTM_PARAMS)
    rb_ref, sink_ref, bucket_ref = take(3)
    swa_in = take(5)
    ya_ref, yb_ref, yc_ref = take(3)
    rwkv_s, mlstm_s = take(N_RWKV_SCRATCH), take(N_MLSTM_SCRATCH)
    (bias_ref,) = take(1)

    @pl.when((pl.program_id(0) == 0) & (pl.program_id(1) == 0))
    def _():
        _swa_bias_table(rb_ref, bucket_ref, bias_ref)

    @pl.when(pl.program_id(1) == 0)
    def _():
        for ref in rwkv_s[:2] + mlstm_s[:4]:
            ref[...] = jnp.zeros_like(ref)

    _rwkv_kernel(za_ref, *rwkv_p, ya_ref, *rwkv_s, tc=tc)
    _mlstm_kernel(zb_ref, *mlstm_p, yb_ref, *mlstm_s, tc=tc)
    _swa_kernel(sink_ref, *swa_in, yc_ref, bias_ref)


def _local_mixers(za, zb, zc, pr, pm, rel_bias, sinks, bsz, t, tc):
    blk = ATTN_BLOCK
    assert tc % blk == 0 and t % tc == 0 and N_HEADS // SWA_KV_HEADS == 2
    nt = t // tc
    full = lambda a: pl.BlockSpec(a.shape, lambda b, i: (0,) * a.ndim)
    tile = lambda w, c=0: pl.BlockSpec((tc, w), lambda b, i: (b * nt + i, c))
    rwkv_p = [pr["mu"], pr["w0"], pr["w_up"], pr["a0"], pr["a_up"], pr["g_up"], pr["k_k"], pr["k_a"], pr["r_k"],
              pr["ln_w"], pr["ln_b"]]
    mlstm_p = [pm["conv_w"], pm["conv_b"], pm["gate_b"], pm["norm_g"]]
    assert len(rwkv_p) == N_RWKV_PARAMS and len(mlstm_p) == N_MLSTM_PARAMS
    dist = np.arange(blk)[None, :] + blk - np.arange(2 * blk)[:, None]
    bucket = jnp.asarray(np.where((dist >= 0) & (dist < SWA_WINDOW),
                                  _t5_bucket(np.clip(dist, 0, SWA_WINDOW - 1)), -1).astype(np.int32))
    kvw = SWA_KV_HEADS * HEAD_DIM
    kcol = MIX_W // kvw
    sub = tc // blk
    before = lambda c: pl.BlockSpec((blk, kvw), lambda b, i: (b * nt * sub + jnp.maximum(i * sub - 1, 0), c))
    smem = pl.BlockSpec(memory_space=pltpu.SMEM)
    return pl.pallas_call(
        functools.partial(_local_mixers_kernel, tc=tc),
        grid=(bsz, nt),
        in_specs=[tile(ZA_W)] + [full(a) for a in rwkv_p] + [tile(ZB_W)] + [full(a) for a in mlstm_p]
        + [smem, smem, full(bucket), tile(MIX_W), before(kcol), tile(kvw, kcol), before(kcol + 1), tile(kvw, kcol + 1)],
        out_specs=[tile(MIX_W)] * 3,
        out_shape=[jax.ShapeDtypeStruct((bsz * t, MIX_W), F32)] * 3,
        scratch_shapes=[
            pltpu.VMEM((SUBLANES, ZA_W), F32),
            pltpu.VMEM((MIX_W, MIX_W), F32),
            pltpu.VMEM((tc, MIX_W), F32),
            pltpu.VMEM((SUBLANES, MIX_W), F32),
            pltpu.VMEM((N_HEADS * MLSTM_DK, MIX_W), F32),
            pltpu.VMEM((N_HEADS * MLSTM_DK, MIX_W), F32),
            pltpu.VMEM((SUBLANES, MIX_W), F32),
            pltpu.VMEM((tc, MIX_W), F32),
            pltpu.VMEM((SWA_KV_HEADS, 2 * blk, 2 * blk), F32),
        ],
        compiler_params=pltpu.CompilerParams(dimension_semantics=("arbitrary", "arbitrary")),
        name="local_mixers",
    )(za, *rwkv_p, zb, *mlstm_p, rel_bias, sinks, bucket, zc, zc, zc, zc, zc)


FOX_BLOCK = 256
FOX_AUG = LANES
FOX_FEAT = 80
FOX_PARTS = 3


def _fox_placements():
    wide = N_HEADS * FOX_AUG
    pk = np.zeros((MIX_W, wide), np.float32)
    pck = np.zeros((LANES, wide), np.float32)
    ones_k = np.zeros((SUBLANES, wide), np.float32)
    tall = N_HEADS * FOX_FEAT
    pq_t = np.zeros((tall, MIX_W), np.float32)
    pv_t = np.zeros((tall, MIX_W), np.float32)
    pcq_t = np.zeros((tall, LANES), np.float32)
    ones_t = np.zeros((2, tall, LANES), np.float32)
    assert HEAD_DIM + 2 * FOX_PARTS <= FOX_FEAT <= FOX_AUG
    for h in range(N_HEADS):
        base = h * FOX_AUG
        base_t = h * FOX_FEAT
        for d in range(HEAD_DIM):
            pk[h * HEAD_DIM + d, base + d] = 1.0
            pq_t[base_t + d, h * HEAD_DIM + d] = HEAD_DIM ** -0.5
            pv_t[base_t + d, h * HEAD_DIM + d] = 1.0
        for n in range(FOX_PARTS):
            pcq_t[base_t + HEAD_DIM + n, n * N_HEADS + h] = 1.0
            pck[n * N_HEADS + h, base + HEAD_DIM + FOX_PARTS + n] = -1.0
            ones_t[0, base_t + HEAD_DIM + FOX_PARTS + n, :] = 1.0
            ones_k[0, base + HEAD_DIM + n] = 1.0
        ones_t[1, base_t + HEAD_DIM, :] = 1.0
    bf = lambda a: jnp.asarray(a, BF16)
    return bf(pk), bf(pck), jnp.asarray(ones_k), bf(np.stack([pq_t, pv_t])), bf(pcq_t), jnp.asarray(ones_t)


def _fox_kernel(q_ref, k_ref, v_ref, f_ref, fb_ref, pk_ref, pck_ref, onesk_ref, pqv_ref, pcq_ref, onest_ref,
                o_ref, kaug_ref, vaug_ref, m_ref, acc_ref, clast_ref, s_ref, s2_ref, p_ref):
    blk = FOX_BLOCK
    i = pl.program_id(1)

    @pl.when(i == 0)
    def _():
        clast_ref[...] = jnp.zeros_like(clast_ref)

    ls = _log_sigmoid(f_ref[...] + fb_ref[...])
    row = lax.broadcasted_iota(jnp.int32, (blk, blk), 0)
    col = lax.broadcasted_iota(jnp.int32, (blk, blk), 1)
    cq = _split_dot((col <= row).astype(BF16), ls, FOX_PARTS) + clast_ref[0:1, :]
    clast_ref[0:1, :] = cq[blk - 1:blk, :]

    widen = lambda a: jnp.concatenate([a] * (blk // LANES), axis=1)
    qa = _dot_nt(pqv_ref[0], q_ref[...].astype(BF16)) + widen(onest_ref[0])
    va = _dot_nt(pqv_ref[1], v_ref[...].astype(BF16)) + widen(onest_ref[1])
    ka = _dot(k_ref[...].astype(BF16), pk_ref[...]) + onesk_ref[0:1, :]
    lane = lax.broadcasted_iota(jnp.int32, (1, LANES), 1)
    pieces = _split_bf16(jnp.where(lane < N_HEADS, cq, 0.0), FOX_PARTS)
    packed = pieces[0].astype(F32)
    for n in range(1, FOX_PARTS):
        packed = packed + pltpu.roll(pieces[n].astype(F32), n * N_HEADS, axis=1)
    packed = packed.astype(BF16)
    qa = (qa + _dot_nt(pcq_ref[...], packed)).astype(BF16)
    ka = ka + _dot(packed, pck_ref[...])
    row0 = pl.multiple_of(i * blk, blk)
    for h in range(N_HEADS):
        kaug_ref[h, pl.ds(row0, blk), :] = ka[:, h * FOX_AUG:(h + 1) * FOX_AUG].astype(BF16)
        vaug_ref[h, :, pl.ds(row0, blk)] = va[h * FOX_FEAT:(h + 1) * FOX_FEAT, :].astype(BF16)

    key_le_query = row <= col

    def scores(j, buf, diagonal=False):
        off = pl.multiple_of(j * blk, blk)
        for h in range(N_HEADS):
            s = _dot(kaug_ref[h, pl.ds(off, blk), 0:FOX_FEAT], qa[h * FOX_FEAT:(h + 1) * FOX_FEAT, :])
            buf[h] = jnp.where(key_le_query, s, -jnp.inf) if diagonal else s

    def consume(j, buf):
        off = pl.multiple_of(j * blk, blk)
        m_new = [jnp.maximum(m_ref[h], jnp.max(buf[h], axis=0, keepdims=True)) for h in range(N_HEADS)]
        for h in range(N_HEADS):
            p_ref[h] = jnp.exp(buf[h] - m_new[h]).astype(BF16)
        for h in range(N_HEADS):
            pv = _dot(vaug_ref[h, :, pl.ds(off, blk)], p_ref[h])
            acc_ref[h] = jnp.exp(m_ref[h] - m_new[h]) * acc_ref[h] + pv
            m_ref[h] = m_new[h]

    m_ref[...] = jnp.full_like(m_ref, -jnp.inf)
    acc_ref[...] = jnp.zeros_like(acc_ref)
    scores(i, s_ref, diagonal=True)

    def body(j, carry):
        scores(2 * j, s2_ref)
        consume(jnp.where(j == 0, i, 2 * j - 1), s_ref)
        scores(jnp.minimum(2 * j + 1, i - 1), s_ref)
        consume(2 * j, s2_ref)
        return carry

    lax.fori_loop(0, (i + 1) // 2, body, 0)

    @pl.when(i % 2 == 0)
    def _():
        consume(jnp.maximum(i - 1, 0), s_ref)

    outs = []
    for h in range(N_HEADS):
        acc = acc_ref[h]
        outs.append(acc[0:HEAD_DIM, :] / acc[HEAD_DIM:HEAD_DIM + 1, :])
    o_ref[...] = jnp.concatenate(outs, axis=0).T


def _fox(zd, fb, bsz, t):
    blk = FOX_BLOCK
    assert t % blk == 0
    nb = t // blk
    consts = _fox_placements()
    const = lambda a: pl.BlockSpec(a.shape, lambda b, i: (0,) * a.ndim)
    cur = lambda c: (lambda b, i: (b * nb + i, c))
    return pl.pallas_call(
        _fox_kernel,
        grid=(bsz, nb),
        in_specs=[
            pl.BlockSpec((blk, MIX_W), cur(0)),
            pl.BlockSpec((blk, MIX_W), cur(1)),
            pl.BlockSpec((blk, MIX_W), cur(2)),
            pl.BlockSpec((blk, LANES), cur(3 * MIX_W // LANES)),
            const(fb)] + [const(a) for a in consts],
        out_specs=pl.BlockSpec((blk, MIX_W), cur(0)),
        out_shape=jax.ShapeDtypeStruct((bsz * t, MIX_W), F32),
        scratch_shapes=[
            pltpu.VMEM((N_HEADS, t, FOX_AUG), BF16),
            pltpu.VMEM((N_HEADS, FOX_FEAT, t), BF16),
            pltpu.VMEM((N_HEADS, 1, blk), F32),
            pltpu.VMEM((N_HEADS, FOX_FEAT, blk), F32),
            pltpu.VMEM((SUBLANES, LANES), F32),
            pltpu.VMEM((N_HEADS, blk, blk), F32),
            pltpu.VMEM((N_HEADS, blk, blk), F32),
            pltpu.VMEM((N_HEADS, blk, blk), BF16),
        ],
        compiler_params=pltpu.CompilerParams(dimension_semantics=("arbitrary", "arbitrary")),
        name="fox",
    )(zd, zd, zd, zd, fb, *consts)


FFN_COL_CHUNK = 6 * MXU_COLS


def _outproj_ffn_kernel(x_ref, ya_ref, yb_ref, yc_ref, yd_ref, wo_ref, gmix_ref, gpre_ref, wg_ref, wu_ref, cw_ref,
                        cb_ref, wd_ref, gpost_ref, o_ref, tail_ref, *, tm, t):
    mixed = None
    for n, ref in enumerate((ya_ref, yb_ref, yc_ref, yd_ref)):
        part = _dot(ref[...].astype(BF16), wo_ref[n * MIX_W:(n + 1) * MIX_W, :])
        mixed = part if mixed is None else mixed + part
    x = x_ref[...] + mixed * _rms_scale(mixed) * gmix_ref[...]
    h = (x * _rms_scale(x) * gpre_ref[...]).astype(BF16)
    seq_start = (pl.program_id(0) * tm) % t == 0

    @pl.when(pl.program_id(0) == 0)
    def _():
        tail_ref[...] = jnp.zeros_like(tail_ref)

    acc = None
    for c0 in range(0, D_FF, FFN_COL_CHUNK):
        cs = slice(c0, min(c0 + FFN_COL_CHUNK, D_FF))
        gate = _dot(h, wg_ref[:, cs])
        halo = jnp.where(seq_start, 0.0, tail_ref[:, cs])
        tail_ref[:, cs] = gate[tm - SUBLANES:, :]
        conv = cb_ref[:, cs] + cw_ref[FFN_CONV - 1:FFN_CONV, cs] * gate
        for d in range(1, FFN_CONV):
            conv = conv + cw_ref[FFN_CONV - 1 - d:FFN_CONV - d, cs] * _shift_rows(gate, halo, d)
        f = jax.nn.gelu(conv, approximate=True) * _dot(h, wu_ref[:, cs])
        part = _dot(f.astype(BF16), wd_ref[cs, :])
        acc = part if acc is None else acc + part
    o_ref[...] = x + acc * _rms_scale(acc) * gpost_ref[...]


def _outproj_ffn(x2, ys, w_out, g_mix, p, t, tm):
    rows = x2.shape[0]
    const = lambda a: pl.BlockSpec(a.shape, lambda i: (0, 0), pipeline_mode=pl.Buffered(1))
    params = [w_out, g_mix, p["g_pre"], p["w_gate"], p["w_up"], p["conv_w"], p["conv_b"], p["w_down"], p["g_post"]]
    return pl.pallas_call(
        functools.partial(_outproj_ffn_kernel, tm=tm, t=t),
        grid=(rows // tm,),
        in_specs=[pl.BlockSpec((tm, D_MODEL), lambda i: (i, 0))]
        + [pl.BlockSpec((tm, MIX_W), lambda i: (i, 0))] * len(ys)
        + [const(a) for a in params],
        out_specs=pl.BlockSpec((tm, D_MODEL), lambda i: (i, 0)),
        out_shape=jax.ShapeDtypeStruct((rows, D_MODEL), F32),
        scratch_shapes=[pltpu.VMEM((SUBLANES, D_FF), F32)],
        compiler_params=pltpu.CompilerParams(dimension_semantics=("arbitrary",),
                                             vmem_limit_bytes=56 * 1024 * 1024),
        name="outproj_ffn",
    )(x2, *ys, *params)


def _pad_cols(a, width):
    return jnp.pad(a, ((0, 0), (0, width - a.shape[1])))


def _row(a):
    return a.reshape(1, -1).astype(F32)


def _layer_params(l, w_in, w_out, norm_mix_pre, norm_mix_post, norm_ffn_pre, norm_ffn_post,
                  rwkv_mu, rwkv_w0, rwkv_w_up, rwkv_a0, rwkv_a_up, rwkv_g_up, rwkv_k_k, rwkv_k_a,
                  rwkv_r_k, rwkv_ln_w, rwkv_ln_b, mlstm_conv_w, mlstm_conv_b, mlstm_b_i, mlstm_b_f,
                  mlstm_norm, swa_sinks, fox_b_f, ffn_w_up, ffn_conv_w, ffn_conv_b, ffn_w_down):
    wa_w = ZA_W
    wb_w = 3 * MIX_W + 2 * N_HEADS
    wc_w = ZC_W
    wi = jnp.transpose(w_in, (2, 0, 1))[:, l, :].astype(BF16)
    pad_rows = lambda a, n: jnp.pad(a, ((0, n - a.shape[0]), (0, 0)))
    groups = [wi[:wa_w],
              pad_rows(wi[wa_w:wa_w + wb_w], ZB_W),
              wi[wa_w + wb_w:wa_w + wb_w + wc_w],
              pad_rows(wi[wa_w + wb_w + wc_w:], ZD_W)]
    gate_b = _pad_cols(jnp.concatenate([mlstm_b_i[l], mlstm_b_f[l]]).reshape(1, -1), LANES)
    return {
        "w_in": jnp.concatenate(groups, axis=0),
        "g_mix_pre": _row(norm_mix_pre[l]),
        "g_mix_post": _row(norm_mix_post[l]),
        "w_out": w_out[l].astype(BF16),
        "rwkv": {"mu": _row(rwkv_mu[l]), "w0": _row(rwkv_w0[l]), "w_up": rwkv_w_up[l].astype(BF16),
                 "a0": _row(rwkv_a0[l]), "a_up": rwkv_a_up[l].astype(BF16), "g_up": rwkv_g_up[l].astype(BF16),
                 "k_k": _row(rwkv_k_k[l]), "k_a": _row(rwkv_k_a[l]), "r_k": _row(rwkv_r_k[l]),
                 "ln_w": _row(rwkv_ln_w[l]), "ln_b": _row(rwkv_ln_b[l])},
        "mlstm": {"conv_w": mlstm_conv_w[l], "conv_b": _row(mlstm_conv_b[l]), "gate_b": gate_b,
                  "norm_g": _row(mlstm_norm[l])},
        "swa_sinks": swa_sinks[l],
        "fox_b": _pad_cols(fox_b_f[l].reshape(1, -1), LANES),
        "ffn": {"g_pre": _row(norm_ffn_pre[l]), "w_gate": ffn_w_up[l][:, :D_FF].astype(BF16),
                "w_up": ffn_w_up[l][:, D_FF:].astype(BF16), "conv_w": ffn_conv_w[l],
                "conv_b": _row(ffn_conv_b[l]), "w_down": ffn_w_down[l].astype(BF16),
                "g_post": _row(norm_ffn_post[l])},
    }


def _tiles(t):
    return min(512, t), min(512, t), min(512, t)


def kernel(x, w_in, w_out, norm_mix_pre, norm_mix_post, norm_ffn_pre, norm_ffn_post, rwkv_mu, rwkv_w0, rwkv_w_up, rwkv_a0, rwkv_a_up, rwkv_g_up, rwkv_k_k, rwkv_k_a, rwkv_r_k, rwkv_ln_w, rwkv_ln_b, mlstm_conv_w, mlstm_conv_b, mlstm_b_i, mlstm_b_f, mlstm_norm, swa_sinks, fox_b_f, rel_bias, ffn_w_up, ffn_conv_w, ffn_conv_b, ffn_w_down):
    bsz, t, d = x.shape
    assert d == D_MODEL and t % ATTN_BLOCK == 0
    tm, tc, tf = _tiles(t)
    x2 = x.reshape(bsz * t, d)
    for l in range(w_in.shape[0]):
        p = _layer_params(l, w_in, w_out, norm_mix_pre, norm_mix_post, norm_ffn_pre, norm_ffn_post,
                          rwkv_mu, rwkv_w0, rwkv_w_up, rwkv_a0, rwkv_a_up, rwkv_g_up, rwkv_k_k, rwkv_k_a,
                          rwkv_r_k, rwkv_ln_w, rwkv_ln_b, mlstm_conv_w, mlstm_conv_b, mlstm_b_i, mlstm_b_f,
                          mlstm_norm, swa_sinks, fox_b_f, ffn_w_up, ffn_conv_w, ffn_conv_b, ffn_w_down)
        za, zb, zc, zd = _inproj(x2, p["g_mix_pre"], p["w_in"], tm)
        ya, yb, yc = _local_mixers(za, zb, zc, p["rwkv"], p["mlstm"], rel_bias, p["swa_sinks"], bsz, t, tc)
        yd = _fox(zd, p["fox_b"], bsz, t)
        x2 = _outproj_ffn(x2, (ya, yb, yc, yd), p["w_out"], p["g_mix_post"], p["ffn"], t, tf)
    return x2.reshape(bsz, t, d)
```

```python
import functools
import math

import jax
import jax.numpy as jnp
import numpy as np
from jax import lax
from jax.experimental import pallas as pl
from jax.experimental.pallas import tpu as pltpu

F32 = jnp.float32
BF16 = jnp.bfloat16

D_MODEL = 1024
HEAD_DIM = 64
N_HEADS = 4
MIX_W = N_HEADS * HEAD_DIM
RWKV_DECAY_RANK = 64
RWKV_AAA_RANK = 64
RWKV_GATE_RANK = 128
RWKV_LN_EPS = 64e-5
RWKV_CHUNK = 64
RWKV_CHUNK_GROUPS = 1
RWKV_STAGES_PER_STEP = 2
MLSTM_DK = 32
MLSTM_CONV = 4
MLSTM_CHUNK = 64
GATE_SOFTCAP = 15.0
SWA_KV_HEADS = 2
SWA_WINDOW = 128
ATTN_BLOCK = 128
REL_BUCKETS = 32
REL_MAX_DIST = 128
D_FF = 2816
FFN_CONV = 3
NORM_EPS = 1e-6

LANES = 128
SUBLANES = 8
MXU_COLS = 256

ZA_W = 3 * MIX_W + RWKV_DECAY_RANK + RWKV_AAA_RANK + RWKV_GATE_RANK
ZB_W = 3 * MIX_W + LANES
ZC_W = MIX_W + 2 * SWA_KV_HEADS * HEAD_DIM
ZD_W = 3 * MIX_W + LANES
Z_W = ZA_W + ZB_W + ZC_W + ZD_W

def _split_bf16(x, parts):
    out = []
    for n in range(parts):
        piece = x.astype(BF16)
        out.append(piece)
        if n + 1 < parts:
            x = x - piece.astype(F32)
    return out


def _dot(a, b, precision=None):
    return jnp.dot(a, b, preferred_element_type=F32, precision=precision)


def _dot_nt(a, b, precision=None):
    return lax.dot_general(a, b, (((1,), (1,)), ((), ())), preferred_element_type=F32, precision=precision)


def _dot_tn(a, b, precision=None):
    return lax.dot_general(a, b, (((0,), (0,)), ((), ())), preferred_element_type=F32, precision=precision)


def _dot_split(x, ones, parts):
    return sum(_dot(piece, ones) for piece in _split_bf16(x, parts))


def _split_dot(ones, x, parts):
    return sum(_dot(ones, piece) for piece in _split_bf16(x, parts))


def _bdot(a, b):
    return _dot(a.astype(BF16), b.astype(BF16))


def _bdot_nt(a, b):
    return _dot_nt(a.astype(BF16), b.astype(BF16))


def _bdot_tn(a, b):
    return _dot_tn(a.astype(BF16), b.astype(BF16))


def _sigmoid(x):
    return 1.0 / (1.0 + jnp.exp(-x))


def _log_sigmoid(x):
    return jnp.minimum(x, 0.0) - jnp.log(1.0 + jnp.exp(-jnp.abs(x)))


def _softplus(x):
    return jnp.maximum(x, 0.0) + jnp.log(1.0 + jnp.exp(-jnp.abs(x)))


def _rms_scale(x):
    return lax.rsqrt(jnp.mean(x * x, axis=-1, keepdims=True) + NORM_EPS)


def _shift_rows(x, prev_tail, d):
    rolled = pltpu.roll(x, d, axis=0)
    head_rows = lax.broadcasted_iota(jnp.int32, (SUBLANES, 1), 0)
    head = jnp.where(head_rows < d, pltpu.roll(prev_tail, d, axis=0), rolled[:SUBLANES])
    return jnp.concatenate([head, rolled[SUBLANES:]], axis=0)


def _idiv(x, n):
    assert n & (n - 1) == 0
    return lax.shift_right_logical(x, jnp.int32(n.bit_length() - 1))


def _imod(x, n):
    assert n & (n - 1) == 0
    return lax.bitwise_and(x, jnp.int32(n - 1))


def _head_ones(width, head):
    r = _idiv(lax.broadcasted_iota(jnp.int32, (width, width), 0), head)
    c = _idiv(lax.broadcasted_iota(jnp.int32, (width, width), 1), head)
    return (r == c).astype(BF16)


def _inproj_kernel(x_ref, g_ref, w_ref, za_ref, zb_ref, zc_ref, zd_ref):
    x = x_ref[...]
    h = (x * _rms_scale(x) * g_ref[...]).astype(BF16)
    z = _dot_nt(h, w_ref[...])
    off = 0
    for ref in (za_ref, zb_ref, zc_ref, zd_ref):
        w = ref.shape[1]
        ref[...] = z[:, off:off + w]
        off += w


def _inproj(x2, g, w_all, layer, tm):
    rows = x2.shape[0]
    return pl.pallas_call(
        _inproj_kernel,
        grid=(rows // tm,),
        in_specs=[
            pl.BlockSpec((tm, D_MODEL), lambda i: (i, 0)),
            pl.BlockSpec((1, D_MODEL), lambda i: (0, 0)),
            pl.BlockSpec((Z_W, D_MODEL), lambda i: (0, layer)),
        ],
        out_specs=[pl.BlockSpec((tm, w_), lambda i: (i, 0)) for w_ in (ZA_W, ZB_W, ZC_W, ZD_W)],
        out_shape=[jax.ShapeDtypeStruct((rows, w_), F32) for w_ in (ZA_W, ZB_W, ZC_W, ZD_W)],
        compiler_params=pltpu.CompilerParams(dimension_semantics=("arbitrary",)),
        name="inproj",
    )(x2, g, w_all)


def _rwkv_kernel(z_ref, mu_ref, w0_ref, wup_ref, a0_ref, aup_ref, gup_ref, kk_ref, ka_ref, rk_ref,
                 lnw_ref, lnb_ref, o_ref, tail_ref, st_ref, y_ref, *, tc):
    L = RWKV_CHUNK
    W = MIX_W

    z = z_ref[...]
    zz = z + mu_ref[...] * (_shift_rows(z, tail_ref[...], 1) - z)
    tail_ref[...] = z[tc - SUBLANES:, :]

    r = zz[:, 0:W]
    k = zz[:, W:2 * W]
    v = zz[:, 2 * W:3 * W]
    o1 = 3 * W
    wd = zz[:, o1:o1 + RWKV_DECAY_RANK]
    ad = zz[:, o1 + RWKV_DECAY_RANK:o1 + RWKV_DECAY_RANK + RWKV_AAA_RANK]
    gd = zz[:, o1 + RWKV_DECAY_RANK + RWKV_AAA_RANK:]

    lw = -jnp.exp(-_softplus(-(w0_ref[...] + _bdot(jnp.tanh(wd), wup_ref[...]))) - 0.5)
    alpha = _sigmoid(a0_ref[...] + _bdot(ad, aup_ref[...]))
    g = _bdot(_sigmoid(gd), gup_ref[...])

    hsum = _head_ones(W, HEAD_DIM)
    kk = k * kk_ref[...]
    kk = kk * lax.rsqrt(jnp.maximum(_dot_split(kk * kk, hsum, 2), 1e-24))
    k = k * (1.0 + (alpha - 1.0) * ka_ref[...])

    span = min(tc, MXU_COLS)
    rt = lax.broadcasted_iota(jnp.int32, (span, span), 0)
    ct = lax.broadcasted_iota(jnp.int32, (span, span), 1)
    tri = ((_idiv(rt, L) == _idiv(ct, L)) & (ct <= rt)).astype(BF16)
    cum = jnp.concatenate([_split_dot(tri, lw[r0:r0 + span], 3) for r0 in range(0, tc, span)], axis=0)
    e_in = jnp.exp(cum)
    e_out = jnp.exp(-cum)
    r_t = r * e_in
    a_t = -kk * jnp.exp(cum - lw)
    b_t = kk * alpha * e_out
    k_t = k * e_out

    lane_head = lax.broadcasted_iota(jnp.int32, (1, W), 1) // HEAD_DIM

    def stack(xc):
        return jnp.concatenate([jnp.where(lane_head == h, xc, 0.0) for h in range(N_HEADS)], axis=0)

    n = N_HEADS * L
    rr = lax.broadcasted_iota(jnp.int32, (n, n), 0)
    cc = lax.broadcasted_iota(jnp.int32, (n, n), 1)
    own_head = _idiv(rr, L) == _idiv(cc, L)
    strict = own_head & (_imod(rr, L) > _imod(cc, L))
    incl = own_head & (_imod(rr, L) >= _imod(cc, L))
    eye = (rr == cc).astype(F32)

    over_heads = lambda x: jnp.concatenate([x, x], axis=1)

    def state_free(chunks):
        stk = [tuple(stack(u[sl]).astype(BF16) for u in (a_t, r_t, b_t, k_t, v)) for sl in chunks]
        bk_rep = [jnp.concatenate([b_t[sl]] * 2 + [k_t[sl]] * 2, axis=0).astype(BF16) for sl in chunks]
        ar = [_dot_nt(jnp.concatenate([s_[0], s_[1]], axis=0), x) for s_, x in zip(stk, bk_rep)]
        a_ab = [jnp.where(strict, over_heads(x[:n, :2 * L]), 0.0) for x in ar]
        a_ak = [jnp.where(strict, over_heads(x[:n, 2 * L:]), 0.0).astype(BF16) for x in ar]
        a_rb = [jnp.where(incl, over_heads(x[n:, :2 * L]), 0.0).astype(BF16) for x in ar]
        a_rk = [jnp.where(incl, over_heads(x[n:, 2 * L:]), 0.0).astype(BF16) for x in ar]
        yield
        inv = [eye + m for m in a_ab]
        pw = [m.astype(BF16) for m in a_ab]
        for _ in range(int(math.log2(L)) - 1):
            pw = [_dot(m, m).astype(BF16) for m in pw]
            inv = [t_ + _dot(t_.astype(BF16), m) for t_, m in zip(inv, pw)]
            yield
        inv = [t_.astype(BF16) for t_ in inv]
        akv = [_dot(m, s_[4]).astype(BF16) for m, s_ in zip(a_ak, stk)]
        wu = [_dot(t_, jnp.concatenate([s_[0], x], axis=1)) for t_, s_, x in zip(inv, stk, akv)]
        w_m = [x[:, :W].astype(BF16) for x in wu]
        u2 = [x[:, W:].astype(BF16) for x in wu]
        yield
        r_eff = [(s_[1].astype(F32) + _dot(m, w_)).astype(BF16) for s_, m, w_ in zip(stk, a_rb, w_m)]
        y_own = [_dot(m, u_) + _dot(n_, s_[4]) for m, u_, n_, s_ in zip(a_rb, u2, a_rk, stk)]
        yield
        st_mix = [_dot_tn(w_, s_[2]).astype(BF16) for w_, s_ in zip(w_m, stk)]
        st_own = [_dot_tn(u_, s_[2]) + _dot_tn(s_[4], s_[3]) for u_, s_ in zip(u2, stk)]
        return r_eff, y_own, st_mix, st_own

    def recurrence(st, sl, r_eff, y_own, st_mix, st_own):
        st_b = st.astype(BF16)
        y_s = _dot_nt(r_eff, st_b) + y_own
        y_c = y_s[0:L]
        for h in range(1, N_HEADS):
            y_c = y_c + y_s[h * L:(h + 1) * L]
        y_ref[sl, :] = y_c
        return (st + _dot(st_b, st_mix) + st_own) * e_in[sl.stop - 1:sl.stop, :]

    yield
    all_chunks = [slice(c * L, (c + 1) * L) for c in range(tc // L)]
    group = max(1, len(all_chunks) // RWKV_CHUNK_GROUPS)
    groups = [all_chunks[g0:g0 + group] for g0 in range(0, len(all_chunks), group)]
    st = st_ref[...]
    ready = yield from state_free(groups[0])
    for gi, chunks in enumerate(groups):
        ahead = state_free(groups[gi + 1]) if gi + 1 < len(groups) else iter(())
        nxt = None
        for c, sl in enumerate(chunks):
            st = recurrence(st, sl, *(part[c] for part in ready))
            for _ in range(RWKV_STAGES_PER_STEP):
                try:
                    next(ahead)
                except StopIteration as stop:
                    nxt = stop.value if stop.value is not None else nxt
            yield
        while gi + 1 < len(groups) and nxt is None:
            try:
                next(ahead)
                yield
            except StopIteration as stop:
                nxt = stop.value
        ready = nxt
    st_ref[...] = st

    y = y_ref[...]
    inv_n = 1.0 / HEAD_DIM
    mean = _dot_split(y, hsum, 2) * inv_n
    yc = y - mean
    var = _dot_split(yc * yc, hsum, 2) * inv_n
    y = yc * lax.rsqrt(var + RWKV_LN_EPS) * lnw_ref[...] + lnb_ref[...]
    bonus = _dot_split(r * k * rk_ref[...], hsum, 2) * v
    o_ref[...] = (y + bonus) * g


def _mlstm_kernel(z_ref, cw_ref, cb_ref, gb_ref, ng_ref, o_ref, tail_ref, c_ref, n_ref, m_ref, h_ref, *, tc):
    L = MLSTM_CHUNK
    W = MIX_W
    DK = MLSTM_DK
    DV = HEAD_DIM

    qk_in = z_ref[:, 0:W]
    v = z_ref[:, W:2 * W]
    og = z_ref[:, 2 * W:3 * W]
    gates = z_ref[:, 3 * W:3 * W + LANES]

    tail = tail_ref[...]
    conv = cb_ref[...] + cw_ref[MLSTM_CONV - 1:MLSTM_CONV, :] * qk_in
    for d in range(1, MLSTM_CONV):
        conv = conv + cw_ref[MLSTM_CONV - 1 - d:MLSTM_CONV - d, :] * _shift_rows(qk_in, tail, d)
    tail_ref[...] = qk_in[tc - SUBLANES:, :]
    qk = conv * _sigmoid(conv)
    q = qk[:, 0:N_HEADS * DK] * (DK ** -0.5)
    k = qk[:, N_HEADS * DK:]

    capped = GATE_SOFTCAP * jnp.tanh((gates + gb_ref[...]) / GATE_SOFTCAP)
    lf = _log_sigmoid(capped)

    gate_col = lax.broadcasted_iota(jnp.int32, (LANES, W), 0)
    lane_head = _idiv(lax.broadcasted_iota(jnp.int32, (LANES, W), 1), DV)
    pick_i = (gate_col == lane_head).astype(BF16)
    pick_f = (gate_col == N_HEADS + lane_head).astype(BF16)
    li_e = _dot_split(capped, pick_i, 3)
    span = min(tc, MXU_COLS)
    rt = lax.broadcasted_iota(jnp.int32, (span, span), 0)
    ct = lax.broadcasted_iota(jnp.int32, (span, span), 1)
    tri = ((_idiv(rt, L) == _idiv(ct, L)) & (ct <= rt)).astype(BF16)
    lf_e = [_dot(piece, pick_f).astype(BF16) for piece in _split_bf16(lf, 3)]
    b_e = jnp.concatenate([sum(_dot(tri, piece[r0:r0 + span]) for piece in lf_e)
                           for r0 in range(0, tc, span)], axis=0)

    key = lax.broadcasted_iota(jnp.int32, (L, W), 0)
    query = _imod(lax.broadcasted_iota(jnp.int32, (L, W), 1), L)
    on_diag = key == query
    causal_t = key <= query
    head_ones = _head_ones(W, DV)
    wide = lambda m: jnp.concatenate([m, m], axis=1)
    wide_lane_head = wide(lax.broadcasted_iota(jnp.int32, (1, W), 1) // DV)
    same_head_k = wide(_idiv(lax.broadcasted_iota(jnp.int32, (N_HEADS * DK, W), 0), DK)
                       == _idiv(lax.broadcasted_iota(jnp.int32, (N_HEADS * DK, W), 1), DV))
    q_lane_head = lax.broadcasted_iota(jnp.int32, (1, N_HEADS * DK), 1) // DK
    ones_b = jnp.ones((L, W), BF16)

    c_st = c_ref[...]
    n_st = n_ref[...]
    m_st = m_ref[0:1, :]
    for c in range(tc // L):
        yield
        sl = slice(c * L, (c + 1) * L)
        bc, lic, vc = b_e[sl], li_e[sl], v[sl]
        qc = q[sl]
        kc = k[sl].astype(BF16)
        q_stack = jnp.concatenate([jnp.where(q_lane_head == h, qc, 0.0) for h in range(N_HEADS)],
                                  axis=0).astype(BF16)
        b_q = jnp.sum(jnp.where(on_diag, bc, 0.0), axis=0, keepdims=True)
        dmat = jnp.where(causal_t, b_q - bc + lic, -jnp.inf)
        m_t = jnp.maximum(b_q + m_st, jnp.max(dmat, axis=0, keepdims=True))
        s_t = (_dot_nt(kc, q_stack) * jnp.exp(dmat - m_t)).astype(BF16)
        m_e = sum(_dot(jnp.where(on_diag, piece.astype(F32), 0.0).astype(BF16), head_ones)
                  for piece in _split_bf16(m_t, 3))
        inter = jnp.exp(bc + m_st - m_e)
        nv = _dot_tn(s_t, jnp.concatenate([vc.astype(BF16), ones_b], axis=1))
        nv = functools.reduce(lambda lo, h: jnp.where(wide_lane_head == h, nv[h * L:(h + 1) * L], lo),
                              range(1, N_HEADS), nv[0:L])
        qcn = _dot(qc.astype(BF16), jnp.concatenate([c_st, n_st], axis=1).astype(BF16))
        num = nv[:, :W] + inter * qcn[:, :W]
        den = nv[:, W:] + inter * qcn[:, W:]
        h_ref[sl, :] = num / jnp.maximum(jnp.abs(den), jnp.exp(-m_e))

        b_last = bc[L - 1:L, :]
        gexp = b_last - bc + lic
        m_new = jnp.maximum(b_last + m_st, jnp.max(gexp, axis=0, keepdims=True))
        wts = jnp.exp(gexp - m_new)
        dec = jnp.exp(b_last + m_st - m_new)
        upd = _dot_tn(kc, jnp.concatenate([wts * vc, wts], axis=1).astype(BF16))
        upd = jnp.where(same_head_k, upd, 0.0)
        c_st = dec * c_st + upd[:, :W]
        n_st = dec * n_st + upd[:, W:]
        m_st = m_new
    c_ref[...] = c_st
    n_ref[...] = n_st
    m_ref[0:1, :] = m_st
    yield

    hv = h_ref[...]
    ms = _dot_split(hv * hv, _head_ones(W, DV), 2) * (1.0 / DV)
    o_ref[...] = hv * lax.rsqrt(ms + NORM_EPS) * ng_ref[...] * _sigmoid(og)


def _t5_bucket(dist):
    max_exact = REL_BUCKETS // 2
    d = np.maximum(dist, 1).astype(np.float32)
    large = max_exact + (np.log(d / max_exact) / math.log(REL_MAX_DIST / max_exact)
                         * (REL_BUCKETS - max_exact)).astype(np.int32)
    large = np.minimum(large, REL_BUCKETS - 1)
    return np.where(dist < max_exact, dist, large).astype(np.int32)


def _swa_bias_table(rb_ref, bucket_ref, bias_ref):
    blk = ATTN_BLOCK
    grp = N_HEADS // SWA_KV_HEADS
    bucket = bucket_ref[...]
    for h in range(N_HEADS):
        acc = jnp.full((2 * blk, blk), -jnp.inf, F32)
        for bk in range(REL_BUCKETS):
            acc = jnp.where(bucket == bk, rb_ref[bk, h], acc)
        bias_ref[h // grp, :, (h % grp) * blk:(h % grp + 1) * blk] = acc


def _swa_kernel(sink_ref, q_ref, kp_ref, kc_ref, vp_ref, vc_ref, o_ref, bias_ref):
    blk = ATTN_BLOCK
    grp = N_HEADS // SWA_KV_HEADS
    kvw = SWA_KV_HEADS * HEAD_DIM
    key = lax.broadcasted_iota(jnp.int32, (2 * blk, grp * blk), 0)
    live = (key >= blk) | (pl.program_id(1) > 0)
    member = lax.broadcasted_iota(jnp.int32, (1, grp * blk), 1) // blk
    kw = jnp.concatenate([kp_ref[...], kc_ref[...]], axis=0).astype(BF16)
    vw = jnp.concatenate([vp_ref[...], vc_ref[...]], axis=0).astype(BF16)
    lane_member = lax.broadcasted_iota(jnp.int32, (1, grp * HEAD_DIM), 1) // HEAD_DIM
    rr = lax.broadcasted_iota(jnp.int32, (kvw, grp * HEAD_DIM), 0)
    cc = lax.broadcasted_iota(jnp.int32, (kvw, grp * HEAD_DIM), 1)
    vr = lax.broadcasted_iota(jnp.int32, (HEAD_DIM, kvw), 0)
    vc_ = lax.broadcasted_iota(jnp.int32, (HEAD_DIM, kvw), 1)
    n_sub = q_ref.shape[0] // blk
    k_rep, v_t, sinks = [], [], []
    for j in range(SWA_KV_HEADS):
        spread = ((_idiv(rr, HEAD_DIM) == j) & (_imod(rr, HEAD_DIM) == _imod(cc, HEAD_DIM))).astype(BF16)
        pick = ((_idiv(vc_, HEAD_DIM) == j) & (_imod(vc_, HEAD_DIM) == vr)).astype(BF16)
        k_rep.append(_dot(kw, spread).astype(BF16))
        v_t.append(_dot_nt(pick, vw).astype(BF16))
        sinks.append(jnp.where(member == 0, sink_ref[j * grp], sink_ref[j * grp + 1]))
    pairs = [(n, j) for n in range(n_sub) for j in range(SWA_KV_HEADS)]
    yield
    scores = []
    for n, j in pairs:
        qp = q_ref[n * blk:(n + 1) * blk, j * grp * HEAD_DIM:(j + 1) * grp * HEAD_DIM] * (HEAD_DIM ** -0.5)
        q_stack = jnp.concatenate([jnp.where(lane_member == g, qp, 0.0) for g in range(grp)],
                                  axis=0).astype(BF16)
        s = _dot_nt(k_rep[j][n * blk:(n + 2) * blk], q_stack) + bias_ref[j]
        scores.append(jnp.where(live, s, -jnp.inf) if n == 0 else s)
    yield
    probs = []
    for (n, j), s in zip(pairs, scores):
        m = jnp.maximum(jnp.max(s, axis=0, keepdims=True), sinks[j])
        p = jnp.exp(s - m)
        denom = jnp.sum(p, axis=0, keepdims=True) + jnp.exp(sinks[j] - m)
        probs.append((p / denom).astype(BF16))
    yield
    outs = [_dot(v_t[j][:, n * blk:(n + 2) * blk], p) for (n, j), p in zip(pairs, probs)]
    for n in range(n_sub):
        heads = [outs[n * SWA_KV_HEADS + j][:, g * blk:(g + 1) * blk]
                 for j in range(SWA_KV_HEADS) for g in range(grp)]
        o_ref[n * blk:(n + 1) * blk, :] = jnp.concatenate(heads, axis=0).T


N_RWKV_PARAMS = 11
N_MLSTM_PARAMS = 4
N_RWKV_SCRATCH = 3
N_MLSTM_SCRATCH = 5


def _trace_interleaved(staged):
    done = object()
    staged = list(staged)
    while staged:
        for item in list(staged):
            body, stride = item
            for _ in range(stride):
                if next(body, done) is done:
                    staged.remove(item)
                    break


def _local_mixers_kernel(*refs, tc):
    it = iter(refs)
    take = lambda n: [next(it) for _ in range(n)]
    (za_ref,), rwkv_p = take(1), take(N_RWKV_PARAMS)
    (zb_ref,), mlstm_p = take(1), take(N_MLSTM_PARAMS)
    rb_ref, sink_ref, bucket_ref = take(3)
    swa_in = take(5)
    ya_ref, yb_ref, yc_ref = take(3)
    rwkv_s, mlstm_s = take(N_RWKV_SCRATCH), take(N_MLSTM_SCRATCH)
    (bias_ref,) = take(1)

    @pl.when((pl.program_id(0) == 0) & (pl.program_id(1) == 0))
    def _():
        _swa_bias_table(rb_ref, bucket_ref, bias_ref)

    @pl.when(pl.program_id(1) == 0)
    def _():
        for ref in rwkv_s[:2] + mlstm_s[:4]:
            ref[...] = jnp.zeros_like(ref)

    _trace_interleaved([
        (_rwkv_kernel(za_ref, *rwkv_p, ya_ref, *rwkv_s, tc=tc), 2),
        (_mlstm_kernel(zb_ref, *mlstm_p, yb_ref, *mlstm_s, tc=tc), 1),
        (_swa_kernel(sink_ref, *swa_in, yc_ref, bias_ref), 1),
    ])


def _local_mixers(za, zb, zc, pr, pm, rel_bias, sinks, bsz, t, tc):
    blk = ATTN_BLOCK
    assert tc % blk == 0 and t % tc == 0 and N_HEADS // SWA_KV_HEADS == 2
    nt = t // tc
    full = lambda a: pl.BlockSpec(a.shape, lambda b, i: (0,) * a.ndim)
    tile = lambda w, c=0: pl.BlockSpec((tc, w), lambda b, i: (b * nt + i, c))
    rwkv_p = [pr["mu"], pr["w0"], pr["w_up"], pr["a0"], pr["a_up"], pr["g_up"], pr["k_k"], pr["k_a"], pr["r_k"],
              pr["ln_w"], pr["ln_b"]]
    mlstm_p = [pm["conv_w"], pm["conv_b"], pm["gate_b"], pm["norm_g"]]
    assert len(rwkv_p) == N_RWKV_PARAMS and len(mlstm_p) == N_MLSTM_PARAMS
    dist = np.arange(blk)[None, :] + blk - np.arange(2 * blk)[:, None]
    bucket = jnp.asarray(np.where((dist >= 0) & (dist < SWA_WINDOW),
                                  _t5_bucket(np.clip(dist, 0, SWA_WINDOW - 1)), -1).astype(np.int32))
    kvw = SWA_KV_HEADS * HEAD_DIM
    kcol = MIX_W // kvw
    sub = tc // blk
    before = lambda c: pl.BlockSpec((blk, kvw), lambda b, i: (b * nt * sub + jnp.maximum(i * sub - 1, 0), c))
    smem = pl.BlockSpec(memory_space=pltpu.SMEM)
    return pl.pallas_call(
        functools.partial(_local_mixers_kernel, tc=tc),
        grid=(bsz, nt),
        in_specs=[tile(ZA_W)] + [full(a) for a in rwkv_p] + [tile(ZB_W)] + [full(a) for a in mlstm_p]
        + [smem, smem, full(bucket), tile(MIX_W), before(kcol), tile(kvw, kcol), before(kcol + 1), tile(kvw, kcol + 1)],
        out_specs=[tile(MIX_W)] * 3,
        out_shape=[jax.ShapeDtypeStruct((bsz * t, MIX_W), F32)] * 3,
        scratch_shapes=[
            pltpu.VMEM((SUBLANES, ZA_W), F32),
            pltpu.VMEM((MIX_W, MIX_W), F32),
            pltpu.VMEM((tc, MIX_W), F32),
            pltpu.VMEM((SUBLANES, MIX_W), F32),
            pltpu.VMEM((N_HEADS * MLSTM_DK, MIX_W), F32),
            pltpu.VMEM((N_HEADS * MLSTM_DK, MIX_W), F32),
            pltpu.VMEM((SUBLANES, MIX_W), F32),
            pltpu.VMEM((tc, MIX_W), F32),
            pltpu.VMEM((SWA_KV_HEADS, 2 * blk, 2 * blk), F32),
        ],
        compiler_params=pltpu.CompilerParams(dimension_semantics=("arbitrary", "arbitrary")),
        name="local_mixers",
    )(za, *rwkv_p, zb, *mlstm_p, rel_bias, sinks, bucket, zc, zc, zc, zc, zc)


FOX_BLOCK = 256
FOX_AUG = LANES
FOX_FEAT = 80
FOX_PARTS = 3


def _fox_placements():
    wide = N_HEADS * FOX_AUG
    pk = np.zeros((MIX_W, wide), np.float32)
    pck = np.zeros((LANES, wide), np.float32)
    ones_k = np.zeros((SUBLANES, wide), np.float32)
    tall = N_HEADS * FOX_FEAT
    pq_t = np.zeros((tall, MIX_W), np.float32)
    pv_t = np.zeros((tall, MIX_W), np.float32)
    pcq_t = np.zeros((tall, LANES), np.float32)
    ones_t = np.zeros((2, tall, LANES), np.float32)
    assert HEAD_DIM + 2 * FOX_PARTS <= FOX_FEAT <= FOX_AUG
    for h in range(N_HEADS):
        base = h * FOX_AUG
        base_t = h * FOX_FEAT
        for d in range(HEAD_DIM):
            pk[h * HEAD_DIM + d, base + d] = 1.0
            pq_t[base_t + d, h * HEAD_DIM + d] = HEAD_DIM ** -0.5
            pv_t[base_t + d, h * HEAD_DIM + d] = 1.0
        for n in range(FOX_PARTS):
            pcq_t[base_t + HEAD_DIM + n, n * N_HEADS + h] = 1.0
            pck[n * N_HEADS + h, base + HEAD_DIM + FOX_PARTS + n] = -1.0
            ones_t[0, base_t + HEAD_DIM + FOX_PARTS + n, :] = 1.0
            ones_k[0, base + HEAD_DIM + n] = 1.0
        ones_t[1, base_t + HEAD_DIM, :] = 1.0
    bf = lambda a: jnp.asarray(a, BF16)
    return bf(pk), bf(pck), jnp.asarray(ones_k), bf(np.stack([pq_t, pv_t])), bf(pcq_t), jnp.asarray(ones_t)


def _fox_kernel(q_ref, k_ref, v_ref, f_ref, fb_ref, pk_ref, pck_ref, onesk_ref, pqv_ref, pcq_ref, onest_ref,
                o_ref, kaug_ref, vaug_ref, m_ref, acc_ref, clast_ref, s_ref, s2_ref, p_ref):
    blk = FOX_BLOCK
    i = pl.program_id(1)

    @pl.when(i == 0)
    def _():
        clast_ref[...] = jnp.zeros_like(clast_ref)

    ls = _log_sigmoid(f_ref[...] + fb_ref[...])
    row = lax.broadcasted_iota(jnp.int32, (blk, blk), 0)
    col = lax.broadcasted_iota(jnp.int32, (blk, blk), 1)
    cq = _split_dot((col <= row).astype(BF16), ls, FOX_PARTS) + clast_ref[0:1, :]
    clast_ref[0:1, :] = cq[blk - 1:blk, :]

    widen = lambda a: jnp.concatenate([a] * (blk // LANES), axis=1)
    qa = _dot_nt(pqv_ref[0], q_ref[...].astype(BF16)) + widen(onest_ref[0])
    va = _dot_nt(pqv_ref[1], v_ref[...].astype(BF16)) + widen(onest_ref[1])
    ka = _dot(k_ref[...].astype(BF16), pk_ref[...]) + onesk_ref[0:1, :]
    lane = lax.broadcasted_iota(jnp.int32, (1, LANES), 1)
    pieces = _split_bf16(jnp.where(lane < N_HEADS, cq, 0.0), FOX_PARTS)
    packed = pieces[0].astype(F32)
    for n in range(1, FOX_PARTS):
        packed = packed + pltpu.roll(pieces[n].astype(F32), n * N_HEADS, axis=1)
    packed = packed.astype(BF16)
    qa = (qa + _dot_nt(pcq_ref[...], packed)).astype(BF16)
    ka = ka + _dot(packed, pck_ref[...])
    row0 = pl.multiple_of(i * blk, blk)
    for h in range(N_HEADS):
        kaug_ref[h, pl.ds(row0, blk), :] = ka[:, h * FOX_AUG:(h + 1) * FOX_AUG].astype(BF16)
        vaug_ref[h, :, pl.ds(row0, blk)] = va[h * FOX_FEAT:(h + 1) * FOX_FEAT, :].astype(BF16)

    key_le_query = row <= col

    def scores(j, buf, diagonal=False):
        off = pl.multiple_of(j * blk, blk)
        for h in range(N_HEADS):
            s = _dot(kaug_ref[h, pl.ds(off, blk), 0:FOX_FEAT], qa[h * FOX_FEAT:(h + 1) * FOX_FEAT, :])
            buf[h] = jnp.where(key_le_query, s, -jnp.inf) if diagonal else s

    def consume(j, buf):
        off = pl.multiple_of(j * blk, blk)
        m_new = [jnp.maximum(m_ref[h], jnp.max(buf[h], axis=0, keepdims=True)) for h in range(N_HEADS)]
        for h in range(N_HEADS):
            p_ref[h] = jnp.exp(buf[h] - m_new[h]).astype(BF16)
        for h in range(N_HEADS):
            pv = _dot(vaug_ref[h, :, pl.ds(off, blk)], p_ref[h])
            acc_ref[h] = jnp.exp(m_ref[h] - m_new[h]) * acc_ref[h] + pv
            m_ref[h] = m_new[h]

    m_ref[...] = jnp.full_like(m_ref, -jnp.inf)
    acc_ref[...] = jnp.zeros_like(acc_ref)
    scores(i, s_ref, diagonal=True)

    def body(j, carry):
        scores(2 * j, s2_ref)
        consume(jnp.where(j == 0, i, 2 * j - 1), s_ref)
        scores(jnp.minimum(2 * j + 1, i - 1), s_ref)
        consume(2 * j, s2_ref)
        return carry

    lax.fori_loop(0, (i + 1) // 2, body, 0)

    @pl.when(i % 2 == 0)
    def _():
        consume(jnp.maximum(i - 1, 0), s_ref)

    outs = []
    for h in range(N_HEADS):
        acc = acc_ref[h]
        outs.append(acc[0:HEAD_DIM, :] / acc[HEAD_DIM:HEAD_DIM + 1, :])
    o_ref[...] = jnp.concatenate(outs, axis=0).T


def _fox(zd, fb, bsz, t):
    blk = FOX_BLOCK
    assert t % blk == 0
    nb = t // blk
    consts = _fox_placements()
    const = lambda a: pl.BlockSpec(a.shape, lambda b, i: (0,) * a.ndim)
    cur = lambda c: (lambda b, i: (b * nb + i, c))
    return pl.pallas_call(
        _fox_kernel,
        grid=(bsz, nb),
        in_specs=[
            pl.BlockSpec((blk, MIX_W), cur(0)),
            pl.BlockSpec((blk, MIX_W), cur(1)),
            pl.BlockSpec((blk, MIX_W), cur(2)),
            pl.BlockSpec((blk, LANES), cur(3 * MIX_W // LANES)),
            const(fb)] + [const(a) for a in consts],
        out_specs=pl.BlockSpec((blk, MIX_W), cur(0)),
        out_shape=jax.ShapeDtypeStruct((bsz * t, MIX_W), F32),
        scratch_shapes=[
            pltpu.VMEM((N_HEADS, t, FOX_AUG), BF16),
            pltpu.VMEM((N_HEADS, FOX_FEAT, t), BF16),
            pltpu.VMEM((N_HEADS, 1, blk), F32),
            pltpu.VMEM((N_HEADS, FOX_FEAT, blk), F32),
            pltpu.VMEM((SUBLANES, LANES), F32),
            pltpu.VMEM((N_HEADS, blk, blk), F32),
            pltpu.VMEM((N_HEADS, blk, blk), F32),
            pltpu.VMEM((N_HEADS, blk, blk), BF16),
        ],
        compiler_params=pltpu.CompilerParams(dimension_semantics=("arbitrary", "arbitrary")),
        name="fox",
    )(zd, zd, zd, zd, fb, *consts)


FFN_COL_CHUNK = 6 * MXU_COLS
FFN_ROW_PARTS = 2


def _outproj_ffn_kernel(x_ref, ya_ref, yb_ref, yc_ref, yd_ref, wo_ref, gmix_ref, gpre_ref, wg_ref, wu_ref, cw_ref,
                        cb_ref, wd_ref, gpost_ref, o_ref, tail_ref, *, tm, t):
    seq_start = (pl.program_id(0) * tm) % t == 0

    @pl.when(pl.program_id(0) == 0)
    def _():
        tail_ref[...] = jnp.zeros_like(tail_ref)

    rows = tm // FFN_ROW_PARTS
    gate_tails = {}

    def row_part(pi):
        rs = slice(pi * rows, (pi + 1) * rows)
        mixed = None
        for n, ref in enumerate((ya_ref, yb_ref, yc_ref, yd_ref)):
            part = _dot(ref[rs, :].astype(BF16), wo_ref[n * MIX_W:(n + 1) * MIX_W, :])
            mixed = part if mixed is None else mixed + part
        x = x_ref[rs, :] + mixed * _rms_scale(mixed) * gmix_ref[...]
        h = (x * _rms_scale(x) * gpre_ref[...]).astype(BF16)
        yield
        acc = None
        for c0 in range(0, D_FF, FFN_COL_CHUNK):
            cs = slice(c0, min(c0 + FFN_COL_CHUNK, D_FF))
            gate = _dot(h, wg_ref[:, cs])
            halo = jnp.where(seq_start, 0.0, tail_ref[:, cs]) if pi == 0 else gate_tails[pi - 1, c0]
            gate_tails[pi, c0] = gate[rows - SUBLANES:, :]
            if pi == FFN_ROW_PARTS - 1:
                tail_ref[:, cs] = gate_tails[pi, c0]
            conv = cb_ref[:, cs] + cw_ref[FFN_CONV - 1:FFN_CONV, cs] * gate
            for d in range(1, FFN_CONV):
                conv = conv + cw_ref[FFN_CONV - 1 - d:FFN_CONV - d, cs] * _shift_rows(gate, halo, d)
            f = jax.nn.gelu(conv, approximate=True) * _dot(h, wu_ref[:, cs])
            part = _dot(f.astype(BF16), wd_ref[cs, :])
            acc = part if acc is None else acc + part
            yield
        o_ref[rs, :] = x + acc * _rms_scale(acc) * gpost_ref[...]

    _trace_interleaved([(row_part(pi), 1) for pi in range(FFN_ROW_PARTS)])


def _outproj_ffn(x2, ys, w_out, g_mix, p, t, tm):
    rows = x2.shape[0]
    const = lambda a: pl.BlockSpec(a.shape, lambda i: (0, 0), pipeline_mode=pl.Buffered(1))
    params = [w_out, g_mix, p["g_pre"], p["w_gate"], p["w_up"], p["conv_w"], p["conv_b"], p["w_down"], p["g_post"]]
    return pl.pallas_call(
        functools.partial(_outproj_ffn_kernel, tm=tm, t=t),
        grid=(rows // tm,),
        in_specs=[pl.BlockSpec((tm, D_MODEL), lambda i: (i, 0))]
        + [pl.BlockSpec((tm, MIX_W), lambda i: (i, 0))] * len(ys)
        + [const(a) for a in params],
        out_specs=pl.BlockSpec((tm, D_MODEL), lambda i: (i, 0)),
        out_shape=jax.ShapeDtypeStruct((rows, D_MODEL), F32),
        scratch_shapes=[pltpu.VMEM((SUBLANES, D_FF), F32)],
        compiler_params=pltpu.CompilerParams(dimension_semantics=("arbitrary",),
                                             vmem_limit_bytes=56 * 1024 * 1024),
        name="outproj_ffn",
    )(x2, *ys, *params)


def _pad_cols(a, width):
    return jnp.pad(a, ((0, 0), (0, width - a.shape[1])))


def _row(a):
    return a.reshape(1, -1).astype(F32)


def _in_weights(w_in):
    wa_w = ZA_W
    wb_w = 3 * MIX_W + 2 * N_HEADS
    wc_w = ZC_W
    wt = jnp.transpose(w_in, (2, 0, 1)).astype(BF16)
    pad_rows = lambda a, n: jnp.pad(a, ((0, n - a.shape[0]), (0, 0), (0, 0)))
    groups = [wt[:wa_w],
              pad_rows(wt[wa_w:wa_w + wb_w], ZB_W),
              wt[wa_w + wb_w:wa_w + wb_w + wc_w],
              pad_rows(wt[wa_w + wb_w + wc_w:], ZD_W)]
    return jnp.concatenate(groups, axis=0).reshape(Z_W, -1)


def _layer_params(l, w_out, norm_mix_pre, norm_mix_post, norm_ffn_pre, norm_ffn_post,
                  rwkv_mu, rwkv_w0, rwkv_w_up, rwkv_a0, rwkv_a_up, rwkv_g_up, rwkv_k_k, rwkv_k_a,
                  rwkv_r_k, rwkv_ln_w, rwkv_ln_b, mlstm_conv_w, mlstm_conv_b, mlstm_b_i, mlstm_b_f,
                  mlstm_norm, swa_sinks, fox_b_f, ffn_w_up, ffn_conv_w, ffn_conv_b, ffn_w_down):
    gate_b = _pad_cols(jnp.concatenate([mlstm_b_i[l], mlstm_b_f[l]]).reshape(1, -1), LANES)
    return {
        "g_mix_pre": _row(norm_mix_pre[l]),
        "g_mix_post": _row(norm_mix_post[l]),
        "w_out": w_out[l].astype(BF16),
        "rwkv": {"mu": _row(rwkv_mu[l]), "w0": _row(rwkv_w0[l]), "w_up": rwkv_w_up[l].astype(BF16),
                 "a0": _row(rwkv_a0[l]), "a_up": rwkv_a_up[l].astype(BF16), "g_up": rwkv_g_up[l].astype(BF16),
                 "k_k": _row(rwkv_k_k[l]), "k_a": _row(rwkv_k_a[l]), "r_k": _row(rwkv_r_k[l]),
                 "ln_w": _row(rwkv_ln_w[l]), "ln_b": _row(rwkv_ln_b[l])},
        "mlstm": {"conv_w": mlstm_conv_w[l], "conv_b": _row(mlstm_conv_b[l]), "gate_b": gate_b,
                  "norm_g": _row(mlstm_norm[l])},
        "swa_sinks": swa_sinks[l],
        "fox_b": _pad_cols(fox_b_f[l].reshape(1, -1), LANES),
        "ffn": {"g_pre": _row(norm_ffn_pre[l]), "w_gate": ffn_w_up[l][:, :D_FF].astype(BF16),
                "w_up": ffn_w_up[l][:, D_FF:].astype(BF16), "conv_w": ffn_conv_w[l],
                "conv_b": _row(ffn_conv_b[l]), "w_down": ffn_w_down[l].astype(BF16),
                "g_post": _row(norm_ffn_post[l])},
    }


def _tiles(t):
    return min(512, t), min(512, t), min(512, t)


def kernel(x, w_in, w_out, norm_mix_pre, norm_mix_post, norm_ffn_pre, norm_ffn_post, rwkv_mu, rwkv_w0, rwkv_w_up, rwkv_a0, rwkv_a_up, rwkv_g_up, rwkv_k_k, rwkv_k_a, rwkv_r_k, rwkv_ln_w, rwkv_ln_b, mlstm_conv_w, mlstm_conv_b, mlstm_b_i, mlstm_b_f, mlstm_norm, swa_sinks, fox_b_f, rel_bias, ffn_w_up, ffn_conv_w, ffn_conv_b, ffn_w_down):
    bsz, t, d = x.shape
    assert d == D_MODEL and t % ATTN_BLOCK == 0
    tm, tc, tf = _tiles(t)
    x2 = x.reshape(bsz * t, d)
    w_in_all = _in_weights(w_in)
    for l in range(w_in.shape[0]):
        p = _layer_params(l, w_out, norm_mix_pre, norm_mix_post, norm_ffn_pre, norm_ffn_post,
                          rwkv_mu, rwkv_w0, rwkv_w_up, rwkv_a0, rwkv_a_up, rwkv_g_up, rwkv_k_k, rwkv_k_a,
                          rwkv_r_k, rwkv_ln_w, rwkv_ln_b, mlstm_conv_w, mlstm_conv_b, mlstm_b_i, mlstm_b_f,
                          mlstm_norm, swa_sinks, fox_b_f, ffn_w_up, ffn_conv_w, ffn_conv_b, ffn_w_down)
        za, zb, zc, zd = _inproj(x2, p["g_mix_pre"], w_in_all, l, tm)
        ya, yb, yc = _local_mixers(za, zb, zc, p["rwkv"], p["mlstm"], rel_bias, p["swa_sinks"], bsz, t, tc)
        yd = _fox(zd, p["fox_b"], bsz, t)
        x2 = _outproj_ffn(x2, (ya, yb, yc, yd), p["w_out"], p["g_mix_post"], p["ffn"], t, tf)
    return x2.reshape(bsz, t, d)
```

```python
import functools
import math

import jax
import jax.numpy as jnp
import numpy as np
from jax import lax
from jax.experimental import pallas as pl
from jax.experimental.pallas import tpu as pltpu

F32 = jnp.float32
BF16 = jnp.bfloat16

D_MODEL = 1024
HEAD_DIM = 64
N_HEADS = 4
MIX_W = N_HEADS * HEAD_DIM
RWKV_DECAY_RANK = 64
RWKV_AAA_RANK = 64
RWKV_GATE_RANK = 128
RWKV_LN_EPS = 64e-5
RWKV_CHUNK = 64
RWKV_CHUNK_GROUPS = 1
RWKV_STAGES_PER_STEP = 2
MLSTM_DK = 32
MLSTM_CONV = 4
MLSTM_CHUNK = 64
GATE_SOFTCAP = 15.0
SWA_KV_HEADS = 2
SWA_WINDOW = 128
ATTN_BLOCK = 128
REL_BUCKETS = 32
REL_MAX_DIST = 128
D_FF = 2816
FFN_CONV = 3
NORM_EPS = 1e-6

LANES = 128
SUBLANES = 8
MXU_COLS = 256

ZA_W = 3 * MIX_W + RWKV_DECAY_RANK + RWKV_AAA_RANK + RWKV_GATE_RANK
ZB_W = 3 * MIX_W + LANES
ZC_W = MIX_W + 2 * SWA_KV_HEADS * HEAD_DIM
ZD_W = 3 * MIX_W + LANES
Z_W = ZA_W + ZB_W + ZC_W + ZD_W

def _split_bf16(x, parts):
    out = []
    for n in range(parts):
        piece = x.astype(BF16)
        out.append(piece)
        if n + 1 < parts:
            x = x - piece.astype(F32)
    return out


def _dot(a, b, precision=None):
    return jnp.dot(a, b, preferred_element_type=F32, precision=precision)


def _dot_nt(a, b, precision=None):
    return lax.dot_general(a, b, (((1,), (1,)), ((), ())), preferred_element_type=F32, precision=precision)


def _dot_tn(a, b, precision=None):
    return lax.dot_general(a, b, (((0,), (0,)), ((), ())), preferred_element_type=F32, precision=precision)


def _dot_split(x, ones, parts):
    return sum(_dot(piece, ones) for piece in _split_bf16(x, parts))


def _split_dot(ones, x, parts):
    return sum(_dot(ones, piece) for piece in _split_bf16(x, parts))


def _bdot(a, b):
    return _dot(a.astype(BF16), b.astype(BF16))


def _bdot_nt(a, b):
    return _dot_nt(a.astype(BF16), b.astype(BF16))


def _bdot_tn(a, b):
    return _dot_tn(a.astype(BF16), b.astype(BF16))


def _sigmoid(x):
    return 1.0 / (1.0 + jnp.exp(-x))


def _log_sigmoid(x):
    return jnp.minimum(x, 0.0) - jnp.log(1.0 + jnp.exp(-jnp.abs(x)))


def _softplus(x):
    return jnp.maximum(x, 0.0) + jnp.log(1.0 + jnp.exp(-jnp.abs(x)))


def _rms_scale(x):
    return lax.rsqrt(jnp.mean(x * x, axis=-1, keepdims=True) + NORM_EPS)


def _shift_rows(x, prev_tail, d):
    rolled = pltpu.roll(x, d, axis=0)
    head_rows = lax.broadcasted_iota(jnp.int32, (SUBLANES, 1), 0)
    head = jnp.where(head_rows < d, pltpu.roll(prev_tail, d, axis=0), rolled[:SUBLANES])
    return jnp.concatenate([head, rolled[SUBLANES:]], axis=0)


def _idiv(x, n):
    assert n & (n - 1) == 0
    return lax.shift_right_logical(x, jnp.int32(n.bit_length() - 1))


def _imod(x, n):
    assert n & (n - 1) == 0
    return lax.bitwise_and(x, jnp.int32(n - 1))


def _head_ones(width, head):
    r = _idiv(lax.broadcasted_iota(jnp.int32, (width, width), 0), head)
    c = _idiv(lax.broadcasted_iota(jnp.int32, (width, width), 1), head)
    return (r == c).astype(BF16)


def _inproj_kernel(x_ref, g_ref, w_ref, za_ref, zb_ref, zc_ref, zd_ref):
    x = x_ref[...]
    h = (x * _rms_scale(x) * g_ref[...]).astype(BF16)
    z = _dot_nt(h, w_ref[...])
    off = 0
    for ref in (za_ref, zb_ref, zc_ref, zd_ref):
        w = ref.shape[1]
        ref[...] = z[:, off:off + w]
        off += w


def _inproj(x2, g, w, tm):
    rows = x2.shape[0]
    return pl.pallas_call(
        _inproj_kernel,
        grid=(rows // tm,),
        in_specs=[
            pl.BlockSpec((tm, D_MODEL), lambda i: (i, 0)),
            pl.BlockSpec((1, D_MODEL), lambda i: (0, 0)),
            pl.BlockSpec((Z_W, D_MODEL), lambda i: (0, 0)),
        ],
        out_specs=[pl.BlockSpec((tm, w_), lambda i: (i, 0)) for w_ in (ZA_W, ZB_W, ZC_W, ZD_W)],
        out_shape=[jax.ShapeDtypeStruct((rows, w_), F32) for w_ in (ZA_W, ZB_W, ZC_W, ZD_W)],
        compiler_params=pltpu.CompilerParams(dimension_semantics=("arbitrary",)),
        name="inproj",
    )(x2, g, w)


def _rwkv_kernel(z_ref, mu_ref, w0_ref, wup_ref, a0_ref, aup_ref, gup_ref, kk_ref, ka_ref, rk_ref,
                 lnw_ref, lnb_ref, o_ref, tail_ref, st_ref, y_ref, *, tc):
    L = RWKV_CHUNK
    W = MIX_W

    z = z_ref[...]
    zz = z + mu_ref[...] * (_shift_rows(z, tail_ref[...], 1) - z)
    tail_ref[...] = z[tc - SUBLANES:, :]

    r = zz[:, 0:W]
    k = zz[:, W:2 * W]
    v = zz[:, 2 * W:3 * W]
    o1 = 3 * W
    wd = zz[:, o1:o1 + RWKV_DECAY_RANK]
    ad = zz[:, o1 + RWKV_DECAY_RANK:o1 + RWKV_DECAY_RANK + RWKV_AAA_RANK]
    gd = zz[:, o1 + RWKV_DECAY_RANK + RWKV_AAA_RANK:]

    lw = -jnp.exp(-_softplus(-(w0_ref[...] + _bdot(jnp.tanh(wd), wup_ref[...]))) - 0.5)
    alpha = _sigmoid(a0_ref[...] + _bdot(ad, aup_ref[...]))
    g = _bdot(_sigmoid(gd), gup_ref[...])

    hsum = _head_ones(W, HEAD_DIM)
    kk = k * kk_ref[...]
    kk = kk * lax.rsqrt(jnp.maximum(_dot_split(kk * kk, hsum, 2), 1e-24))
    k = k * (1.0 + (alpha - 1.0) * ka_ref[...])

    span = min(tc, MXU_COLS)
    rt = lax.broadcasted_iota(jnp.int32, (span, span), 0)
    ct = lax.broadcasted_iota(jnp.int32, (span, span), 1)
    tri = ((_idiv(rt, L) == _idiv(ct, L)) & (ct <= rt)).astype(BF16)
    cum = jnp.concatenate([_split_dot(tri, lw[r0:r0 + span], 3) for r0 in range(0, tc, span)], axis=0)
    e_in = jnp.exp(cum)
    e_out = jnp.exp(-cum)
    r_t = r * e_in
    a_t = -kk * jnp.exp(cum - lw)
    b_t = kk * alpha * e_out
    k_t = k * e_out

    lane_head = lax.broadcasted_iota(jnp.int32, (1, W), 1) // HEAD_DIM

    def stack(xc):
        return jnp.concatenate([jnp.where(lane_head == h, xc, 0.0) for h in range(N_HEADS)], axis=0)

    n = N_HEADS * L
    rr = lax.broadcasted_iota(jnp.int32, (n, n), 0)
    cc = lax.broadcasted_iota(jnp.int32, (n, n), 1)
    own_head = _idiv(rr, L) == _idiv(cc, L)
    strict = own_head & (_imod(rr, L) > _imod(cc, L))
    incl = own_head & (_imod(rr, L) >= _imod(cc, L))
    eye = (rr == cc).astype(F32)

    over_heads = lambda x: jnp.concatenate([x, x], axis=1)

    def state_free(chunks):
        stk = [tuple(stack(u[sl]).astype(BF16) for u in (a_t, r_t, b_t, k_t, v)) for sl in chunks]
        bk_rep = [jnp.concatenate([b_t[sl]] * 2 + [k_t[sl]] * 2, axis=0).astype(BF16) for sl in chunks]
        ar = [_dot_nt(jnp.concatenate([s_[0], s_[1]], axis=0), x) for s_, x in zip(stk, bk_rep)]
        a_ab = [jnp.where(strict, over_heads(x[:n, :2 * L]), 0.0) for x in ar]
        a_ak = [jnp.where(strict, over_heads(x[:n, 2 * L:]), 0.0).astype(BF16) for x in ar]
        a_rb = [jnp.where(incl, over_heads(x[n:, :2 * L]), 0.0).astype(BF16) for x in ar]
        a_rk = [jnp.where(incl, over_heads(x[n:, 2 * L:]), 0.0).astype(BF16) for x in ar]
        yield
        inv = [eye + m for m in a_ab]
        pw = [m.astype(BF16) for m in a_ab]
        for _ in range(int(math.log2(L)) - 1):
            pw = [_dot(m, m).astype(BF16) for m in pw]
            inv = [t_ + _dot(t_.astype(BF16), m) for t_, m in zip(inv, pw)]
            yield
        inv = [t_.astype(BF16) for t_ in inv]
        akv = [_dot(m, s_[4]).astype(BF16) for m, s_ in zip(a_ak, stk)]
        wu = [_dot(t_, jnp.concatenate([s_[0], x], axis=1)) for t_, s_, x in zip(inv, stk, akv)]
        w_m = [x[:, :W].astype(BF16) for x in wu]
        u2 = [x[:, W:].astype(BF16) for x in wu]
        yield
        r_eff = [(s_[1].astype(F32) + _dot(m, w_)).astype(BF16) for s_, m, w_ in zip(stk, a_rb, w_m)]
        y_own = [_dot(m, u_) + _dot(n_, s_[4]) for m, u_, n_, s_ in zip(a_rb, u2, a_rk, stk)]
        yield
        st_mix = [_dot_tn(w_, s_[2]).astype(BF16) for w_, s_ in zip(w_m, stk)]
        st_own = [_dot_tn(u_, s_[2]) + _dot_tn(s_[4], s_[3]) for u_, s_ in zip(u2, stk)]
        return r_eff, y_own, st_mix, st_own

    def recurrence(st, sl, r_eff, y_own, st_mix, st_own):
        st_b = st.astype(BF16)
        y_s = _dot_nt(r_eff, st_b) + y_own
        y_c = y_s[0:L]
        for h in range(1, N_HEADS):
            y_c = y_c + y_s[h * L:(h + 1) * L]
        y_ref[sl, :] = y_c
        return (st + _dot(st_b, st_mix) + st_own) * e_in[sl.stop - 1:sl.stop, :]

    yield
    all_chunks = [slice(c * L, (c + 1) * L) for c in range(tc // L)]
    group = max(1, len(all_chunks) // RWKV_CHUNK_GROUPS)
    groups = [all_chunks[g0:g0 + group] for g0 in range(0, len(all_chunks), group)]
    st = st_ref[...]
    ready = yield from state_free(groups[0])
    for gi, chunks in enumerate(groups):
        ahead = state_free(groups[gi + 1]) if gi + 1 < len(groups) else iter(())
        nxt = None
        for c, sl in enumerate(chunks):
            st = recurrence(st, sl, *(part[c] for part in ready))
            for _ in range(RWKV_STAGES_PER_STEP):
                try:
                    next(ahead)
                except StopIteration as stop:
                    nxt = stop.value if stop.value is not None else nxt
            yield
        while gi + 1 < len(groups) and nxt is None:
            try:
                next(ahead)
                yield
            except StopIteration as stop:
                nxt = stop.value
        ready = nxt
    st_ref[...] = st

    y = y_ref[...]
    inv_n = 1.0 / HEAD_DIM
    mean = _dot_split(y, hsum, 2) * inv_n
    yc = y - mean
    var = _dot_split(yc * yc, hsum, 2) * inv_n
    y = yc * lax.rsqrt(var + RWKV_LN_EPS) * lnw_ref[...] + lnb_ref[...]
    bonus = _dot_split(r * k * rk_ref[...], hsum, 2) * v
    o_ref[...] = (y + bonus) * g


def _mlstm_kernel(z_ref, cw_ref, cb_ref, gb_ref, ng_ref, o_ref, tail_ref, c_ref, n_ref, m_ref, h_ref, *, tc):
    L = MLSTM_CHUNK
    W = MIX_W
    DK = MLSTM_DK
    DV = HEAD_DIM

    qk_in = z_ref[:, 0:W]
    v = z_ref[:, W:2 * W]
    og = z_ref[:, 2 * W:3 * W]
    gates = z_ref[:, 3 * W:3 * W + LANES]

    tail = tail_ref[...]
    conv = cb_ref[...] + cw_ref[MLSTM_CONV - 1:MLSTM_CONV, :] * qk_in
    for d in range(1, MLSTM_CONV):
        conv = conv + cw_ref[MLSTM_CONV - 1 - d:MLSTM_CONV - d, :] * _shift_rows(qk_in, tail, d)
    tail_ref[...] = qk_in[tc - SUBLANES:, :]
    qk = conv * _sigmoid(conv)
    q = qk[:, 0:N_HEADS * DK] * (DK ** -0.5)
    k = qk[:, N_HEADS * DK:]

    capped = GATE_SOFTCAP * jnp.tanh((gates + gb_ref[...]) / GATE_SOFTCAP)
    lf = _log_sigmoid(capped)

    gate_col = lax.broadcasted_iota(jnp.int32, (LANES, W), 0)
    lane_head = _idiv(lax.broadcasted_iota(jnp.int32, (LANES, W), 1), DV)
    pick_i = (gate_col == lane_head).astype(BF16)
    pick_f = (gate_col == N_HEADS + lane_head).astype(BF16)
    li_e = _dot_split(capped, pick_i, 3)
    span = min(tc, MXU_COLS)
    rt = lax.broadcasted_iota(jnp.int32, (span, span), 0)
    ct = lax.broadcasted_iota(jnp.int32, (span, span), 1)
    tri = ((_idiv(rt, L) == _idiv(ct, L)) & (ct <= rt)).astype(BF16)
    lf_e = [_dot(piece, pick_f).astype(BF16) for piece in _split_bf16(lf, 3)]
    b_e = jnp.concatenate([sum(_dot(tri, piece[r0:r0 + span]) for piece in lf_e)
                           for r0 in range(0, tc, span)], axis=0)

    key = lax.broadcasted_iota(jnp.int32, (L, W), 0)
    query = _imod(lax.broadcasted_iota(jnp.int32, (L, W), 1), L)
    on_diag = key == query
    causal_t = key <= query
    head_ones = _head_ones(W, DV)
    wide = lambda m: jnp.concatenate([m, m], axis=1)
    wide_lane_head = wide(lax.broadcasted_iota(jnp.int32, (1, W), 1) // DV)
    same_head_k = wide(_idiv(lax.broadcasted_iota(jnp.int32, (N_HEADS * DK, W), 0), DK)
                       == _idiv(lax.broadcasted_iota(jnp.int32, (N_HEADS * DK, W), 1), DV))
    q_lane_head = lax.broadcasted_iota(jnp.int32, (1, N_HEADS * DK), 1) // DK
    ones_b = jnp.ones((L, W), BF16)

    c_st = c_ref[...]
    n_st = n_ref[...]
    m_st = m_ref[0:1, :]
    for c in range(tc // L):
        yield
        sl = slice(c * L, (c + 1) * L)
        bc, lic, vc = b_e[sl], li_e[sl], v[sl]
        qc = q[sl]
        kc = k[sl].astype(BF16)
        q_stack = jnp.concatenate([jnp.where(q_lane_head == h, qc, 0.0) for h in range(N_HEADS)],
                                  axis=0).astype(BF16)
        b_q = jnp.sum(jnp.where(on_diag, bc, 0.0), axis=0, keepdims=True)
        dmat = jnp.where(causal_t, b_q - bc + lic, -jnp.inf)
        m_t = jnp.maximum(b_q + m_st, jnp.max(dmat, axis=0, keepdims=True))
        s_t = (_dot_nt(kc, q_stack) * jnp.exp(dmat - m_t)).astype(BF16)
        m_e = sum(_dot(jnp.where(on_diag, piece.astype(F32), 0.0).astype(BF16), head_ones)
                  for piece in _split_bf16(m_t, 3))
        inter = jnp.exp(bc + m_st - m_e)
        nv = _dot_tn(s_t, jnp.concatenate([vc.astype(BF16), ones_b], axis=1))
        nv = functools.reduce(lambda lo, h: jnp.where(wide_lane_head == h, nv[h * L:(h + 1) * L], lo),
                              range(1, N_HEADS), nv[0:L])
        qcn = _dot(qc.astype(BF16), jnp.concatenate([c_st, n_st], axis=1).astype(BF16))
        num = nv[:, :W] + inter * qcn[:, :W]
        den = nv[:, W:] + inter * qcn[:, W:]
        h_ref[sl, :] = num / jnp.maximum(jnp.abs(den), jnp.exp(-m_e))

        b_last = bc[L - 1:L, :]
        gexp = b_last - bc + lic
        m_new = jnp.maximum(b_last + m_st, jnp.max(gexp, axis=0, keepdims=True))
        wts = jnp.exp(gexp - m_new)
        dec = jnp.exp(b_last + m_st - m_new)
        upd = _dot_tn(kc, jnp.concatenate([wts * vc, wts], axis=1).astype(BF16))
        upd = jnp.where(same_head_k, upd, 0.0)
        c_st = dec * c_st + upd[:, :W]
        n_st = dec * n_st + upd[:, W:]
        m_st = m_new
    c_ref[...] = c_st
    n_ref[...] = n_st
    m_ref[0:1, :] = m_st
    yield

    hv = h_ref[...]
    ms = _dot_split(hv * hv, _head_ones(W, DV), 2) * (1.0 / DV)
    o_ref[...] = hv * lax.rsqrt(ms + NORM_EPS) * ng_ref[...] * _sigmoid(og)


def _t5_bucket(dist):
    max_exact = REL_BUCKETS // 2
    d = np.maximum(dist, 1).astype(np.float32)
    large = max_exact + (np.log(d / max_exact) / math.log(REL_MAX_DIST / max_exact)
                         * (REL_BUCKETS - max_exact)).astype(np.int32)
    large = np.minimum(large, REL_BUCKETS - 1)
    return np.where(dist < max_exact, dist, large).astype(np.int32)


def _swa_bias_table(rb_ref, bucket_ref, bias_ref):
    blk = ATTN_BLOCK
    grp = N_HEADS // SWA_KV_HEADS
    bucket = bucket_ref[...]
    for h in range(N_HEADS):
        acc = jnp.full((2 * blk, blk), -jnp.inf, F32)
        for bk in range(REL_BUCKETS):
            acc = jnp.where(bucket == bk, rb_ref[bk, h], acc)
        bias_ref[h // grp, :, (h % grp) * blk:(h % grp + 1) * blk] = acc


def _swa_kernel(sink_ref, q_ref, kp_ref, kc_ref, vp_ref, vc_ref, o_ref, bias_ref):
    blk = ATTN_BLOCK
    grp = N_HEADS // SWA_KV_HEADS
    kvw = SWA_KV_HEADS * HEAD_DIM
    key = lax.broadcasted_iota(jnp.int32, (2 * blk, grp * blk), 0)
    live = (key >= blk) | (pl.program_id(1) > 0)
    member = lax.broadcasted_iota(jnp.int32, (1, grp * blk), 1) // blk
    kw = jnp.concatenate([kp_ref[...], kc_ref[...]], axis=0).astype(BF16)
    vw = jnp.concatenate([vp_ref[...], vc_ref[...]], axis=0).astype(BF16)
    lane_member = lax.broadcasted_iota(jnp.int32, (1, grp * HEAD_DIM), 1) // HEAD_DIM
    rr = lax.broadcasted_iota(jnp.int32, (kvw, grp * HEAD_DIM), 0)
    cc = lax.broadcasted_iota(jnp.int32, (kvw, grp * HEAD_DIM), 1)
    vr = lax.broadcasted_iota(jnp.int32, (HEAD_DIM, kvw), 0)
    vc_ = lax.broadcasted_iota(jnp.int32, (HEAD_DIM, kvw), 1)
    n_sub = q_ref.shape[0] // blk
    k_rep, v_t, sinks = [], [], []
    for j in range(SWA_KV_HEADS):
        spread = ((_idiv(rr, HEAD_DIM) == j) & (_imod(rr, HEAD_DIM) == _imod(cc, HEAD_DIM))).astype(BF16)
        pick = ((_idiv(vc_, HEAD_DIM) == j) & (_imod(vc_, HEAD_DIM) == vr)).astype(BF16)
        k_rep.append(_dot(kw, spread).astype(BF16))
        v_t.append(_dot_nt(pick, vw).astype(BF16))
        sinks.append(jnp.where(member == 0, sink_ref[j * grp], sink_ref[j * grp + 1]))
    pairs = [(n, j) for n in range(n_sub) for j in range(SWA_KV_HEADS)]
    yield
    scores = []
    for n, j in pairs:
        qp = q_ref[n * blk:(n + 1) * blk, j * grp * HEAD_DIM:(j + 1) * grp * HEAD_DIM] * (HEAD_DIM ** -0.5)
        q_stack = jnp.concatenate([jnp.where(lane_member == g, qp, 0.0) for g in range(grp)],
                                  axis=0).astype(BF16)
        s = _dot_nt(k_rep[j][n * blk:(n + 2) * blk], q_stack) + bias_ref[j]
        scores.append(jnp.where(live, s, -jnp.inf) if n == 0 else s)
    yield
    probs = []
    for (n, j), s in zip(pairs, scores):
        m = jnp.maximum(jnp.max(s, axis=0, keepdims=True), sinks[j])
        p = jnp.exp(s - m)
        denom = jnp.sum(p, axis=0, keepdims=True) + jnp.exp(sinks[j] - m)
        probs.append((p / denom).astype(BF16))
    yield
    outs = [_dot(v_t[j][:, n * blk:(n + 2) * blk], p) for (n, j), p in zip(pairs, probs)]
    for n in range(n_sub):
        heads = [outs[n * SWA_KV_HEADS + j][:, g * blk:(g + 1) * blk]
                 for j in range(SWA_KV_HEADS) for g in range(grp)]
        o_ref[n * blk:(n + 1) * blk, :] = jnp.concatenate(heads, axis=0).T


N_RWKV_PARAMS = 11
N_MLSTM_PARAMS = 4
N_RWKV_SCRATCH = 3
N_MLSTM_SCRATCH = 5


def _trace_interleaved(staged):
    done = object()
    staged = list(staged)
    while staged:
        for item in list(staged):
            body, stride = item
            for _ in range(stride):
                if next(body, done) is done:
                    staged.remove(item)
                    break


def _local_mixers_kernel(*refs, tc):
    it = iter(refs)
    take = lambda n: [next(it) for _ in range(n)]
    (za_ref,), rwkv_p = take(1), take(N_RWKV_PARAMS)
    (zb_ref,), mlstm_p = take(1), take(N_MLSTM_PARAMS)
    rb_ref, sink_ref, bucket_ref = take(3)
    swa_in = take(5)
    ya_ref, yb_ref, yc_ref = take(3)
    rwkv_s, mlstm_s = take(N_RWKV_SCRATCH), take(N_MLSTM_SCRATCH)
    (bias_ref,) = take(1)

    @pl.when((pl.program_id(0) == 0) & (pl.program_id(1) == 0))
    def _():
        _swa_bias_table(rb_ref, bucket_ref, bias_ref)

    @pl.when(pl.program_id(1) == 0)
    def _():
        for ref in rwkv_s[:2] + mlstm_s[:4]:
            ref[...] = jnp.zeros_like(ref)

    _trace_interleaved([
        (_rwkv_kernel(za_ref, *rwkv_p, ya_ref, *rwkv_s, tc=tc), 2),
        (_mlstm_kernel(zb_ref, *mlstm_p, yb_ref, *mlstm_s, tc=tc), 1),
        (_swa_kernel(sink_ref, *swa_in, yc_ref, bias_ref), 1),
    ])


def _local_mixers(za, zb, zc, pr, pm, rel_bias, sinks, bsz, t, tc):
    blk = ATTN_BLOCK
    assert tc % blk == 0 and t % tc == 0 and N_HEADS // SWA_KV_HEADS == 2
    nt = t // tc
    full = lambda a: pl.BlockSpec(a.shape, lambda b, i: (0,) * a.ndim)
    tile = lambda w, c=0: pl.BlockSpec((tc, w), lambda b, i: (b * nt + i, c))
    rwkv_p = [pr["mu"], pr["w0"], pr["w_up"], pr["a0"], pr["a_up"], pr["g_up"], pr["k_k"], pr["k_a"], pr["r_k"],
              pr["ln_w"], pr["ln_b"]]
    mlstm_p = [pm["conv_w"], pm["conv_b"], pm["gate_b"], pm["norm_g"]]
    assert len(rwkv_p) == N_RWKV_PARAMS and len(mlstm_p) == N_MLSTM_PARAMS
    dist = np.arange(blk)[None, :] + blk - np.arange(2 * blk)[:, None]
    bucket = jnp.asarray(np.where((dist >= 0) & (dist < SWA_WINDOW),
                                  _t5_bucket(np.clip(dist, 0, SWA_WINDOW - 1)), -1).astype(np.int32))
    kvw = SWA_KV_HEADS * HEAD_DIM
    kcol = MIX_W // kvw
    sub = tc // blk
    before = lambda c: pl.BlockSpec((blk, kvw), lambda b, i: (b * nt * sub + jnp.maximum(i * sub - 1, 0), c))
    smem = pl.BlockSpec(memory_space=pltpu.SMEM)
    return pl.pallas_call(
        functools.partial(_local_mixers_kernel, tc=tc),
        grid=(bsz, nt),
        in_specs=[tile(ZA_W)] + [full(a) for a in rwkv_p] + [tile(ZB_W)] + [full(a) for a in mlstm_p]
        + [smem, smem, full(bucket), tile(MIX_W), before(kcol), tile(kvw, kcol), before(kcol + 1), tile(kvw, kcol + 1)],
        out_specs=[tile(MIX_W)] * 3,
        out_shape=[jax.ShapeDtypeStruct((bsz * t, MIX_W), F32)] * 3,
        scratch_shapes=[
            pltpu.VMEM((SUBLANES, ZA_W), F32),
            pltpu.VMEM((MIX_W, MIX_W), F32),
            pltpu.VMEM((tc, MIX_W), F32),
            pltpu.VMEM((SUBLANES, MIX_W), F32),
            pltpu.VMEM((N_HEADS * MLSTM_DK, MIX_W), F32),
            pltpu.VMEM((N_HEADS * MLSTM_DK, MIX_W), F32),
            pltpu.VMEM((SUBLANES, MIX_W), F32),
            pltpu.VMEM((tc, MIX_W), F32),
            pltpu.VMEM((SWA_KV_HEADS, 2 * blk, 2 * blk), F32),
        ],
        compiler_params=pltpu.CompilerParams(dimension_semantics=("arbitrary", "arbitrary")),
        name="local_mixers",
    )(za, *rwkv_p, zb, *mlstm_p, rel_bias, sinks, bucket, zc, zc, zc, zc, zc)


FOX_BLOCK = 256
FOX_AUG = LANES
FOX_FEAT = 80
FOX_PARTS = 3


def _fox_placements():
    wide = N_HEADS * FOX_AUG
    pk = np.zeros((MIX_W, wide), np.float32)
    pck = np.zeros((LANES, wide), np.float32)
    ones_k = np.zeros((SUBLANES, wide), np.float32)
    tall = N_HEADS * FOX_FEAT
    pq_t = np.zeros((tall, MIX_W), np.float32)
    pv_t = np.zeros((tall, MIX_W), np.float32)
    pcq_t = np.zeros((tall, LANES), np.float32)
    ones_t = np.zeros((2, tall, LANES), np.float32)
    assert HEAD_DIM + 2 * FOX_PARTS <= FOX_FEAT <= FOX_AUG
    for h in range(N_HEADS):
        base = h * FOX_AUG
        base_t = h * FOX_FEAT
        for d in range(HEAD_DIM):
            pk[h * HEAD_DIM + d, base + d] = 1.0
            pq_t[base_t + d, h * HEAD_DIM + d] = HEAD_DIM ** -0.5
            pv_t[base_t + d, h * HEAD_DIM + d] = 1.0
        for n in range(FOX_PARTS):
            pcq_t[base_t + HEAD_DIM + n, n * N_HEADS + h] = 1.0
            pck[n * N_HEADS + h, base + HEAD_DIM + FOX_PARTS + n] = -1.0
            ones_t[0, base_t + HEAD_DIM + FOX_PARTS + n, :] = 1.0
            ones_k[0, base + HEAD_DIM + n] = 1.0
        ones_t[1, base_t + HEAD_DIM, :] = 1.0
    bf = lambda a: jnp.asarray(a, BF16)
    return bf(pk), bf(pck), jnp.asarray(ones_k), bf(np.stack([pq_t, pv_t])), bf(pcq_t), jnp.asarray(ones_t)


def _fox_kernel(q_ref, k_ref, v_ref, f_ref, fb_ref, pk_ref, pck_ref, onesk_ref, pqv_ref, pcq_ref, onest_ref,
                o_ref, kaug_ref, vaug_ref, m_ref, acc_ref, clast_ref, s_ref, s2_ref, p_ref):
    blk = FOX_BLOCK
    i = pl.program_id(1)

    @pl.when(i == 0)
    def _():
        clast_ref[...] = jnp.zeros_like(clast_ref)

    ls = _log_sigmoid(f_ref[...] + fb_ref[...])
    row = lax.broadcasted_iota(jnp.int32, (blk, blk), 0)
    col = lax.broadcasted_iota(jnp.int32, (blk, blk), 1)
    cq = _split_dot((col <= row).astype(BF16), ls, FOX_PARTS) + clast_ref[0:1, :]
    clast_ref[0:1, :] = cq[blk - 1:blk, :]

    widen = lambda a: jnp.concatenate([a] * (blk // LANES), axis=1)
    qa = _dot_nt(pqv_ref[0], q_ref[...].astype(BF16)) + widen(onest_ref[0])
    va = _dot_nt(pqv_ref[1], v_ref[...].astype(BF16)) + widen(onest_ref[1])
    ka = _dot(k_ref[...].astype(BF16), pk_ref[...]) + onesk_ref[0:1, :]
    lane = lax.broadcasted_iota(jnp.int32, (1, LANES), 1)
    pieces = _split_bf16(jnp.where(lane < N_HEADS, cq, 0.0), FOX_PARTS)
    packed = pieces[0].astype(F32)
    for n in range(1, FOX_PARTS):
        packed = packed + pltpu.roll(pieces[n].astype(F32), n * N_HEADS, axis=1)
    packed = packed.astype(BF16)
    qa = (qa + _dot_nt(pcq_ref[...], packed)).astype(BF16)
    ka = ka + _dot(packed, pck_ref[...])
    row0 = pl.multiple_of(i * blk, blk)
    for h in range(N_HEADS):
        kaug_ref[h, pl.ds(row0, blk), :] = ka[:, h * FOX_AUG:(h + 1) * FOX_AUG].astype(BF16)
        vaug_ref[h, :, pl.ds(row0, blk)] = va[h * FOX_FEAT:(h + 1) * FOX_FEAT, :].astype(BF16)

    key_le_query = row <= col

    def scores(j, buf, diagonal=False):
        off = pl.multiple_of(j * blk, blk)
        for h in range(N_HEADS):
            s = _dot(kaug_ref[h, pl.ds(off, blk), 0:FOX_FEAT], qa[h * FOX_FEAT:(h + 1) * FOX_FEAT, :])
            buf[h] = jnp.where(key_le_query, s, -jnp.inf) if diagonal else s

    def consume(j, buf):
        off = pl.multiple_of(j * blk, blk)
        m_new = [jnp.maximum(m_ref[h], jnp.max(buf[h], axis=0, keepdims=True)) for h in range(N_HEADS)]
        for h in range(N_HEADS):
            p_ref[h] = jnp.exp(buf[h] - m_new[h]).astype(BF16)
        for h in range(N_HEADS):
            pv = _dot(vaug_ref[h, :, pl.ds(off, blk)], p_ref[h])
            acc_ref[h] = jnp.exp(m_ref[h] - m_new[h]) * acc_ref[h] + pv
            m_ref[h] = m_new[h]

    m_ref[...] = jnp.full_like(m_ref, -jnp.inf)
    acc_ref[...] = jnp.zeros_like(acc_ref)
    scores(i, s_ref, diagonal=True)

    def body(j, carry):
        scores(2 * j, s2_ref)
        consume(jnp.where(j == 0, i, 2 * j - 1), s_ref)
        scores(jnp.minimum(2 * j + 1, i - 1), s_ref)
        consume(2 * j, s2_ref)
        return carry

    lax.fori_loop(0, (i + 1) // 2, body, 0)

    @pl.when(i % 2 == 0)
    def _():
        consume(jnp.maximum(i - 1, 0), s_ref)

    outs = []
    for h in range(N_HEADS):
        acc = acc_ref[h]
        outs.append(acc[0:HEAD_DIM, :] / acc[HEAD_DIM:HEAD_DIM + 1, :])
    o_ref[...] = jnp.concatenate(outs, axis=0).T


def _fox(zd, fb, bsz, t):
    blk = FOX_BLOCK
    assert t % blk == 0
    nb = t // blk
    consts = _fox_placements()
    const = lambda a: pl.BlockSpec(a.shape, lambda b, i: (0,) * a.ndim)
    cur = lambda c: (lambda b, i: (b * nb + i, c))
    return pl.pallas_call(
        _fox_kernel,
        grid=(bsz, nb),
        in_specs=[
            pl.BlockSpec((blk, MIX_W), cur(0)),
            pl.BlockSpec((blk, MIX_W), cur(1)),
            pl.BlockSpec((blk, MIX_W), cur(2)),
            pl.BlockSpec((blk, LANES), cur(3 * MIX_W // LANES)),
            const(fb)] + [const(a) for a in consts],
        out_specs=pl.BlockSpec((blk, MIX_W), cur(0)),
        out_shape=jax.ShapeDtypeStruct((bsz * t, MIX_W), F32),
        scratch_shapes=[
            pltpu.VMEM((N_HEADS, t, FOX_AUG), BF16),
            pltpu.VMEM((N_HEADS, FOX_FEAT, t), BF16),
            pltpu.VMEM((N_HEADS, 1, blk), F32),
            pltpu.VMEM((N_HEADS, FOX_FEAT, blk), F32),
            pltpu.VMEM((SUBLANES, LANES), F32),
            pltpu.VMEM((N_HEADS, blk, blk), F32),
            pltpu.VMEM((N_HEADS, blk, blk), F32),
            pltpu.VMEM((N_HEADS, blk, blk), BF16),
        ],
        compiler_params=pltpu.CompilerParams(dimension_semantics=("arbitrary", "arbitrary")),
        name="fox",
    )(zd, zd, zd, zd, fb, *consts)


FFN_COL_CHUNK = 6 * MXU_COLS
FFN_ROW_PARTS = 2


def _outproj_ffn_kernel(x_ref, ya_ref, yb_ref, yc_ref, yd_ref, wo_ref, gmix_ref, gpre_ref, wg_ref, wu_ref, cw_ref,
                        cb_ref, wd_ref, gpost_ref, o_ref, tail_ref, *, tm, t):
    seq_start = (pl.program_id(0) * tm) % t == 0

    @pl.when(pl.program_id(0) == 0)
    def _():
        tail_ref[...] = jnp.zeros_like(tail_ref)

    rows = tm // FFN_ROW_PARTS
    gate_tails = {}

    def row_part(pi):
        rs = slice(pi * rows, (pi + 1) * rows)
        mixed = None
        for n, ref in enumerate((ya_ref, yb_ref, yc_ref, yd_ref)):
            part = _dot(ref[rs, :].astype(BF16), wo_ref[n * MIX_W:(n + 1) * MIX_W, :])
            mixed = part if mixed is None else mixed + part
        x = x_ref[rs, :] + mixed * _rms_scale(mixed) * gmix_ref[...]
        h = (x * _rms_scale(x) * gpre_ref[...]).astype(BF16)
        yield
        acc = None
        for c0 in range(0, D_FF, FFN_COL_CHUNK):
            cs = slice(c0, min(c0 + FFN_COL_CHUNK, D_FF))
            gate = _dot(h, wg_ref[:, cs])
            halo = jnp.where(seq_start, 0.0, tail_ref[:, cs]) if pi == 0 else gate_tails[pi - 1, c0]
            gate_tails[pi, c0] = gate[rows - SUBLANES:, :]
            if pi == FFN_ROW_PARTS - 1:
                tail_ref[:, cs] = gate_tails[pi, c0]
            conv = cb_ref[:, cs] + cw_ref[FFN_CONV - 1:FFN_CONV, cs] * gate
            for d in range(1, FFN_CONV):
                conv = conv + cw_ref[FFN_CONV - 1 - d:FFN_CONV - d, cs] * _shift_rows(gate, halo, d)
            f = jax.nn.gelu(conv, approximate=True) * _dot(h, wu_ref[:, cs])
            part = _dot(f.astype(BF16), wd_ref[cs, :])
            acc = part if acc is None else acc + part
            yield
        o_ref[rs, :] = x + acc * _rms_scale(acc) * gpost_ref[...]

    _trace_interleaved([(row_part(pi), 1) for pi in range(FFN_ROW_PARTS)])


def _outproj_ffn(x2, ys, w_out, g_mix, p, t, tm):
    rows = x2.shape[0]
    const = lambda a: pl.BlockSpec(a.shape, lambda i: (0, 0), pipeline_mode=pl.Buffered(1))
    params = [w_out, g_mix, p["g_pre"], p["w_gate"], p["w_up"], p["conv_w"], p["conv_b"], p["w_down"], p["g_post"]]
    return pl.pallas_call(
        functools.partial(_outproj_ffn_kernel, tm=tm, t=t),
        grid=(rows // tm,),
        in_specs=[pl.BlockSpec((tm, D_MODEL), lambda i: (i, 0))]
        + [pl.BlockSpec((tm, MIX_W), lambda i: (i, 0))] * len(ys)
        + [const(a) for a in params],
        out_specs=pl.BlockSpec((tm, D_MODEL), lambda i: (i, 0)),
        out_shape=jax.ShapeDtypeStruct((rows, D_MODEL), F32),
        scratch_shapes=[pltpu.VMEM((SUBLANES, D_FF), F32)],
        compiler_params=pltpu.CompilerParams(dimension_semantics=("arbitrary",),
                                             vmem_limit_bytes=56 * 1024 * 1024),
        name="outproj_ffn",
    )(x2, *ys, *params)


def _pad_cols(a, width):
    return jnp.pad(a, ((0, 0), (0, width - a.shape[1])))


def _row(a):
    return a.reshape(1, -1).astype(F32)


def _in_weights(w_in, l):
    wa_w = ZA_W
    wb_w = 3 * MIX_W + 2 * N_HEADS
    wc_w = ZC_W
    wi = jnp.transpose(w_in, (2, 0, 1))[:, l, :].astype(BF16)
    pad_rows = lambda a, n: jnp.pad(a, ((0, n - a.shape[0]), (0, 0)))
    groups = [wi[:wa_w],
              pad_rows(wi[wa_w:wa_w + wb_w], ZB_W),
              wi[wa_w + wb_w:wa_w + wb_w + wc_w],
              pad_rows(wi[wa_w + wb_w + wc_w:], ZD_W)]
    return jnp.concatenate(groups, axis=0)


def _layer_params(l, w_out, norm_mix_pre, norm_mix_post, norm_ffn_pre, norm_ffn_post,
                  rwkv_mu, rwkv_w0, rwkv_w_up, rwkv_a0, rwkv_a_up, rwkv_g_up, rwkv_k_k, rwkv_k_a,
                  rwkv_r_k, rwkv_ln_w, rwkv_ln_b, mlstm_conv_w, mlstm_conv_b, mlstm_b_i, mlstm_b_f,
                  mlstm_norm, swa_sinks, fox_b_f, ffn_w_up, ffn_conv_w, ffn_conv_b, ffn_w_down):
    gate_b = _pad_cols(jnp.concatenate([mlstm_b_i[l], mlstm_b_f[l]]).reshape(1, -1), LANES)
    return {
        "g_mix_pre": _row(norm_mix_pre[l]),
        "g_mix_post": _row(norm_mix_post[l]),
        "w_out": w_out[l].astype(BF16),
        "rwkv": {"mu": _row(rwkv_mu[l]), "w0": _row(rwkv_w0[l]), "w_up": rwkv_w_up[l].astype(BF16),
                 "a0": _row(rwkv_a0[l]), "a_up": rwkv_a_up[l].astype(BF16), "g_up": rwkv_g_up[l].astype(BF16),
                 "k_k": _row(rwkv_k_k[l]), "k_a": _row(rwkv_k_a[l]), "r_k": _row(rwkv_r_k[l]),
                 "ln_w": _row(rwkv_ln_w[l]), "ln_b": _row(rwkv_ln_b[l])},
        "mlstm": {"conv_w": mlstm_conv_w[l], "conv_b": _row(mlstm_conv_b[l]), "gate_b": gate_b,
                  "norm_g": _row(mlstm_norm[l])},
        "swa_sinks": swa_sinks[l],
        "fox_b": _pad_cols(fox_b_f[l].reshape(1, -1), LANES),
        "ffn": {"g_pre": _row(norm_ffn_pre[l]), "w_gate": ffn_w_up[l][:, :D_FF].astype(BF16),
                "w_up": ffn_w_up[l][:, D_FF:].astype(BF16), "conv_w": ffn_conv_w[l],
                "conv_b": _row(ffn_conv_b[l]), "w_down": ffn_w_down[l].astype(BF16),
                "g_post": _row(norm_ffn_post[l])},
    }


def _tiles(t):
    return min(512, t), min(512, t), min(512, t)


def kernel(x, w_in, w_out, norm_mix_pre, norm_mix_post, norm_ffn_pre, norm_ffn_post, rwkv_mu, rwkv_w0, rwkv_w_up, rwkv_a0, rwkv_a_up, rwkv_g_up, rwkv_k_k, rwkv_k_a, rwkv_r_k, rwkv_ln_w, rwkv_ln_b, mlstm_conv_w, mlstm_conv_b, mlstm_b_i, mlstm_b_f, mlstm_norm, swa_sinks, fox_b_f, rel_bias, ffn_w_up, ffn_conv_w, ffn_conv_b, ffn_w_down):
    bsz, t, d = x.shape
    assert d == D_MODEL and t % ATTN_BLOCK == 0
    tm, tc, tf = _tiles(t)
    x2 = x.reshape(bsz * t, d)
    for l in range(w_in.shape[0]):
        p = _layer_params(l, w_out, norm_mix_pre, norm_mix_post, norm_ffn_pre, norm_ffn_post,
                          rwkv_mu, rwkv_w0, rwkv_w_up, rwkv_a0, rwkv_a_up, rwkv_g_up, rwkv_k_k, rwkv_k_a,
                          rwkv_r_k, rwkv_ln_w, rwkv_ln_b, mlstm_conv_w, mlstm_conv_b, mlstm_b_i, mlstm_b_f,
                          mlstm_norm, swa_sinks, fox_b_f, ffn_w_up, ffn_conv_w, ffn_conv_b, ffn_w_down)
        za, zb, zc, zd = _inproj(x2, p["g_mix_pre"], _in_weights(w_in, l), tm)
        ya, yb, yc = _local_mixers(za, zb, zc, p["rwkv"], p["mlstm"], rel_bias, p["swa_sinks"], bsz, t, tc)
        yd = _fox(zd, p["fox_b"], bsz, t)
        x2 = _outproj_ffn(x2, (ya, yb, yc, yd), p["w_out"], p["g_mix_post"], p["ffn"], t, tf)
    return x2.reshape(bsz, t, d)
```

```python
import functools
import math

import jax
import jax.numpy as jnp
import numpy as np
from jax import lax
from jax.experimental import pallas as pl
from jax.experimental.pallas import tpu as pltpu

F32 = jnp.float32
BF16 = jnp.bfloat16

D_MODEL = 1024
HEAD_DIM = 64
N_HEADS = 4
MIX_W = N_HEADS * HEAD_DIM
RWKV_DECAY_RANK = 64
RWKV_AAA_RANK = 64
RWKV_GATE_RANK = 128
RWKV_LN_EPS = 64e-5
RWKV_CHUNK = 64
RWKV_CHUNK_GROUPS = 1
RWKV_STAGES_PER_STEP = 2
MLSTM_DK = 32
MLSTM_CONV = 4
MLSTM_CHUNK = 64
GATE_SOFTCAP = 15.0
SWA_KV_HEADS = 2
SWA_WINDOW = 128
ATTN_BLOCK = 128
REL_BUCKETS = 32
REL_MAX_DIST = 128
D_FF = 2816
FFN_CONV = 3
NORM_EPS = 1e-6

LANES = 128
SUBLANES = 8
MXU_COLS = 256

ZA_W = 3 * MIX_W + RWKV_DECAY_RANK + RWKV_AAA_RANK + RWKV_GATE_RANK
ZB_W = 3 * MIX_W + LANES
ZC_W = MIX_W + 2 * SWA_KV_HEADS * HEAD_DIM
ZD_W = 3 * MIX_W + LANES
Z_W = ZA_W + ZB_W + ZC_W + ZD_W

def _split_bf16(x, parts):
    out = []
    for n in range(parts):
        piece = x.astype(BF16)
        out.append(piece)
        if n + 1 < parts:
            x = x - piece.astype(F32)
    return out


def _dot(a, b, precision=None):
    return jnp.dot(a, b, preferred_element_type=F32, precision=precision)


def _dot_nt(a, b, precision=None):
    return lax.dot_general(a, b, (((1,), (1,)), ((), ())), preferred_element_type=F32, precision=precision)


def _dot_tn(a, b, precision=None):
    return lax.dot_general(a, b, (((0,), (0,)), ((), ())), preferred_element_type=F32, precision=precision)


def _dot_split(x, ones, parts):
    return sum(_dot(piece, ones) for piece in _split_bf16(x, parts))


def _split_dot(ones, x, parts):
    return sum(_dot(ones, piece) for piece in _split_bf16(x, parts))


def _bdot(a, b):
    return _dot(a.astype(BF16), b.astype(BF16))


def _bdot_nt(a, b):
    return _dot_nt(a.astype(BF16), b.astype(BF16))


def _bdot_tn(a, b):
    return _dot_tn(a.astype(BF16), b.astype(BF16))


def _sigmoid(x):
    return 1.0 / (1.0 + jnp.exp(-x))


def _log_sigmoid(x):
    return jnp.minimum(x, 0.0) - jnp.log(1.0 + jnp.exp(-jnp.abs(x)))


def _softplus(x):
    return jnp.maximum(x, 0.0) + jnp.log(1.0 + jnp.exp(-jnp.abs(x)))


def _rms_scale(x):
    return lax.rsqrt(jnp.mean(x * x, axis=-1, keepdims=True) + NORM_EPS)


def _shift_rows(x, prev_tail, d):
    rolled = pltpu.roll(x, d, axis=0)
    head_rows = lax.broadcasted_iota(jnp.int32, (SUBLANES, 1), 0)
    head = jnp.where(head_rows < d, pltpu.roll(prev_tail, d, axis=0), rolled[:SUBLANES])
    return jnp.concatenate([head, rolled[SUBLANES:]], axis=0)


def _idiv(x, n):
    assert n & (n - 1) == 0
    return lax.shift_right_logical(x, jnp.int32(n.bit_length() - 1))


def _imod(x, n):
    assert n & (n - 1) == 0
    return lax.bitwise_and(x, jnp.int32(n - 1))


def _head_ones(width, head):
    r = _idiv(lax.broadcasted_iota(jnp.int32, (width, width), 0), head)
    c = _idiv(lax.broadcasted_iota(jnp.int32, (width, width), 1), head)
    return (r == c).astype(BF16)


def _inproj_kernel(x_ref, g_ref, w_ref, za_ref, zb_ref, zc_ref, zd_ref):
    x = x_ref[...]
    h = (x * _rms_scale(x) * g_ref[...]).astype(BF16)
    z = _dot_nt(h, w_ref[...])
    off = 0
    for ref in (za_ref, zb_ref, zc_ref, zd_ref):
        w = ref.shape[1]
        ref[...] = z[:, off:off + w]
        off += w


def _inproj(x2, g, w, layer, tm):
    rows = x2.shape[0]
    return pl.pallas_call(
        _inproj_kernel,
        grid=(rows // tm,),
        in_specs=[
            pl.BlockSpec((tm, D_MODEL), lambda i: (i, 0)),
            _lspec(g, layer),
            pl.BlockSpec((Z_W, D_MODEL), lambda i: (0, 0)),
        ],
        out_specs=[pl.BlockSpec((tm, w_), lambda i: (i, 0)) for w_ in (ZA_W, ZB_W, ZC_W, ZD_W)],
        out_shape=[jax.ShapeDtypeStruct((rows, w_), F32) for w_ in (ZA_W, ZB_W, ZC_W, ZD_W)],
        compiler_params=pltpu.CompilerParams(dimension_semantics=("arbitrary",)),
        name="inproj",
    )(x2, g, w)


def _rwkv_kernel(z_ref, mu_ref, w0_ref, wup_ref, a0_ref, aup_ref, gup_ref, kk_ref, ka_ref, rk_ref,
                 lnw_ref, lnb_ref, o_ref, tail_ref, st_ref, y_ref, *, tc):
    L = RWKV_CHUNK
    W = MIX_W

    z = z_ref[...]
    zz = z + mu_ref[...] * (_shift_rows(z, tail_ref[...], 1) - z)
    tail_ref[...] = z[tc - SUBLANES:, :]

    r = zz[:, 0:W]
    k = zz[:, W:2 * W]
    v = zz[:, 2 * W:3 * W]
    o1 = 3 * W
    wd = zz[:, o1:o1 + RWKV_DECAY_RANK]
    ad = zz[:, o1 + RWKV_DECAY_RANK:o1 + RWKV_DECAY_RANK + RWKV_AAA_RANK]
    gd = zz[:, o1 + RWKV_DECAY_RANK + RWKV_AAA_RANK:]

    lw = -jnp.exp(-_softplus(-(w0_ref[...] + _bdot(jnp.tanh(wd), wup_ref[...]))) - 0.5)
    alpha = _sigmoid(a0_ref[...] + _bdot(ad, aup_ref[...]))
    g = _bdot(_sigmoid(gd), gup_ref[...])

    hsum = _head_ones(W, HEAD_DIM)
    kk = k * kk_ref[...]
    kk = kk * lax.rsqrt(jnp.maximum(_dot_split(kk * kk, hsum, 2), 1e-24))
    k = k * (1.0 + (alpha - 1.0) * ka_ref[...])

    span = min(tc, MXU_COLS)
    rt = lax.broadcasted_iota(jnp.int32, (span, span), 0)
    ct = lax.broadcasted_iota(jnp.int32, (span, span), 1)
    tri = ((_idiv(rt, L) == _idiv(ct, L)) & (ct <= rt)).astype(BF16)
    cum = jnp.concatenate([_split_dot(tri, lw[r0:r0 + span], 3) for r0 in range(0, tc, span)], axis=0)
    e_in = jnp.exp(cum)
    e_out = jnp.exp(-cum)
    r_t = r * e_in
    a_t = -kk * jnp.exp(cum - lw)
    b_t = kk * alpha * e_out
    k_t = k * e_out

    lane_head = lax.broadcasted_iota(jnp.int32, (1, W), 1) // HEAD_DIM

    def stack(xc):
        return jnp.concatenate([jnp.where(lane_head == h, xc, 0.0) for h in range(N_HEADS)], axis=0)

    n = N_HEADS * L
    rr = lax.broadcasted_iota(jnp.int32, (n, n), 0)
    cc = lax.broadcasted_iota(jnp.int32, (n, n), 1)
    own_head = _idiv(rr, L) == _idiv(cc, L)
    strict = own_head & (_imod(rr, L) > _imod(cc, L))
    incl = own_head & (_imod(rr, L) >= _imod(cc, L))
    eye = (rr == cc).astype(F32)

    over_heads = lambda x: jnp.concatenate([x, x], axis=1)

    def state_free(chunks):
        stk = [tuple(stack(u[sl]).astype(BF16) for u in (a_t, r_t, b_t, k_t, v)) for sl in chunks]
        bk_rep = [jnp.concatenate([b_t[sl]] * 2 + [k_t[sl]] * 2, axis=0).astype(BF16) for sl in chunks]
        ar = [_dot_nt(jnp.concatenate([s_[0], s_[1]], axis=0), x) for s_, x in zip(stk, bk_rep)]
        a_ab = [jnp.where(strict, over_heads(x[:n, :2 * L]), 0.0) for x in ar]
        a_ak = [jnp.where(strict, over_heads(x[:n, 2 * L:]), 0.0).astype(BF16) for x in ar]
        a_rb = [jnp.where(incl, over_heads(x[n:, :2 * L]), 0.0).astype(BF16) for x in ar]
        a_rk = [jnp.where(incl, over_heads(x[n:, 2 * L:]), 0.0).astype(BF16) for x in ar]
        yield
        inv = [eye + m for m in a_ab]
        pw = [m.astype(BF16) for m in a_ab]
        for _ in range(int(math.log2(L)) - 1):
            pw = [_dot(m, m).astype(BF16) for m in pw]
            inv = [t_ + _dot(t_.astype(BF16), m) for t_, m in zip(inv, pw)]
            yield
        inv = [t_.astype(BF16) for t_ in inv]
        akv = [_dot(m, s_[4]).astype(BF16) for m, s_ in zip(a_ak, stk)]
        wu = [_dot(t_, jnp.concatenate([s_[0], x], axis=1)) for t_, s_, x in zip(inv, stk, akv)]
        w_m = [x[:, :W].astype(BF16) for x in wu]
        u2 = [x[:, W:].astype(BF16) for x in wu]
        yield
        r_eff = [(s_[1].astype(F32) + _dot(m, w_)).astype(BF16) for s_, m, w_ in zip(stk, a_rb, w_m)]
        y_own = [_dot(m, u_) + _dot(n_, s_[4]) for m, u_, n_, s_ in zip(a_rb, u2, a_rk, stk)]
        yield
        st_mix = [_dot_tn(w_, s_[2]).astype(BF16) for w_, s_ in zip(w_m, stk)]
        st_own = [_dot_tn(u_, s_[2]) + _dot_tn(s_[4], s_[3]) for u_, s_ in zip(u2, stk)]
        return r_eff, y_own, st_mix, st_own

    def recurrence(st, sl, r_eff, y_own, st_mix, st_own):
        st_b = st.astype(BF16)
        y_s = _dot_nt(r_eff, st_b) + y_own
        y_c = y_s[0:L]
        for h in range(1, N_HEADS):
            y_c = y_c + y_s[h * L:(h + 1) * L]
        y_ref[sl, :] = y_c
        return (st + _dot(st_b, st_mix) + st_own) * e_in[sl.stop - 1:sl.stop, :]

    yield
    all_chunks = [slice(c * L, (c + 1) * L) for c in range(tc // L)]
    group = max(1, len(all_chunks) // RWKV_CHUNK_GROUPS)
    groups = [all_chunks[g0:g0 + group] for g0 in range(0, len(all_chunks), group)]
    st = st_ref[...]
    ready = yield from state_free(groups[0])
    for gi, chunks in enumerate(groups):
        ahead = state_free(groups[gi + 1]) if gi + 1 < len(groups) else iter(())
        nxt = None
        for c, sl in enumerate(chunks):
            st = recurrence(st, sl, *(part[c] for part in ready))
            for _ in range(RWKV_STAGES_PER_STEP):
                try:
                    next(ahead)
                except StopIteration as stop:
                    nxt = stop.value if stop.value is not None else nxt
            yield
        while gi + 1 < len(groups) and nxt is None:
            try:
                next(ahead)
                yield
            except StopIteration as stop:
                nxt = stop.value
        ready = nxt
    st_ref[...] = st

    y = y_ref[...]
    inv_n = 1.0 / HEAD_DIM
    mean = _dot_split(y, hsum, 2) * inv_n
    yc = y - mean
    var = _dot_split(yc * yc, hsum, 2) * inv_n
    y = yc * lax.rsqrt(var + RWKV_LN_EPS) * lnw_ref[...] + lnb_ref[...]
    bonus = _dot_split(r * k * rk_ref[...], hsum, 2) * v
    o_ref[...] = (y + bonus) * g


def _mlstm_kernel(z_ref, cw_ref, cb_ref, gb_ref, ng_ref, o_ref, tail_ref, c_ref, n_ref, m_ref, h_ref, *, tc):
    L = MLSTM_CHUNK
    W = MIX_W
    DK = MLSTM_DK
    DV = HEAD_DIM

    qk_in = z_ref[:, 0:W]
    v = z_ref[:, W:2 * W]
    og = z_ref[:, 2 * W:3 * W]
    gates = z_ref[:, 3 * W:3 * W + LANES]

    tail = tail_ref[...]
    conv = cb_ref[...] + cw_ref[MLSTM_CONV - 1:MLSTM_CONV, :] * qk_in
    for d in range(1, MLSTM_CONV):
        conv = conv + cw_ref[MLSTM_CONV - 1 - d:MLSTM_CONV - d, :] * _shift_rows(qk_in, tail, d)
    tail_ref[...] = qk_in[tc - SUBLANES:, :]
    qk = conv * _sigmoid(conv)
    q = qk[:, 0:N_HEADS * DK] * (DK ** -0.5)
    k = qk[:, N_HEADS * DK:]

    capped = GATE_SOFTCAP * jnp.tanh((gates + gb_ref[...]) / GATE_SOFTCAP)
    lf = _log_sigmoid(capped)

    gate_col = lax.broadcasted_iota(jnp.int32, (LANES, W), 0)
    lane_head = _idiv(lax.broadcasted_iota(jnp.int32, (LANES, W), 1), DV)
    pick_i = (gate_col == lane_head).astype(BF16)
    pick_f = (gate_col == N_HEADS + lane_head).astype(BF16)
    li_e = _dot_split(capped, pick_i, 3)
    span = min(tc, MXU_COLS)
    rt = lax.broadcasted_iota(jnp.int32, (span, span), 0)
    ct = lax.broadcasted_iota(jnp.int32, (span, span), 1)
    tri = ((_idiv(rt, L) == _idiv(ct, L)) & (ct <= rt)).astype(BF16)
    lf_e = [_dot(piece, pick_f).astype(BF16) for piece in _split_bf16(lf, 3)]
    b_e = jnp.concatenate([sum(_dot(tri, piece[r0:r0 + span]) for piece in lf_e)
                           for r0 in range(0, tc, span)], axis=0)

    key = lax.broadcasted_iota(jnp.int32, (L, W), 0)
    query = _imod(lax.broadcasted_iota(jnp.int32, (L, W), 1), L)
    on_diag = key == query
    causal_t = key <= query
    head_ones = _head_ones(W, DV)
    wide = lambda m: jnp.concatenate([m, m], axis=1)
    wide_lane_head = wide(lax.broadcasted_iota(jnp.int32, (1, W), 1) // DV)
    same_head_k = wide(_idiv(lax.broadcasted_iota(jnp.int32, (N_HEADS * DK, W), 0), DK)
                       == _idiv(lax.broadcasted_iota(jnp.int32, (N_HEADS * DK, W), 1), DV))
    q_lane_head = lax.broadcasted_iota(jnp.int32, (1, N_HEADS * DK), 1) // DK
    ones_b = jnp.ones((L, W), BF16)

    c_st = c_ref[...]
    n_st = n_ref[...]
    m_st = m_ref[0:1, :]
    for c in range(tc // L):
        yield
        sl = slice(c * L, (c + 1) * L)
        bc, lic, vc = b_e[sl], li_e[sl], v[sl]
        qc = q[sl]
        kc = k[sl].astype(BF16)
        q_stack = jnp.concatenate([jnp.where(q_lane_head == h, qc, 0.0) for h in range(N_HEADS)],
                                  axis=0).astype(BF16)
        b_q = jnp.sum(jnp.where(on_diag, bc, 0.0), axis=0, keepdims=True)
        dmat = jnp.where(causal_t, b_q - bc + lic, -jnp.inf)
        m_t = jnp.maximum(b_q + m_st, jnp.max(dmat, axis=0, keepdims=True))
        s_t = (_dot_nt(kc, q_stack) * jnp.exp(dmat - m_t)).astype(BF16)
        m_e = sum(_dot(jnp.where(on_diag, piece.astype(F32), 0.0).astype(BF16), head_ones)
                  for piece in _split_bf16(m_t, 3))
        inter = jnp.exp(bc + m_st - m_e)
        nv = _dot_tn(s_t, jnp.concatenate([vc.astype(BF16), ones_b], axis=1))
        nv = functools.reduce(lambda lo, h: jnp.where(wide_lane_head == h, nv[h * L:(h + 1) * L], lo),
                              range(1, N_HEADS), nv[0:L])
        qcn = _dot(qc.astype(BF16), jnp.concatenate([c_st, n_st], axis=1).astype(BF16))
        num = nv[:, :W] + inter * qcn[:, :W]
        den = nv[:, W:] + inter * qcn[:, W:]
        h_ref[sl, :] = num / jnp.maximum(jnp.abs(den), jnp.exp(-m_e))

        b_last = bc[L - 1:L, :]
        gexp = b_last - bc + lic
        m_new = jnp.maximum(b_last + m_st, jnp.max(gexp, axis=0, keepdims=True))
        wts = jnp.exp(gexp - m_new)
        dec = jnp.exp(b_last + m_st - m_new)
        upd = _dot_tn(kc, jnp.concatenate([wts * vc, wts], axis=1).astype(BF16))
        upd = jnp.where(same_head_k, upd, 0.0)
        c_st = dec * c_st + upd[:, :W]
        n_st = dec * n_st + upd[:, W:]
        m_st = m_new
    c_ref[...] = c_st
    n_ref[...] = n_st
    m_ref[0:1, :] = m_st
    yield

    hv = h_ref[...]
    ms = _dot_split(hv * hv, _head_ones(W, DV), 2) * (1.0 / DV)
    o_ref[...] = hv * lax.rsqrt(ms + NORM_EPS) * ng_ref[...] * _sigmoid(og)


def _t5_bucket(dist):
    max_exact = REL_BUCKETS // 2
    d = np.maximum(dist, 1).astype(np.float32)
    large = max_exact + (np.log(d / max_exact) / math.log(REL_MAX_DIST / max_exact)
                         * (REL_BUCKETS - max_exact)).astype(np.int32)
    large = np.minimum(large, REL_BUCKETS - 1)
    return np.where(dist < max_exact, dist, large).astype(np.int32)


def _swa_bias_table(rb_ref, bucket_ref, bias_ref):
    blk = ATTN_BLOCK
    grp = N_HEADS // SWA_KV_HEADS
    bucket = bucket_ref[...]
    for h in range(N_HEADS):
        acc = jnp.full((2 * blk, blk), -jnp.inf, F32)
        for bk in range(REL_BUCKETS):
            acc = jnp.where(bucket == bk, rb_ref[bk, h], acc)
        bias_ref[h // grp, :, (h % grp) * blk:(h % grp + 1) * blk] = acc


def _swa_kernel(sink_ref, q_ref, kp_ref, kc_ref, vp_ref, vc_ref, o_ref, bias_ref, *, layer):
    blk = ATTN_BLOCK
    grp = N_HEADS // SWA_KV_HEADS
    kvw = SWA_KV_HEADS * HEAD_DIM
    key = lax.broadcasted_iota(jnp.int32, (2 * blk, grp * blk), 0)
    live = (key >= blk) | (pl.program_id(1) > 0)
    member = lax.broadcasted_iota(jnp.int32, (1, grp * blk), 1) // blk
    kw = jnp.concatenate([kp_ref[...], kc_ref[...]], axis=0).astype(BF16)
    vw = jnp.concatenate([vp_ref[...], vc_ref[...]], axis=0).astype(BF16)
    lane_member = lax.broadcasted_iota(jnp.int32, (1, grp * HEAD_DIM), 1) // HEAD_DIM
    rr = lax.broadcasted_iota(jnp.int32, (kvw, grp * HEAD_DIM), 0)
    cc = lax.broadcasted_iota(jnp.int32, (kvw, grp * HEAD_DIM), 1)
    vr = lax.broadcasted_iota(jnp.int32, (HEAD_DIM, kvw), 0)
    vc_ = lax.broadcasted_iota(jnp.int32, (HEAD_DIM, kvw), 1)
    n_sub = q_ref.shape[0] // blk
    k_rep, v_t, sinks = [], [], []
    for j in range(SWA_KV_HEADS):
        spread = ((_idiv(rr, HEAD_DIM) == j) & (_imod(rr, HEAD_DIM) == _imod(cc, HEAD_DIM))).astype(BF16)
        pick = ((_idiv(vc_, HEAD_DIM) == j) & (_imod(vc_, HEAD_DIM) == vr)).astype(BF16)
        k_rep.append(_dot(kw, spread).astype(BF16))
        v_t.append(_dot_nt(pick, vw).astype(BF16))
        sinks.append(jnp.where(member == 0, sink_ref[layer, j * grp], sink_ref[layer, j * grp + 1]))
    pairs = [(n, j) for n in range(n_sub) for j in range(SWA_KV_HEADS)]
    yield
    scores = []
    for n, j in pairs:
        qp = q_ref[n * blk:(n + 1) * blk, j * grp * HEAD_DIM:(j + 1) * grp * HEAD_DIM] * (HEAD_DIM ** -0.5)
        q_stack = jnp.concatenate([jnp.where(lane_member == g, qp, 0.0) for g in range(grp)],
                                  axis=0).astype(BF16)
        s = _dot_nt(k_rep[j][n * blk:(n + 2) * blk], q_stack) + bias_ref[j]
        scores.append(jnp.where(live, s, -jnp.inf) if n == 0 else s)
    yield
    probs = []
    for (n, j), s in zip(pairs, scores):
        m = jnp.maximum(jnp.max(s, axis=0, keepdims=True), sinks[j])
        p = jnp.exp(s - m)
        denom = jnp.sum(p, axis=0, keepdims=True) + jnp.exp(sinks[j] - m)
        probs.append((p / denom).astype(BF16))
    yield
    outs = [_dot(v_t[j][:, n * blk:(n + 2) * blk], p) for (n, j), p in zip(pairs, probs)]
    for n in range(n_sub):
        heads = [outs[n * SWA_KV_HEADS + j][:, g * blk:(g + 1) * blk]
                 for j in range(SWA_KV_HEADS) for g in range(grp)]
        o_ref[n * blk:(n + 1) * blk, :] = jnp.concatenate(heads, axis=0).T


N_RWKV_PARAMS = 11
N_MLSTM_PARAMS = 4
N_RWKV_SCRATCH = 3
N_MLSTM_SCRATCH = 5


def _trace_interleaved(staged):
    done = object()
    staged = list(staged)
    while staged:
        for item in list(staged):
            body, stride = item
            for _ in range(stride):
                if next(body, done) is done:
                    staged.remove(item)
                    break


def _local_mixers_kernel(*refs, tc, layer):
    it = iter(refs)
    take = lambda n: [next(it) for _ in range(n)]
    (za_ref,), rwkv_p = take(1), take(N_RWKV_PARAMS)
    (zb_ref,), mlstm_p = take(1), take(N_MLSTM_PARAMS)
    rb_ref, sink_ref, bucket_ref = take(3)
    swa_in = take(5)
    ya_ref, yb_ref, yc_ref = take(3)
    rwkv_s, mlstm_s = take(N_RWKV_SCRATCH), take(N_MLSTM_SCRATCH)
    (bias_ref,) = take(1)

    @pl.when((pl.program_id(0) == 0) & (pl.program_id(1) == 0))
    def _():
        _swa_bias_table(rb_ref, bucket_ref, bias_ref)

    @pl.when(pl.program_id(1) == 0)
    def _():
        for ref in rwkv_s[:2] + mlstm_s[:4]:
            ref[...] = jnp.zeros_like(ref)

    _trace_interleaved([
        (_rwkv_kernel(za_ref, *rwkv_p, ya_ref, *rwkv_s, tc=tc), 2),
        (_mlstm_kernel(zb_ref, *mlstm_p, yb_ref, *mlstm_s, tc=tc), 1),
        (_swa_kernel(sink_ref, *swa_in, yc_ref, bias_ref, layer=layer), 1),
    ])


def _local_mixers(za, zb, zc, pr, pm, rel_bias, sinks, layer, bsz, t, tc):
    blk = ATTN_BLOCK
    assert tc % blk == 0 and t % tc == 0 and N_HEADS // SWA_KV_HEADS == 2
    nt = t // tc
    full = lambda a: pl.BlockSpec(a.shape, lambda b, i: (0,) * a.ndim)
    tile = lambda w, c=0: pl.BlockSpec((tc, w), lambda b, i: (b * nt + i, c))
    rwkv_p = [pr["mu"], pr["w0"], pr["w_up"], pr["a0"], pr["a_up"], pr["g_up"], pr["k_k"], pr["k_a"], pr["r_k"],
              pr["ln_w"], pr["ln_b"]]
    mlstm_p = [pm["conv_w"], pm["conv_b"], pm["gate_b"], pm["norm_g"]]
    assert len(rwkv_p) == N_RWKV_PARAMS and len(mlstm_p) == N_MLSTM_PARAMS
    dist = np.arange(blk)[None, :] + blk - np.arange(2 * blk)[:, None]
    bucket = jnp.asarray(np.where((dist >= 0) & (dist < SWA_WINDOW),
                                  _t5_bucket(np.clip(dist, 0, SWA_WINDOW - 1)), -1).astype(np.int32))
    kvw = SWA_KV_HEADS * HEAD_DIM
    kcol = MIX_W // kvw
    sub = tc // blk
    before = lambda c: pl.BlockSpec((blk, kvw), lambda b, i: (b * nt * sub + jnp.maximum(i * sub - 1, 0), c))
    smem = pl.BlockSpec(memory_space=pltpu.SMEM)
    return pl.pallas_call(
        functools.partial(_local_mixers_kernel, tc=tc, layer=layer),
        grid=(bsz, nt),
        in_specs=[tile(ZA_W)] + [_lspec(a, layer) for a in rwkv_p] + [tile(ZB_W)] + [_lspec(a, layer) for a in mlstm_p]
        + [smem, smem, full(bucket), tile(MIX_W), before(kcol), tile(kvw, kcol), before(kcol + 1), tile(kvw, kcol + 1)],
        out_specs=[tile(MIX_W)] * 3,
        out_shape=[jax.ShapeDtypeStruct((bsz * t, MIX_W), F32)] * 3,
        scratch_shapes=[
            pltpu.VMEM((SUBLANES, ZA_W), F32),
            pltpu.VMEM((MIX_W, MIX_W), F32),
            pltpu.VMEM((tc, MIX_W), F32),
            pltpu.VMEM((SUBLANES, MIX_W), F32),
            pltpu.VMEM((N_HEADS * MLSTM_DK, MIX_W), F32),
            pltpu.VMEM((N_HEADS * MLSTM_DK, MIX_W), F32),
            pltpu.VMEM((SUBLANES, MIX_W), F32),
            pltpu.VMEM((tc, MIX_W), F32),
            pltpu.VMEM((SWA_KV_HEADS, 2 * blk, 2 * blk), F32),
        ],
        compiler_params=pltpu.CompilerParams(dimension_semantics=("arbitrary", "arbitrary")),
        name="local_mixers",
    )(za, *rwkv_p, zb, *mlstm_p, rel_bias, sinks, bucket, zc, zc, zc, zc, zc)


FOX_BLOCK = 256
FOX_AUG = LANES
FOX_FEAT = 80
FOX_PARTS = 3


def _fox_placements():
    wide = N_HEADS * FOX_AUG
    pk = np.zeros((MIX_W, wide), np.float32)
    pck = np.zeros((LANES, wide), np.float32)
    ones_k = np.zeros((SUBLANES, wide), np.float32)
    tall = N_HEADS * FOX_FEAT
    pq_t = np.zeros((tall, MIX_W), np.float32)
    pv_t = np.zeros((tall, MIX_W), np.float32)
    pcq_t = np.zeros((tall, LANES), np.float32)
    ones_t = np.zeros((2, tall, LANES), np.float32)
    assert HEAD_DIM + 2 * FOX_PARTS <= FOX_FEAT <= FOX_AUG
    for h in range(N_HEADS):
        base = h * FOX_AUG
        base_t = h * FOX_FEAT
        for d in range(HEAD_DIM):
            pk[h * HEAD_DIM + d, base + d] = 1.0
            pq_t[base_t + d, h * HEAD_DIM + d] = HEAD_DIM ** -0.5
            pv_t[base_t + d, h * HEAD_DIM + d] = 1.0
        for n in range(FOX_PARTS):
            pcq_t[base_t + HEAD_DIM + n, n * N_HEADS + h] = 1.0
            pck[n * N_HEADS + h, base + HEAD_DIM + FOX_PARTS + n] = -1.0
            ones_t[0, base_t + HEAD_DIM + FOX_PARTS + n, :] = 1.0
            ones_k[0, base + HEAD_DIM + n] = 1.0
        ones_t[1, base_t + HEAD_DIM, :] = 1.0
    bf = lambda a: jnp.asarray(a, BF16)
    return bf(pk), bf(pck), jnp.asarray(ones_k), bf(np.stack([pq_t, pv_t])), bf(pcq_t), jnp.asarray(ones_t)


def _fox_kernel(q_ref, k_ref, v_ref, f_ref, fb_ref, pk_ref, pck_ref, onesk_ref, pqv_ref, pcq_ref, onest_ref,
                o_ref, kaug_ref, vaug_ref, m_ref, acc_ref, clast_ref, s_ref, s2_ref, p_ref):
    blk = FOX_BLOCK
    i = pl.program_id(1)

    @pl.when(i == 0)
    def _():
        clast_ref[...] = jnp.zeros_like(clast_ref)

    ls = _log_sigmoid(f_ref[...] + fb_ref[...])
    row = lax.broadcasted_iota(jnp.int32, (blk, blk), 0)
    col = lax.broadcasted_iota(jnp.int32, (blk, blk), 1)
    cq = _split_dot((col <= row).astype(BF16), ls, FOX_PARTS) + clast_ref[0:1, :]
    clast_ref[0:1, :] = cq[blk - 1:blk, :]

    widen = lambda a: jnp.concatenate([a] * (blk // LANES), axis=1)
    qa = _dot_nt(pqv_ref[0], q_ref[...].astype(BF16)) + widen(onest_ref[0])
    va = _dot_nt(pqv_ref[1], v_ref[...].astype(BF16)) + widen(onest_ref[1])
    ka = _dot(k_ref[...].astype(BF16), pk_ref[...]) + onesk_ref[0:1, :]
    lane = lax.broadcasted_iota(jnp.int32, (1, LANES), 1)
    pieces = _split_bf16(jnp.where(lane < N_HEADS, cq, 0.0), FOX_PARTS)
    packed = pieces[0].astype(F32)
    for n in range(1, FOX_PARTS):
        packed = packed + pltpu.roll(pieces[n].astype(F32), n * N_HEADS, axis=1)
    packed = packed.astype(BF16)
    qa = (qa + _dot_nt(pcq_ref[...], packed)).astype(BF16)
    ka = ka + _dot(packed, pck_ref[...])
    row0 = pl.multiple_of(i * blk, blk)
    for h in range(N_HEADS):
        kaug_ref[h, pl.ds(row0, blk), :] = ka[:, h * FOX_AUG:(h + 1) * FOX_AUG].astype(BF16)
        vaug_ref[h, :, pl.ds(row0, blk)] = va[h * FOX_FEAT:(h + 1) * FOX_FEAT, :].astype(BF16)

    key_le_query = row <= col

    def scores(j, buf, diagonal=False):
        off = pl.multiple_of(j * blk, blk)
        for h in range(N_HEADS):
            s = _dot(kaug_ref[h, pl.ds(off, blk), 0:FOX_FEAT], qa[h * FOX_FEAT:(h + 1) * FOX_FEAT, :])
            buf[h] = jnp.where(key_le_query, s, -jnp.inf) if diagonal else s

    def consume(j, buf):
        off = pl.multiple_of(j * blk, blk)
        m_new = [jnp.maximum(m_ref[h], jnp.max(buf[h], axis=0, keepdims=True)) for h in range(N_HEADS)]
        for h in range(N_HEADS):
            p_ref[h] = jnp.exp(buf[h] - m_new[h]).astype(BF16)
        for h in range(N_HEADS):
            pv = _dot(vaug_ref[h, :, pl.ds(off, blk)], p_ref[h])
            acc_ref[h] = jnp.exp(m_ref[h] - m_new[h]) * acc_ref[h] + pv
            m_ref[h] = m_new[h]

    m_ref[...] = jnp.full_like(m_ref, -jnp.inf)
    acc_ref[...] = jnp.zeros_like(acc_ref)
    scores(i, s_ref, diagonal=True)

    def body(j, carry):
        scores(2 * j, s2_ref)
        consume(jnp.where(j == 0, i, 2 * j - 1), s_ref)
        scores(jnp.minimum(2 * j + 1, i - 1), s_ref)
        consume(2 * j, s2_ref)
        return carry

    lax.fori_loop(0, (i + 1) // 2, body, 0)

    @pl.when(i % 2 == 0)
    def _():
        consume(jnp.maximum(i - 1, 0), s_ref)

    outs = []
    for h in range(N_HEADS):
        acc = acc_ref[h]
        outs.append(acc[0:HEAD_DIM, :] / acc[HEAD_DIM:HEAD_DIM + 1, :])
    o_ref[...] = jnp.concatenate(outs, axis=0).T


def _fox(zd, fb, layer, bsz, t):
    blk = FOX_BLOCK
    assert t % blk == 0
    nb = t // blk
    consts = _fox_placements()
    const = lambda a: pl.BlockSpec(a.shape, lambda b, i: (0,) * a.ndim)
    cur = lambda c: (lambda b, i: (b * nb + i, c))
    return pl.pallas_call(
        _fox_kernel,
        grid=(bsz, nb),
        in_specs=[
            pl.BlockSpec((blk, MIX_W), cur(0)),
            pl.BlockSpec((blk, MIX_W), cur(1)),
            pl.BlockSpec((blk, MIX_W), cur(2)),
            pl.BlockSpec((blk, LANES), cur(3 * MIX_W // LANES)),
            _lspec(fb, layer)] + [const(a) for a in consts],
        out_specs=pl.BlockSpec((blk, MIX_W), cur(0)),
        out_shape=jax.ShapeDtypeStruct((bsz * t, MIX_W), F32),
        scratch_shapes=[
            pltpu.VMEM((N_HEADS, t, FOX_AUG), BF16),
            pltpu.VMEM((N_HEADS, FOX_FEAT, t), BF16),
            pltpu.VMEM((N_HEADS, 1, blk), F32),
            pltpu.VMEM((N_HEADS, FOX_FEAT, blk), F32),
            pltpu.VMEM((SUBLANES, LANES), F32),
            pltpu.VMEM((N_HEADS, blk, blk), F32),
            pltpu.VMEM((N_HEADS, blk, blk), F32),
            pltpu.VMEM((N_HEADS, blk, blk), BF16),
        ],
        compiler_params=pltpu.CompilerParams(dimension_semantics=("arbitrary", "arbitrary")),
        name="fox",
    )(zd, zd, zd, zd, fb, *consts)


FFN_COL_CHUNK = 6 * MXU_COLS
FFN_ROW_PARTS = 2


def _outproj_ffn_kernel(x_ref, ya_ref, yb_ref, yc_ref, yd_ref, wo_ref, gmix_ref, gpre_ref, wg_ref, wu_ref, cw_ref,
                        cb_ref, wd_ref, gpost_ref, o_ref, tail_ref, *, tm, t):
    seq_start = (pl.program_id(0) * tm) % t == 0

    @pl.when(pl.program_id(0) == 0)
    def _():
        tail_ref[...] = jnp.zeros_like(tail_ref)

    rows = tm // FFN_ROW_PARTS
    gate_tails = {}

    def row_part(pi):
        rs = slice(pi * rows, (pi + 1) * rows)
        mixed = None
        for n, ref in enumerate((ya_ref, yb_ref, yc_ref, yd_ref)):
            part = _dot(ref[rs, :].astype(BF16), wo_ref[n * MIX_W:(n + 1) * MIX_W, :])
            mixed = part if mixed is None else mixed + part
        x = x_ref[rs, :] + mixed * _rms_scale(mixed) * gmix_ref[...]
        h = (x * _rms_scale(x) * gpre_ref[...]).astype(BF16)
        yield
        acc = None
        for c0 in range(0, D_FF, FFN_COL_CHUNK):
            cs = slice(c0, min(c0 + FFN_COL_CHUNK, D_FF))
            gate = _dot(h, wg_ref[:, cs])
            halo = jnp.where(seq_start, 0.0, tail_ref[:, cs]) if pi == 0 else gate_tails[pi - 1, c0]
            gate_tails[pi, c0] = gate[rows - SUBLANES:, :]
            if pi == FFN_ROW_PARTS - 1:
                tail_ref[:, cs] = gate_tails[pi, c0]
            conv = cb_ref[:, cs] + cw_ref[FFN_CONV - 1:FFN_CONV, cs] * gate
            for d in range(1, FFN_CONV):
                conv = conv + cw_ref[FFN_CONV - 1 - d:FFN_CONV - d, cs] * _shift_rows(gate, halo, d)
            f = jax.nn.gelu(conv, approximate=True) * _dot(h, wu_ref[:, cs])
            part = _dot(f.astype(BF16), wd_ref[cs, :])
            acc = part if acc is None else acc + part
            yield
        o_ref[rs, :] = x + acc * _rms_scale(acc) * gpost_ref[...]

    _trace_interleaved([(row_part(pi), 1) for pi in range(FFN_ROW_PARTS)])


def _outproj_ffn(x2, ys, w_out, g_mix, p, layer, t, tm):
    rows = x2.shape[0]
    once = dict(pipeline_mode=pl.Buffered(1))
    gate_up = p["w_gate_up"]
    params = [w_out, g_mix, p["g_pre"], gate_up, gate_up, p["conv_w"], p["conv_b"], p["w_down"], p["g_post"]]
    specs = [_lspec(a, layer, **once) for a in params]
    specs[3] = _lspec(gate_up, layer, block=(D_MODEL, D_FF), at=(0, 0), **once)
    specs[4] = _lspec(gate_up, layer, block=(D_MODEL, D_FF), at=(0, 1), **once)
    return pl.pallas_call(
        functools.partial(_outproj_ffn_kernel, tm=tm, t=t),
        grid=(rows // tm,),
        in_specs=[pl.BlockSpec((tm, D_MODEL), lambda i: (i, 0))]
        + [pl.BlockSpec((tm, MIX_W), lambda i: (i, 0))] * len(ys)
        + specs,
        out_specs=pl.BlockSpec((tm, D_MODEL), lambda i: (i, 0)),
        out_shape=jax.ShapeDtypeStruct((rows, D_MODEL), F32),
        scratch_shapes=[pltpu.VMEM((SUBLANES, D_FF), F32)],
        compiler_params=pltpu.CompilerParams(dimension_semantics=("arbitrary",),
                                             vmem_limit_bytes=56 * 1024 * 1024),
        name="outproj_ffn",
    )(x2, *ys, *params)


def _in_weights(w_in, l):
    wa_w = ZA_W
    wb_w = 3 * MIX_W + 2 * N_HEADS
    wc_w = ZC_W
    wi = jnp.transpose(w_in, (2, 0, 1))[:, l, :].astype(BF16)
    pad_rows = lambda a, n: jnp.pad(a, ((0, n - a.shape[0]), (0, 0)))
    groups = [wi[:wa_w],
              pad_rows(wi[wa_w:wa_w + wb_w], ZB_W),
              wi[wa_w + wb_w:wa_w + wb_w + wc_w],
              pad_rows(wi[wa_w + wb_w + wc_w:], ZD_W)]
    return jnp.concatenate(groups, axis=0)


def _rows(a):
    return a.reshape(a.shape[0], 1, -1).astype(F32)


def _lspec(a, layer, block=None, at=None, **kw):
    block = tuple(a.shape[1:]) if block is None else block
    at = (0,) * len(block) if at is None else at
    return pl.BlockSpec((None,) + block, lambda *_: (layer,) + at, **kw)


def _stacked_params(w_out, norm_mix_pre, norm_mix_post, norm_ffn_pre, norm_ffn_post,
                    rwkv_mu, rwkv_w0, rwkv_w_up, rwkv_a0, rwkv_a_up, rwkv_g_up, rwkv_k_k, rwkv_k_a,
                    rwkv_r_k, rwkv_ln_w, rwkv_ln_b, mlstm_conv_w, mlstm_conv_b, mlstm_b_i, mlstm_b_f,
                    mlstm_norm, swa_sinks, fox_b_f, ffn_w_up, ffn_conv_w, ffn_conv_b, ffn_w_down):
    pad_lanes = lambda a: jnp.pad(a, ((0, 0), (0, 0), (0, LANES - a.shape[-1])))
    return {
        "g_mix_pre": _rows(norm_mix_pre),
        "g_mix_post": _rows(norm_mix_post),
        "w_out": w_out.astype(BF16),
        "rwkv": {"mu": _rows(rwkv_mu), "w0": _rows(rwkv_w0), "w_up": rwkv_w_up.astype(BF16),
                 "a0": _rows(rwkv_a0), "a_up": rwkv_a_up.astype(BF16), "g_up": rwkv_g_up.astype(BF16),
                 "k_k": _rows(rwkv_k_k), "k_a": _rows(rwkv_k_a), "r_k": _rows(rwkv_r_k),
                 "ln_w": _rows(rwkv_ln_w), "ln_b": _rows(rwkv_ln_b)},
        "mlstm": {"conv_w": mlstm_conv_w, "conv_b": _rows(mlstm_conv_b),
                  "gate_b": pad_lanes(_rows(jnp.concatenate([mlstm_b_i, mlstm_b_f], axis=-1))),
                  "norm_g": _rows(mlstm_norm)},
        "swa_sinks": swa_sinks,
        "fox_b": pad_lanes(_rows(fox_b_f)),
        "ffn": {"g_pre": _rows(norm_ffn_pre), "w_gate_up": ffn_w_up.astype(BF16), "conv_w": ffn_conv_w,
                "conv_b": _rows(ffn_conv_b), "w_down": ffn_w_down.astype(BF16), "g_post": _rows(norm_ffn_post)},
    }


def _tiles(t):
    return min(512, t), min(512, t), min(512, t)


def kernel(x, w_in, w_out, norm_mix_pre, norm_mix_post, norm_ffn_pre, norm_ffn_post, rwkv_mu, rwkv_w0, rwkv_w_up, rwkv_a0, rwkv_a_up, rwkv_g_up, rwkv_k_k, rwkv_k_a, rwkv_r_k, rwkv_ln_w, rwkv_ln_b, mlstm_conv_w, mlstm_conv_b, mlstm_b_i, mlstm_b_f, mlstm_norm, swa_sinks, fox_b_f, rel_bias, ffn_w_up, ffn_conv_w, ffn_conv_b, ffn_w_down):
    bsz, t, d = x.shape
    assert d == D_MODEL and t % ATTN_BLOCK == 0
    tm, tc, tf = _tiles(t)
    x2 = x.reshape(bsz * t, d)
    p = _stacked_params(w_out, norm_mix_pre, norm_mix_post, norm_ffn_pre, norm_ffn_post,
                        rwkv_mu, rwkv_w0, rwkv_w_up, rwkv_a0, rwkv_a_up, rwkv_g_up, rwkv_k_k, rwkv_k_a,
                        rwkv_r_k, rwkv_ln_w, rwkv_ln_b, mlstm_conv_w, mlstm_conv_b, mlstm_b_i, mlstm_b_f,
                        mlstm_norm, swa_sinks, fox_b_f, ffn_w_up, ffn_conv_w, ffn_conv_b, ffn_w_down)
    for l in range(w_in.shape[0]):
        za, zb, zc, zd = _inproj(x2, p["g_mix_pre"], _in_weights(w_in, l), l, tm)
        ya, yb, yc = _local_mixers(za, zb, zc, p["rwkv"], p["mlstm"], rel_bias, p["swa_sinks"], l, bsz, t, tc)
        yd = _fox(zd, p["fox_b"], l, bsz, t)
        x2 = _outproj_ffn(x2, (ya, yb, yc, yd), p["w_out"], p["g_mix_post"], p["ffn"], l, t, tf)
    return x2.reshape(bsz, t, d)
```

```python
import functools
import math

import jax
import jax.numpy as jnp
import numpy as np
from jax import lax
from jax.experimental import pallas as pl
from jax.experimental.pallas import tpu as pltpu

F32 = jnp.float32
BF16 = jnp.bfloat16

D_MODEL = 1024
HEAD_DIM = 64
N_HEADS = 4
MIX_W = N_HEADS * HEAD_DIM
RWKV_DECAY_RANK = 64
RWKV_AAA_RANK = 64
RWKV_GATE_RANK = 128
RWKV_LN_EPS = 64e-5
RWKV_CHUNK = 64
MLSTM_DK = 32
MLSTM_CONV = 4
MLSTM_CHUNK = 64
GATE_SOFTCAP = 15.0
SWA_KV_HEADS = 2
SWA_WINDOW = 128
ATTN_BLOCK = 128
REL_BUCKETS = 32
REL_MAX_DIST = 128
D_FF = 2816
FFN_CONV = 3
NORM_EPS = 1e-6

LANES = 128
SUBLANES = 8
MXU_COLS = 256

ZA_W = 3 * MIX_W + RWKV_DECAY_RANK + RWKV_AAA_RANK + RWKV_GATE_RANK
ZB_W = 3 * MIX_W + LANES
ZC_W = MIX_W + 2 * SWA_KV_HEADS * HEAD_DIM
ZD_W = 3 * MIX_W + LANES
Z_W = ZA_W + ZB_W + ZC_W + ZD_W

def _split_bf16(x, parts):
    out = []
    for n in range(parts):
        piece = x.astype(BF16)
        out.append(piece)
        if n + 1 < parts:
            x = x - piece.astype(F32)
    return out


def _dot(a, b, precision=None):
    return jnp.dot(a, b, preferred_element_type=F32, precision=precision)


def _dot_nt(a, b, precision=None):
    return lax.dot_general(a, b, (((1,), (1,)), ((), ())), preferred_element_type=F32, precision=precision)


def _dot_tn(a, b, precision=None):
    return lax.dot_general(a, b, (((0,), (0,)), ((), ())), preferred_element_type=F32, precision=precision)


def _dot_split(x, ones, parts):
    return sum(_dot(piece, ones) for piece in _split_bf16(x, parts))


def _split_dot(ones, x, parts):
    return sum(_dot(ones, piece) for piece in _split_bf16(x, parts))


def _bdot(a, b):
    return _dot(a.astype(BF16), b.astype(BF16))


def _bdot_nt(a, b):
    return _dot_nt(a.astype(BF16), b.astype(BF16))


def _bdot_tn(a, b):
    return _dot_tn(a.astype(BF16), b.astype(BF16))


def _sigmoid(x):
    return 1.0 / (1.0 + jnp.exp(-x))


def _log_sigmoid(x):
    return jnp.minimum(x, 0.0) - jnp.log(1.0 + jnp.exp(-jnp.abs(x)))


def _softplus(x):
    return jnp.maximum(x, 0.0) + jnp.log(1.0 + jnp.exp(-jnp.abs(x)))


def _rms_scale(x):
    return lax.rsqrt(jnp.mean(x * x, axis=-1, keepdims=True) + NORM_EPS)


def _shift_rows(x, prev_tail, d):
    rolled = pltpu.roll(x, d, axis=0)
    head_rows = lax.broadcasted_iota(jnp.int32, (SUBLANES, 1), 0)
    head = jnp.where(head_rows < d, pltpu.roll(prev_tail, d, axis=0), rolled[:SUBLANES])
    return jnp.concatenate([head, rolled[SUBLANES:]], axis=0)


def _idiv(x, n):
    assert n & (n - 1) == 0
    return lax.shift_right_logical(x, jnp.int32(n.bit_length() - 1))


def _imod(x, n):
    assert n & (n - 1) == 0
    return lax.bitwise_and(x, jnp.int32(n - 1))


def _head_ones(width, head):
    r = _idiv(lax.broadcasted_iota(jnp.int32, (width, width), 0), head)
    c = _idiv(lax.broadcasted_iota(jnp.int32, (width, width), 1), head)
    return (r == c).astype(BF16)


def _inproj_kernel(x_ref, g_ref, w_ref, za_ref, zb_ref, zc_ref, zd_ref):
    x = x_ref[...]
    h = (x * _rms_scale(x) * g_ref[...]).astype(BF16)
    z = _dot_nt(h, w_ref[...])
    off = 0
    for ref in (za_ref, zb_ref, zc_ref, zd_ref):
        w = ref.shape[1]
        ref[...] = z[:, off:off + w]
        off += w


def _inproj(x2, g, w, layer, tm):
    rows = x2.shape[0]
    return pl.pallas_call(
        _inproj_kernel,
        grid=(rows // tm,),
        in_specs=[
            pl.BlockSpec((tm, D_MODEL), lambda i: (i, 0)),
            _lspec(g, layer),
            pl.BlockSpec((Z_W, D_MODEL), lambda i: (0, 0)),
        ],
        out_specs=[pl.BlockSpec((tm, w_), lambda i: (i, 0)) for w_ in (ZA_W, ZB_W, ZC_W, ZD_W)],
        out_shape=[jax.ShapeDtypeStruct((rows, w_), F32) for w_ in (ZA_W, ZB_W, ZC_W, ZD_W)],
        compiler_params=pltpu.CompilerParams(dimension_semantics=("arbitrary",)),
        name="inproj",
    )(x2, g, w)


def _rwkv_kernel(z_ref, mu_ref, w0_ref, wup_ref, a0_ref, aup_ref, gup_ref, kk_ref, ka_ref, rk_ref,
                 lnw_ref, lnb_ref, o_ref, tail_ref, st_ref, y_ref, *, tc):
    L = RWKV_CHUNK
    W = MIX_W

    z = z_ref[...]
    zz = z + mu_ref[...] * (_shift_rows(z, tail_ref[...], 1) - z)
    tail_ref[...] = z[tc - SUBLANES:, :]

    r = zz[:, 0:W]
    k = zz[:, W:2 * W]
    v = zz[:, 2 * W:3 * W]
    o1 = 3 * W
    wd = zz[:, o1:o1 + RWKV_DECAY_RANK]
    ad = zz[:, o1 + RWKV_DECAY_RANK:o1 + RWKV_DECAY_RANK + RWKV_AAA_RANK]
    gd = zz[:, o1 + RWKV_DECAY_RANK + RWKV_AAA_RANK:]

    lw = -jnp.exp(-_softplus(-(w0_ref[...] + _bdot(jnp.tanh(wd), wup_ref[...]))) - 0.5)
    alpha = _sigmoid(a0_ref[...] + _bdot(ad, aup_ref[...]))
    g = _bdot(_sigmoid(gd), gup_ref[...])

    hsum = _head_ones(W, HEAD_DIM)
    kk = k * kk_ref[...]
    kk = kk * lax.rsqrt(jnp.maximum(_dot_split(kk * kk, hsum, 2), 1e-24))
    k = k * (1.0 + (alpha - 1.0) * ka_ref[...])

    span = min(tc, MXU_COLS)
    rt = lax.broadcasted_iota(jnp.int32, (span, span), 0)
    ct = lax.broadcasted_iota(jnp.int32, (span, span), 1)
    tri = ((_idiv(rt, L) == _idiv(ct, L)) & (ct <= rt)).astype(BF16)
    cum = jnp.concatenate([_split_dot(tri, lw[r0:r0 + span], 2) for r0 in range(0, tc, span)], axis=0)
    e_in = jnp.exp(cum)
    e_out = jnp.exp(-cum)
    r_t = r * e_in
    a_t = -kk * jnp.exp(cum - lw)
    b_t = kk * alpha * e_out
    k_t = k * e_out

    lane_head = lax.broadcasted_iota(jnp.int32, (1, W), 1) // HEAD_DIM

    def stack(xc):
        return jnp.concatenate([jnp.where(lane_head == h, xc, 0.0) for h in range(N_HEADS)], axis=0)

    n = N_HEADS * L
    rr = lax.broadcasted_iota(jnp.int32, (n, n), 0)
    cc = lax.broadcasted_iota(jnp.int32, (n, n), 1)
    own_head = _idiv(rr, L) == _idiv(cc, L)
    strict = own_head & (_imod(rr, L) > _imod(cc, L))
    incl = own_head & (_imod(rr, L) >= _imod(cc, L))
    eye = (rr == cc).astype(F32)

    over_heads = lambda x: jnp.concatenate([x, x], axis=1)

    def state_free(chunks):
        stk = [tuple(stack(u[sl]).astype(BF16) for u in (a_t, r_t, b_t, k_t, v)) for sl in chunks]
        bk_rep = [jnp.concatenate([b_t[sl]] * 2 + [k_t[sl]] * 2, axis=0).astype(BF16) for sl in chunks]
        ar = [_dot_nt(jnp.concatenate([s_[0], s_[1]], axis=0), x) for s_, x in zip(stk, bk_rep)]
        a_ab = [jnp.where(strict, over_heads(x[:n, :2 * L]), 0.0) for x in ar]
        a_ak = [jnp.where(strict, over_heads(x[:n, 2 * L:]), 0.0).astype(BF16) for x in ar]
        a_rb = [jnp.where(incl, over_heads(x[n:, :2 * L]), 0.0).astype(BF16) for x in ar]
        a_rk = [jnp.where(incl, over_heads(x[n:, 2 * L:]), 0.0).astype(BF16) for x in ar]
        yield
        inv = [eye + m for m in a_ab]
        pw = [m.astype(BF16) for m in a_ab]
        for _ in range(int(math.log2(L)) - 1):
            pw = [_dot(m, m).astype(BF16) for m in pw]
            inv = [t_ + _dot(t_.astype(BF16), m) for t_, m in zip(inv, pw)]
            yield
        inv = [t_.astype(BF16) for t_ in inv]
        akv = [_dot(m, s_[4]).astype(BF16) for m, s_ in zip(a_ak, stk)]
        wu = [_dot(t_, jnp.concatenate([s_[0], x], axis=1)) for t_, s_, x in zip(inv, stk, akv)]
        w_m = [x[:, :W].astype(BF16) for x in wu]
        u2 = [x[:, W:].astype(BF16) for x in wu]
        yield
        r_eff = [(s_[1].astype(F32) + _dot(m, w_)).astype(BF16) for s_, m, w_ in zip(stk, a_rb, w_m)]
        y_own = [_dot(m, u_) + _dot(n_, s_[4]) for m, u_, n_, s_ in zip(a_rb, u2, a_rk, stk)]
        yield
        st_mix = [_dot_tn(w_, s_[2]).astype(BF16) for w_, s_ in zip(w_m, stk)]
        st_own = [_dot_tn(u_, s_[2]) + _dot_tn(s_[4], s_[3]) for u_, s_ in zip(u2, stk)]
        return r_eff, y_own, st_mix, st_own

    def recurrence(st, sl, r_eff, y_own, st_mix, st_own):
        st_b = st.astype(BF16)
        y_s = _dot_nt(r_eff, st_b) + y_own
        y_c = y_s[0:L]
        for h in range(1, N_HEADS):
            y_c = y_c + y_s[h * L:(h + 1) * L]
        y_ref[sl, :] = y_c
        return (st + _dot(st_b, st_mix) + st_own) * e_in[sl.stop - 1:sl.stop, :]

    yield
    chunks = [slice(c * L, (c + 1) * L) for c in range(tc // L)]
    ready = yield from state_free(chunks)
    st = st_ref[...]
    for c, sl in enumerate(chunks):
        st = recurrence(st, sl, *(part[c] for part in ready))
        yield
    st_ref[...] = st

    y = y_ref[...]
    inv_n = 1.0 / HEAD_DIM
    mean = _dot_split(y, hsum, 1) * inv_n
    yc = y - mean
    var = _dot_split(yc * yc, hsum, 1) * inv_n
    y = yc * lax.rsqrt(var + RWKV_LN_EPS) * lnw_ref[...] + lnb_ref[...]
    bonus = _dot_split(r * k * rk_ref[...], hsum, 1) * v
    o_ref[...] = (y + bonus) * g


def _mlstm_kernel(z_ref, cw_ref, cb_ref, gb_ref, ng_ref, o_ref, tail_ref, c_ref, n_ref, m_ref, h_ref, *, tc):
    L = MLSTM_CHUNK
    W = MIX_W
    DK = MLSTM_DK
    DV = HEAD_DIM

    qk_in = z_ref[:, 0:W]
    v = z_ref[:, W:2 * W]
    og = z_ref[:, 2 * W:3 * W]
    gates = z_ref[:, 3 * W:3 * W + LANES]

    tail = tail_ref[...]
    conv = cb_ref[...] + cw_ref[MLSTM_CONV - 1:MLSTM_CONV, :] * qk_in
    for d in range(1, MLSTM_CONV):
        conv = conv + cw_ref[MLSTM_CONV - 1 - d:MLSTM_CONV - d, :] * _shift_rows(qk_in, tail, d)
    tail_ref[...] = qk_in[tc - SUBLANES:, :]
    qk = conv * _sigmoid(conv)
    q = qk[:, 0:N_HEADS * DK] * (DK ** -0.5)
    k = qk[:, N_HEADS * DK:]

    capped = GATE_SOFTCAP * jnp.tanh((gates + gb_ref[...]) / GATE_SOFTCAP)
    lf = _log_sigmoid(capped)

    gate_col = lax.broadcasted_iota(jnp.int32, (LANES, W), 0)
    lane_head = _idiv(lax.broadcasted_iota(jnp.int32, (LANES, W), 1), DV)
    pick_i = (gate_col == lane_head).astype(BF16)
    pick_f = (gate_col == N_HEADS + lane_head).astype(BF16)
    li_e = _dot_split(capped, pick_i, 3)
    span = min(tc, MXU_COLS)
    rt = lax.broadcasted_iota(jnp.int32, (span, span), 0)
    ct = lax.broadcasted_iota(jnp.int32, (span, span), 1)
    tri = ((_idiv(rt, L) == _idiv(ct, L)) & (ct <= rt)).astype(BF16)
    lf_e = [_dot(piece, pick_f).astype(BF16) for piece in _split_bf16(lf, 3)]
    b_e = jnp.concatenate([sum(_dot(tri, piece[r0:r0 + span]) for piece in lf_e)
                           for r0 in range(0, tc, span)], axis=0)

    key = lax.broadcasted_iota(jnp.int32, (L, W), 0)
    query = _imod(lax.broadcasted_iota(jnp.int32, (L, W), 1), L)
    on_diag = key == query
    causal_t = key <= query
    head_ones = _head_ones(W, DV)
    wide = lambda m: jnp.concatenate([m, m], axis=1)
    wide_lane_head = wide(lax.broadcasted_iota(jnp.int32, (1, W), 1) // DV)
    same_head_k = wide(_idiv(lax.broadcasted_iota(jnp.int32, (N_HEADS * DK, W), 0), DK)
                       == _idiv(lax.broadcasted_iota(jnp.int32, (N_HEADS * DK, W), 1), DV))
    q_lane_head = lax.broadcasted_iota(jnp.int32, (1, N_HEADS * DK), 1) // DK
    ones_b = jnp.ones((L, W), BF16)

    c_st = c_ref[...]
    n_st = n_ref[...]
    m_st = m_ref[0:1, :]
    for c in range(tc // L):
        yield
        sl = slice(c * L, (c + 1) * L)
        bc, lic, vc = b_e[sl], li_e[sl], v[sl]
        qc = q[sl]
        kc = k[sl].astype(BF16)
        q_stack = jnp.concatenate([jnp.where(q_lane_head == h, qc, 0.0) for h in range(N_HEADS)],
                                  axis=0).astype(BF16)
        b_q = jnp.sum(jnp.where(on_diag, bc, 0.0), axis=0, keepdims=True)
        dmat = jnp.where(causal_t, b_q - bc + lic, -jnp.inf)
        m_t = jnp.maximum(b_q + m_st, jnp.max(dmat, axis=0, keepdims=True))
        s_t = (_dot_nt(kc, q_stack) * jnp.exp(dmat - m_t)).astype(BF16)
        m_e = sum(_dot(jnp.where(on_diag, piece.astype(F32), 0.0).astype(BF16), head_ones)
                  for piece in _split_bf16(m_t, 3))
        inter = jnp.exp(bc + m_st - m_e)
        nv = _dot_tn(s_t, jnp.concatenate([vc.astype(BF16), ones_b], axis=1))
        nv = functools.reduce(lambda lo, h: jnp.where(wide_lane_head == h, nv[h * L:(h + 1) * L], lo),
                              range(1, N_HEADS), nv[0:L])
        qcn = _dot(qc.astype(BF16), jnp.concatenate([c_st, n_st], axis=1).astype(BF16))
        num = nv[:, :W] + inter * qcn[:, :W]
        den = nv[:, W:] + inter * qcn[:, W:]
        h_ref[sl, :] = num / jnp.maximum(jnp.abs(den), jnp.exp(-m_e))

        b_last = bc[L - 1:L, :]
        gexp = b_last - bc + lic
        m_new = jnp.maximum(b_last + m_st, jnp.max(gexp, axis=0, keepdims=True))
        wts = jnp.exp(gexp - m_new)
        dec = jnp.exp(b_last + m_st - m_new)
        upd = _dot_tn(kc, jnp.concatenate([wts * vc, wts], axis=1).astype(BF16))
        upd = jnp.where(same_head_k, upd, 0.0)
        c_st = dec * c_st + upd[:, :W]
        n_st = dec * n_st + upd[:, W:]
        m_st = m_new
    c_ref[...] = c_st
    n_ref[...] = n_st
    m_ref[0:1, :] = m_st
    yield

    hv = h_ref[...]
    ms = _dot_split(hv * hv, _head_ones(W, DV), 2) * (1.0 / DV)
    o_ref[...] = hv * lax.rsqrt(ms + NORM_EPS) * ng_ref[...] * _sigmoid(og)


def _t5_bucket(dist):
    max_exact = REL_BUCKETS // 2
    d = np.maximum(dist, 1).astype(np.float32)
    large = max_exact + (np.log(d / max_exact) / math.log(REL_MAX_DIST / max_exact)
                         * (REL_BUCKETS - max_exact)).astype(np.int32)
    large = np.minimum(large, REL_BUCKETS - 1)
    return np.where(dist < max_exact, dist, large).astype(np.int32)


def _swa_bias_table(rb_ref, bucket_ref, bias_ref):
    blk = ATTN_BLOCK
    grp = N_HEADS // SWA_KV_HEADS
    bucket = bucket_ref[...]
    for h in range(N_HEADS):
        acc = jnp.full((2 * blk, blk), -jnp.inf, F32)
        for bk in range(REL_BUCKETS):
            acc = jnp.where(bucket == bk, rb_ref[bk, h], acc)
        bias_ref[h // grp, :, (h % grp) * blk:(h % grp + 1) * blk] = acc


def _swa_kernel(sink_ref, q_ref, kp_ref, kc_ref, vp_ref, vc_ref, o_ref, bias_ref, *, layer):
    blk = ATTN_BLOCK
    grp = N_HEADS // SWA_KV_HEADS
    kvw = SWA_KV_HEADS * HEAD_DIM
    key = lax.broadcasted_iota(jnp.int32, (2 * blk, grp * blk), 0)
    live = (key >= blk) | (pl.program_id(1) > 0)
    member = lax.broadcasted_iota(jnp.int32, (1, grp * blk), 1) // blk
    kw = jnp.concatenate([kp_ref[...], kc_ref[...]], axis=0).astype(BF16)
    vw = jnp.concatenate([vp_ref[...], vc_ref[...]], axis=0).astype(BF16)
    lane_member = lax.broadcasted_iota(jnp.int32, (1, grp * HEAD_DIM), 1) // HEAD_DIM
    rr = lax.broadcasted_iota(jnp.int32, (kvw, grp * HEAD_DIM), 0)
    cc = lax.broadcasted_iota(jnp.int32, (kvw, grp * HEAD_DIM), 1)
    vr = lax.broadcasted_iota(jnp.int32, (HEAD_DIM, kvw), 0)
    vc_ = lax.broadcasted_iota(jnp.int32, (HEAD_DIM, kvw), 1)
    n_sub = q_ref.shape[0] // blk
    k_rep, v_t, sinks = [], [], []
    for j in range(SWA_KV_HEADS):
        spread = ((_idiv(rr, HEAD_DIM) == j) & (_imod(rr, HEAD_DIM) == _imod(cc, HEAD_DIM))).astype(BF16)
        pick = ((_idiv(vc_, HEAD_DIM) == j) & (_imod(vc_, HEAD_DIM) == vr)).astype(BF16)
        k_rep.append(_dot(kw, spread).astype(BF16))
        v_t.append(_dot_nt(pick, vw).astype(BF16))
        sinks.append(jnp.where(member == 0, sink_ref[layer, j * grp], sink_ref[layer, j * grp + 1]))
    pairs = [(n, j) for n in range(n_sub) for j in range(SWA_KV_HEADS)]
    yield
    scores = []
    for n, j in pairs:
        qp = q_ref[n * blk:(n + 1) * blk, j * grp * HEAD_DIM:(j + 1) * grp * HEAD_DIM] * (HEAD_DIM ** -0.5)
        q_stack = jnp.concatenate([jnp.where(lane_member == g, qp, 0.0) for g in range(grp)],
                                  axis=0).astype(BF16)
        s = _dot_nt(k_rep[j][n * blk:(n + 2) * blk], q_stack) + bias_ref[j]
        scores.append(jnp.where(live, s, -jnp.inf) if n == 0 else s)
    yield
    probs = []
    for (n, j), s in zip(pairs, scores):
        m = jnp.maximum(jnp.max(s, axis=0, keepdims=True), sinks[j])
        p = jnp.exp(s - m)
        denom = jnp.sum(p, axis=0, keepdims=True) + jnp.exp(sinks[j] - m)
        probs.append((p / denom).astype(BF16))
    yield
    outs = [_dot(v_t[j][:, n * blk:(n + 2) * blk], p) for (n, j), p in zip(pairs, probs)]
    for n in range(n_sub):
        heads = [outs[n * SWA_KV_HEADS + j][:, g * blk:(g + 1) * blk]
                 for j in range(SWA_KV_HEADS) for g in range(grp)]
        o_ref[n * blk:(n + 1) * blk, :] = jnp.concatenate(heads, axis=0).T


N_RWKV_PARAMS = 11
N_MLSTM_PARAMS = 4
N_RWKV_SCRATCH = 3
N_MLSTM_SCRATCH = 5


def _trace_interleaved(staged):
    done = object()
    staged = list(staged)
    while staged:
        for item in list(staged):
            body, stride = item
            for _ in range(stride):
                if next(body, done) is done:
                    staged.remove(item)
                    break


def _local_mixers_kernel(*refs, tc, layer):
    it = iter(refs)
    take = lambda n: [next(it) for _ in range(n)]
    (za_ref,), rwkv_p = take(1), take(N_RWKV_PARAMS)
    (zb_ref,), mlstm_p = take(1), take(N_MLSTM_PARAMS)
    rb_ref, sink_ref, bucket_ref = take(3)
    swa_in = take(5)
    ya_ref, yb_ref, yc_ref = take(3)
    rwkv_s, mlstm_s = take(N_RWKV_SCRATCH), take(N_MLSTM_SCRATCH)
    (bias_ref,) = take(1)

    @pl.when((pl.program_id(0) == 0) & (pl.program_id(1) == 0))
    def _():
        _swa_bias_table(rb_ref, bucket_ref, bias_ref)

    @pl.when(pl.program_id(1) == 0)
    def _():
        for ref in rwkv_s[:2] + mlstm_s[:4]:
            ref[...] = jnp.zeros_like(ref)

    _trace_interleaved([
        (_rwkv_kernel(za_ref, *rwkv_p, ya_ref, *rwkv_s, tc=tc), 2),
        (_mlstm_kernel(zb_ref, *mlstm_p, yb_ref, *mlstm_s, tc=tc), 1),
        (_swa_kernel(sink_ref, *swa_in, yc_ref, bias_ref, layer=layer), 1),
    ])


def _local_mixers(za, zb, zc, pr, pm, rel_bias, sinks, layer, bsz, t, tc):
    blk = ATTN_BLOCK
    assert tc % blk == 0 and t % tc == 0 and N_HEADS // SWA_KV_HEADS == 2
    nt = t // tc
    full = lambda a: pl.BlockSpec(a.shape, lambda b, i: (0,) * a.ndim)
    tile = lambda w, c=0: pl.BlockSpec((tc, w), lambda b, i: (b * nt + i, c))
    rwkv_p = [pr["mu"], pr["w0"], pr["w_up"], pr["a0"], pr["a_up"], pr["g_up"], pr["k_k"], pr["k_a"], pr["r_k"],
              pr["ln_w"], pr["ln_b"]]
    mlstm_p = [pm["conv_w"], pm["conv_b"], pm["gate_b"], pm["norm_g"]]
    assert len(rwkv_p) == N_RWKV_PARAMS and len(mlstm_p) == N_MLSTM_PARAMS
    dist = np.arange(blk)[None, :] + blk - np.arange(2 * blk)[:, None]
    bucket = jnp.asarray(np.where((dist >= 0) & (dist < SWA_WINDOW),
                                  _t5_bucket(np.clip(dist, 0, SWA_WINDOW - 1)), -1).astype(np.int32))
    kvw = SWA_KV_HEADS * HEAD_DIM
    kcol = MIX_W // kvw
    sub = tc // blk
    before = lambda c: pl.BlockSpec((blk, kvw), lambda b, i: (b * nt * sub + jnp.maximum(i * sub - 1, 0), c))
    smem = pl.BlockSpec(memory_space=pltpu.SMEM)
    return pl.pallas_call(
        functools.partial(_local_mixers_kernel, tc=tc, layer=layer),
        grid=(bsz, nt),
        in_specs=[tile(ZA_W)] + [_lspec(a, layer) for a in rwkv_p] + [tile(ZB_W)] + [_lspec(a, layer) for a in mlstm_p]
        + [smem, smem, full(bucket), tile(MIX_W), before(kcol), tile(kvw, kcol), before(kcol + 1), tile(kvw, kcol + 1)],
        out_specs=[tile(MIX_W)] * 3,
        out_shape=[jax.ShapeDtypeStruct((bsz * t, MIX_W), F32)] * 3,
        scratch_shapes=[
            pltpu.VMEM((SUBLANES, ZA_W), F32),
            pltpu.VMEM((MIX_W, MIX_W), F32),
            pltpu.VMEM((tc, MIX_W), F32),
            pltpu.VMEM((SUBLANES, MIX_W), F32),
            pltpu.VMEM((N_HEADS * MLSTM_DK, MIX_W), F32),
            pltpu.VMEM((N_HEADS * MLSTM_DK, MIX_W), F32),
            pltpu.VMEM((SUBLANES, MIX_W), F32),
            pltpu.VMEM((tc, MIX_W), F32),
            pltpu.VMEM((SWA_KV_HEADS, 2 * blk, 2 * blk), F32),
        ],
        compiler_params=pltpu.CompilerParams(dimension_semantics=("arbitrary", "arbitrary")),
        name="local_mixers",
    )(za, *rwkv_p, zb, *mlstm_p, rel_bias, sinks, bucket, zc, zc, zc, zc, zc)


FOX_BLOCK = 256
FOX_AUG = LANES
FOX_FEAT = 80
FOX_PARTS = 3


def _fox_placements():
    wide = N_HEADS * FOX_AUG
    pk = np.zeros((MIX_W, wide), np.float32)
    pck = np.zeros((LANES, wide), np.float32)
    ones_k = np.zeros((SUBLANES, wide), np.float32)
    tall = N_HEADS * FOX_FEAT
    pq_t = np.zeros((tall, MIX_W), np.float32)
    pv_t = np.zeros((tall, MIX_W), np.float32)
    pcq_t = np.zeros((tall, LANES), np.float32)
    ones_t = np.zeros((2, tall, LANES), np.float32)
    assert HEAD_DIM + 2 * FOX_PARTS <= FOX_FEAT <= FOX_AUG
    for h in range(N_HEADS):
        base = h * FOX_AUG
        base_t = h * FOX_FEAT
        for d in range(HEAD_DIM):
            pk[h * HEAD_DIM + d, base + d] = 1.0
            pq_t[base_t + d, h * HEAD_DIM + d] = HEAD_DIM ** -0.5
            pv_t[base_t + d, h * HEAD_DIM + d] = 1.0
        for n in range(FOX_PARTS):
            pcq_t[base_t + HEAD_DIM + n, n * N_HEADS + h] = 1.0
            pck[n * N_HEADS + h, base + HEAD_DIM + FOX_PARTS + n] = -1.0
            ones_t[0, base_t + HEAD_DIM + FOX_PARTS + n, :] = 1.0
            ones_k[0, base + HEAD_DIM + n] = 1.0
        ones_t[1, base_t + HEAD_DIM, :] = 1.0
    bf = lambda a: jnp.asarray(a, BF16)
    return bf(pk), bf(pck), jnp.asarray(ones_k), bf(np.stack([pq_t, pv_t])), bf(pcq_t), jnp.asarray(ones_t)


def _fox_kernel(q_ref, k_ref, v_ref, f_ref, fb_ref, pk_ref, pck_ref, onesk_ref, pqv_ref, pcq_ref, onest_ref,
                o_ref, kaug_ref, vaug_ref, m_ref, acc_ref, clast_ref, s_ref, s2_ref, p_ref):
    blk = FOX_BLOCK
    i = pl.program_id(1)

    @pl.when(i == 0)
    def _():
        clast_ref[...] = jnp.zeros_like(clast_ref)

    ls = _log_sigmoid(f_ref[...] + fb_ref[...])
    row = lax.broadcasted_iota(jnp.int32, (blk, blk), 0)
    col = lax.broadcasted_iota(jnp.int32, (blk, blk), 1)
    cq = _split_dot((col <= row).astype(BF16), ls, FOX_PARTS) + clast_ref[0:1, :]
    clast_ref[0:1, :] = cq[blk - 1:blk, :]

    widen = lambda a: jnp.concatenate([a] * (blk // LANES), axis=1)
    qa = _dot_nt(pqv_ref[0], q_ref[...].astype(BF16)) + widen(onest_ref[0])
    va = _dot_nt(pqv_ref[1], v_ref[...].astype(BF16)) + widen(onest_ref[1])
    ka = _dot(k_ref[...].astype(BF16), pk_ref[...]) + onesk_ref[0:1, :]
    lane = lax.broadcasted_iota(jnp.int32, (1, LANES), 1)
    pieces = _split_bf16(jnp.where(lane < N_HEADS, cq, 0.0), FOX_PARTS)
    packed = pieces[0].astype(F32)
    for n in range(1, FOX_PARTS):
        packed = packed + pltpu.roll(pieces[n].astype(F32), n * N_HEADS, axis=1)
    packed = packed.astype(BF16)
    qa = (qa + _dot_nt(pcq_ref[...], packed)).astype(BF16)
    ka = ka + _dot(packed, pck_ref[...])
    row0 = pl.multiple_of(i * blk, blk)
    for h in range(N_HEADS):
        kaug_ref[h, pl.ds(row0, blk), :] = ka[:, h * FOX_AUG:(h + 1) * FOX_AUG].astype(BF16)
        vaug_ref[h, :, pl.ds(row0, blk)] = va[h * FOX_FEAT:(h + 1) * FOX_FEAT, :].astype(BF16)

    key_le_query = row <= col

    def scores(j, buf, diagonal=False):
        off = pl.multiple_of(j * blk, blk)
        for h in range(N_HEADS):
            s = _dot(kaug_ref[h, pl.ds(off, blk), 0:FOX_FEAT], qa[h * FOX_FEAT:(h + 1) * FOX_FEAT, :])
            buf[h] = jnp.where(key_le_query, s, -jnp.inf) if diagonal else s

    def consume(j, buf):
        off = pl.multiple_of(j * blk, blk)
        m_new = [jnp.maximum(m_ref[h], jnp.max(buf[h], axis=0, keepdims=True)) for h in range(N_HEADS)]
        for h in range(N_HEADS):
            p_ref[h] = jnp.exp(buf[h] - m_new[h]).astype(BF16)
        for h in range(N_HEADS):
            pv = _dot(vaug_ref[h, :, pl.ds(off, blk)], p_ref[h])
            acc_ref[h] = jnp.exp(m_ref[h] - m_new[h]) * acc_ref[h] + pv
            m_ref[h] = m_new[h]

    m_ref[...] = jnp.full_like(m_ref, -jnp.inf)
    acc_ref[...] = jnp.zeros_like(acc_ref)
    scores(i, s_ref, diagonal=True)

    def body(j, carry):
        scores(2 * j, s2_ref)
        consume(jnp.where(j == 0, i, 2 * j - 1), s_ref)
        scores(jnp.minimum(2 * j + 1, i - 1), s_ref)
        consume(2 * j, s2_ref)
        return carry

    lax.fori_loop(0, (i + 1) // 2, body, 0)

    @pl.when(i % 2 == 0)
    def _():
        consume(jnp.maximum(i - 1, 0), s_ref)

    outs = []
    for h in range(N_HEADS):
        acc = acc_ref[h]
        outs.append(acc[0:HEAD_DIM, :] / acc[HEAD_DIM:HEAD_DIM + 1, :])
    o_ref[...] = jnp.concatenate(outs, axis=0).T


def _fox(zd, fb, layer, bsz, t):
    blk = FOX_BLOCK
    assert t % blk == 0
    nb = t // blk
    consts = _fox_placements()
    const = lambda a: pl.BlockSpec(a.shape, lambda b, i: (0,) * a.ndim)
    cur = lambda c: (lambda b, i: (b * nb + i, c))
    return pl.pallas_call(
        _fox_kernel,
        grid=(bsz, nb),
        in_specs=[
            pl.BlockSpec((blk, MIX_W), cur(0)),
            pl.BlockSpec((blk, MIX_W), cur(1)),
            pl.BlockSpec((blk, MIX_W), cur(2)),
            pl.BlockSpec((blk, LANES), cur(3 * MIX_W // LANES)),
            _lspec(fb, layer)] + [const(a) for a in consts],
        out_specs=pl.BlockSpec((blk, MIX_W), cur(0)),
        out_shape=jax.ShapeDtypeStruct((bsz * t, MIX_W), F32),
        scratch_shapes=[
            pltpu.VMEM((N_HEADS, t, FOX_AUG), BF16),
            pltpu.VMEM((N_HEADS, FOX_FEAT, t), BF16),
            pltpu.VMEM((N_HEADS, 1, blk), F32),
            pltpu.VMEM((N_HEADS, FOX_FEAT, blk), F32),
            pltpu.VMEM((SUBLANES, LANES), F32),
            pltpu.VMEM((N_HEADS, blk, blk), F32),
            pltpu.VMEM((N_HEADS, blk, blk), F32),
            pltpu.VMEM((N_HEADS, blk, blk), BF16),
        ],
        compiler_params=pltpu.CompilerParams(dimension_semantics=("arbitrary", "arbitrary")),
        name="fox",
    )(zd, zd, zd, zd, fb, *consts)


FFN_COL_CHUNK = 6 * MXU_COLS
FFN_ROW_PARTS = 2


def _outproj_ffn_kernel(x_ref, ya_ref, yb_ref, yc_ref, yd_ref, wo_ref, gmix_ref, gpre_ref, wg_ref, wu_ref, cw_ref,
                        cb_ref, wd_ref, gpost_ref, o_ref, tail_ref, *, tm, t):
    seq_start = (pl.program_id(0) * tm) % t == 0

    @pl.when(pl.program_id(0) == 0)
    def _():
        tail_ref[...] = jnp.zeros_like(tail_ref)

    rows = tm // FFN_ROW_PARTS
    gate_tails = {}

    def row_part(pi):
        rs = slice(pi * rows, (pi + 1) * rows)
        mixed = None
        for n, ref in enumerate((ya_ref, yb_ref, yc_ref, yd_ref)):
            part = _dot(ref[rs, :].astype(BF16), wo_ref[n * MIX_W:(n + 1) * MIX_W, :])
            mixed = part if mixed is None else mixed + part
        x = x_ref[rs, :] + mixed * _rms_scale(mixed) * gmix_ref[...]
        h = (x * _rms_scale(x) * gpre_ref[...]).astype(BF16)
        yield
        acc = None
        for c0 in range(0, D_FF, FFN_COL_CHUNK):
            cs = slice(c0, min(c0 + FFN_COL_CHUNK, D_FF))
            gate = _dot(h, wg_ref[:, cs])
            halo = jnp.where(seq_start, 0.0, tail_ref[:, cs]) if pi == 0 else gate_tails[pi - 1, c0]
            gate_tails[pi, c0] = gate[rows - SUBLANES:, :]
            if pi == FFN_ROW_PARTS - 1:
                tail_ref[:, cs] = gate_tails[pi, c0]
            conv = cb_ref[:, cs] + cw_ref[FFN_CONV - 1:FFN_CONV, cs] * gate
            for d in range(1, FFN_CONV):
                conv = conv + cw_ref[FFN_CONV - 1 - d:FFN_CONV - d, cs] * _shift_rows(gate, halo, d)
            f = jax.nn.gelu(conv, approximate=True) * _dot(h, wu_ref[:, cs])
            part = _dot(f.astype(BF16), wd_ref[cs, :])
            acc = part if acc is None else acc + part
            yield
        o_ref[rs, :] = x + acc * _rms_scale(acc) * gpost_ref[...]

    _trace_interleaved([(row_part(pi), 1) for pi in range(FFN_ROW_PARTS)])


def _outproj_ffn_vmem_bytes(tm):
    weights = 2 * (N_HEADS * MIX_W * D_MODEL + 2 * D_MODEL * D_FF + D_FF * D_MODEL)
    tiles = 2 * 4 * tm * (2 * D_MODEL + N_HEADS * MIX_W)
    temps = 4 * tm * (6 * FFN_COL_CHUNK + 3 * D_MODEL)
    return weights + tiles + temps


def _outproj_ffn(x2, ys, w_out, g_mix, p, layer, t, tm):
    rows = x2.shape[0]
    once = dict(pipeline_mode=pl.Buffered(1))
    gate_up = p["w_gate_up"]
    params = [w_out, g_mix, p["g_pre"], gate_up, gate_up, p["conv_w"], p["conv_b"], p["w_down"], p["g_post"]]
    specs = [_lspec(a, layer, **once) for a in params]
    specs[3] = _lspec(gate_up, layer, block=(D_MODEL, D_FF), at=(0, 0), **once)
    specs[4] = _lspec(gate_up, layer, block=(D_MODEL, D_FF), at=(0, 1), **once)
    return pl.pallas_call(
        functools.partial(_outproj_ffn_kernel, tm=tm, t=t),
        grid=(rows // tm,),
        in_specs=[pl.BlockSpec((tm, D_MODEL), lambda i: (i, 0))]
        + [pl.BlockSpec((tm, MIX_W), lambda i: (i, 0))] * len(ys)
        + specs,
        out_specs=pl.BlockSpec((tm, D_MODEL), lambda i: (i, 0)),
        out_shape=jax.ShapeDtypeStruct((rows, D_MODEL), F32),
        scratch_shapes=[pltpu.VMEM((SUBLANES, D_FF), F32)],
        compiler_params=pltpu.CompilerParams(dimension_semantics=("arbitrary",),
                                             vmem_limit_bytes=_outproj_ffn_vmem_bytes(tm)),
        name="outproj_ffn",
    )(x2, *ys, *params)


def _in_weights(w_in, l):
    wa_w = ZA_W
    wb_w = 3 * MIX_W + 2 * N_HEADS
    wc_w = ZC_W
    wi = jnp.transpose(w_in, (2, 0, 1))[:, l, :].astype(BF16)
    pad_rows = lambda a, n: jnp.pad(a, ((0, n - a.shape[0]), (0, 0)))
    groups = [wi[:wa_w],
              pad_rows(wi[wa_w:wa_w + wb_w], ZB_W),
              wi[wa_w + wb_w:wa_w + wb_w + wc_w],
              pad_rows(wi[wa_w + wb_w + wc_w:], ZD_W)]
    return jnp.concatenate(groups, axis=0)


def _rows(a):
    return a.reshape(a.shape[0], 1, -1).astype(F32)


def _lspec(a, layer, block=None, at=None, **kw):
    block = tuple(a.shape[1:]) if block is None else block
    at = (0,) * len(block) if at is None else at
    return pl.BlockSpec((None,) + block, lambda *_: (layer,) + at, **kw)


def _stacked_params(w_out, norm_mix_pre, norm_mix_post, norm_ffn_pre, norm_ffn_post,
                    rwkv_mu, rwkv_w0, rwkv_w_up, rwkv_a0, rwkv_a_up, rwkv_g_up, rwkv_k_k, rwkv_k_a,
                    rwkv_r_k, rwkv_ln_w, rwkv_ln_b, mlstm_conv_w, mlstm_conv_b, mlstm_b_i, mlstm_b_f,
                    mlstm_norm, swa_sinks, fox_b_f, ffn_w_up, ffn_conv_w, ffn_conv_b, ffn_w_down):
    pad_lanes = lambda a: jnp.pad(a, ((0, 0), (0, 0), (0, LANES - a.shape[-1])))
    return {
        "g_mix_pre": _rows(norm_mix_pre),
        "g_mix_post": _rows(norm_mix_post),
        "w_out": w_out.astype(BF16),
        "rwkv": {"mu": _rows(rwkv_mu), "w0": _rows(rwkv_w0), "w_up": rwkv_w_up.astype(BF16),
                 "a0": _rows(rwkv_a0), "a_up": rwkv_a_up.astype(BF16), "g_up": rwkv_g_up.astype(BF16),
                 "k_k": _rows(rwkv_k_k), "k_a": _rows(rwkv_k_a), "r_k": _rows(rwkv_r_k),
                 "ln_w": _rows(rwkv_ln_w), "ln_b": _rows(rwkv_ln_b)},
        "mlstm": {"conv_w": mlstm_conv_w, "conv_b": _rows(mlstm_conv_b),
                  "gate_b": pad_lanes(_rows(jnp.concatenate([mlstm_b_i, mlstm_b_f], axis=-1))),
                  "norm_g": _rows(mlstm_norm)},
        "swa_sinks": swa_sinks,
        "fox_b": pad_lanes(_rows(fox_b_f)),
        "ffn": {"g_pre": _rows(norm_ffn_pre), "w_gate_up": ffn_w_up.astype(BF16), "conv_w": ffn_conv_w,
                "conv_b": _rows(ffn_conv_b), "w_down": ffn_w_down.astype(BF16), "g_post": _rows(norm_ffn_post)},
    }


def _tiles(t):
    return min(512, t), min(512, t), min(512, t)


def kernel(x, w_in, w_out, norm_mix_pre, norm_mix_post, norm_ffn_pre, norm_ffn_post, rwkv_mu, rwkv_w0, rwkv_w_up, rwkv_a0, rwkv_a_up, rwkv_g_up, rwkv_k_k, rwkv_k_a, rwkv_r_k, rwkv_ln_w, rwkv_ln_b, mlstm_conv_w, mlstm_conv_b, mlstm_b_i, mlstm_b_f, mlstm_norm, swa_sinks, fox_b_f, rel_bias, ffn_w_up, ffn_conv_w, ffn_conv_b, ffn_w_down):
    bsz, t, d = x.shape
    assert d == D_MODEL and t % ATTN_BLOCK == 0
    tm, tc, tf = _tiles(t)
    x2 = x.reshape(bsz * t, d)
    p = _stacked_params(w_out, norm_mix_pre, norm_mix_post, norm_ffn_pre, norm_ffn_post,
                        rwkv_mu, rwkv_w0, rwkv_w_up, rwkv_a0, rwkv_a_up, rwkv_g_up, rwkv_k_k, rwkv_k_a,
                        rwkv_r_k, rwkv_ln_w, rwkv_ln_b, mlstm_conv_w, mlstm_conv_b, mlstm_b_i, mlstm_b_f,
                        mlstm_norm, swa_sinks, fox_b_f, ffn_w_up, ffn_conv_w, ffn_conv_b, ffn_w_down)
    for l in range(w_in.shape[0]):
        za, zb, zc, zd = _inproj(x2, p["g_mix_pre"], _in_weights(w_in, l), l, tm)
        ya, yb, yc = _local_mixers(za, zb, zc, p["rwkv"], p["mlstm"], rel_bias, p["swa_sinks"], l, bsz, t, tc)
        yd = _fox(zd, p["fox_b"], l, bsz, t)
        x2 = _outproj_ffn(x2, (ya, yb, yc, yd), p["w_out"], p["g_mix_post"], p["ffn"], l, t, tf)
    return x2.reshape(bsz, t, d)
```

```python
import functools
import math

import jax
import jax.numpy as jnp
import numpy as np
from jax import lax
from jax.experimental import pallas as pl
from jax.experimental.pallas import tpu as pltpu

F32 = jnp.float32
BF16 = jnp.bfloat16

D_MODEL = 1024
HEAD_DIM = 64
N_HEADS = 4
MIX_W = N_HEADS * HEAD_DIM
RWKV_DECAY_RANK = 64
RWKV_AAA_RANK = 64
RWKV_GATE_RANK = 128
RWKV_LN_EPS = 64e-5
RWKV_CHUNK = 64
MLSTM_DK = 32
MLSTM_CONV = 4
MLSTM_CHUNK = 64
GATE_SOFTCAP = 15.0
SWA_KV_HEADS = 2
SWA_WINDOW = 128
ATTN_BLOCK = 128
REL_BUCKETS = 32
REL_MAX_DIST = 128
D_FF = 2816
FFN_CONV = 3
NORM_EPS = 1e-6

LANES = 128
SUBLANES = 8
MXU_COLS = 256

ZA_W = 3 * MIX_W + RWKV_DECAY_RANK + RWKV_AAA_RANK + RWKV_GATE_RANK
ZB_W = 3 * MIX_W + LANES
ZC_W = MIX_W + 2 * SWA_KV_HEADS * HEAD_DIM
ZD_W = 3 * MIX_W + LANES
Z_W = ZA_W + ZB_W + ZC_W + ZD_W


def _split_bf16(x, parts):
    out = []
    for n in range(parts):
        piece = x.astype(BF16)
        out.append(piece)
        if n + 1 < parts:
            x = x - piece.astype(F32)
    return out


def _dot(a, b):
    return jnp.dot(a, b, preferred_element_type=F32)


def _dot_nt(a, b):
    return lax.dot_general(a, b, (((1,), (1,)), ((), ())), preferred_element_type=F32)


def _dot_tn(a, b):
    return lax.dot_general(a, b, (((0,), (0,)), ((), ())), preferred_element_type=F32)


def _dot_split(x, ones, parts):
    return sum(_dot(piece, ones) for piece in _split_bf16(x, parts))


def _split_dot(ones, x, parts):
    return sum(_dot(ones, piece) for piece in _split_bf16(x, parts))


def _bdot(a, b):
    return _dot(a.astype(BF16), b.astype(BF16))


def _sigmoid(x):
    return 1.0 / (1.0 + jnp.exp(-x))


def _log_sigmoid(x):
    return jnp.minimum(x, 0.0) - jnp.log(1.0 + jnp.exp(-jnp.abs(x)))


def _softplus(x):
    return jnp.maximum(x, 0.0) + jnp.log(1.0 + jnp.exp(-jnp.abs(x)))


def _rms_scale(x):
    return lax.rsqrt(jnp.mean(x * x, axis=-1, keepdims=True) + NORM_EPS)


def _shift_rows(x, prev_tail, d):
    rolled = pltpu.roll(x, d, axis=0)
    head_rows = lax.broadcasted_iota(jnp.int32, (SUBLANES, 1), 0)
    head = jnp.where(head_rows < d, pltpu.roll(prev_tail, d, axis=0), rolled[:SUBLANES])
    return jnp.concatenate([head, rolled[SUBLANES:]], axis=0)


def _idiv(x, n):
    assert n & (n - 1) == 0
    return lax.shift_right_logical(x, jnp.int32(n.bit_length() - 1))


def _imod(x, n):
    assert n & (n - 1) == 0
    return lax.bitwise_and(x, jnp.int32(n - 1))


def _head_ones(width, head):
    r = _idiv(lax.broadcasted_iota(jnp.int32, (width, width), 0), head)
    c = _idiv(lax.broadcasted_iota(jnp.int32, (width, width), 1), head)
    return (r == c).astype(BF16)


INPROJ_ROW_PARTS = 2


def _inproj_kernel(x_ref, g_ref, w_ref, za_ref, zb_ref, zc_ref, zd_ref):
    rows = x_ref.shape[0] // INPROJ_ROW_PARTS
    for pi in range(INPROJ_ROW_PARTS):
        rs = slice(pi * rows, (pi + 1) * rows)
        x = x_ref[rs, :]
        h = (x * _rms_scale(x) * g_ref[...]).astype(BF16)
        z = _dot_nt(h, w_ref[...])
        off = 0
        for ref in (za_ref, zb_ref, zc_ref, zd_ref):
            w = ref.shape[1]
            ref[rs, :] = z[:, off:off + w]
            off += w


def _inproj(x2, g, w, layer, tm):
    rows = x2.shape[0]
    return pl.pallas_call(
        _inproj_kernel,
        grid=(rows // tm,),
        in_specs=[
            pl.BlockSpec((tm, D_MODEL), lambda i: (i, 0)),
            _lspec(g, layer),
            pl.BlockSpec((Z_W, D_MODEL), lambda i: (0, 0)),
        ],
        out_specs=[pl.BlockSpec((tm, w_), lambda i: (i, 0)) for w_ in (ZA_W, ZB_W, ZC_W, ZD_W)],
        out_shape=[jax.ShapeDtypeStruct((rows, w_), F32) for w_ in (ZA_W, ZB_W, ZC_W, ZD_W)],
        compiler_params=pltpu.CompilerParams(dimension_semantics=("arbitrary",)),
        name="inproj",
    )(x2, g, w)


def _rwkv_kernel(z_ref, mu_ref, w0_ref, wup_ref, a0_ref, aup_ref, gup_ref, kk_ref, ka_ref, rk_ref,
                 lnw_ref, lnb_ref, o_ref, tail_ref, st_ref, y_ref, *, tc):
    L = RWKV_CHUNK
    W = MIX_W

    z = z_ref[...]
    zz = z + mu_ref[...] * (_shift_rows(z, tail_ref[...], 1) - z)
    tail_ref[...] = z[tc - SUBLANES:, :]

    r = zz[:, 0:W]
    k = zz[:, W:2 * W]
    v = zz[:, 2 * W:3 * W]
    o1 = 3 * W
    wd = zz[:, o1:o1 + RWKV_DECAY_RANK]
    ad = zz[:, o1 + RWKV_DECAY_RANK:o1 + RWKV_DECAY_RANK + RWKV_AAA_RANK]
    gd = zz[:, o1 + RWKV_DECAY_RANK + RWKV_AAA_RANK:]

    lw = -jnp.exp(-_softplus(-(w0_ref[...] + _bdot(jnp.tanh(wd), wup_ref[...]))) - 0.5)
    alpha = _sigmoid(a0_ref[...] + _bdot(ad, aup_ref[...]))
    g = _bdot(_sigmoid(gd), gup_ref[...])

    hsum = _head_ones(W, HEAD_DIM)
    kk = k * kk_ref[...]
    kk = kk * lax.rsqrt(jnp.maximum(_dot_split(kk * kk, hsum, 2), 1e-24))
    k = k * (1.0 + (alpha - 1.0) * ka_ref[...])

    span = min(tc, MXU_COLS)
    rt = lax.broadcasted_iota(jnp.int32, (span, span), 0)
    ct = lax.broadcasted_iota(jnp.int32, (span, span), 1)
    tri = ((_idiv(rt, L) == _idiv(ct, L)) & (ct <= rt)).astype(BF16)
    cum = jnp.concatenate([_split_dot(tri, lw[r0:r0 + span], 2) for r0 in range(0, tc, span)], axis=0)
    e_in = jnp.exp(cum)
    e_out = jnp.exp(-cum)
    r_t = r * e_in
    a_t = -kk * jnp.exp(cum - lw)
    b_t = kk * alpha * e_out
    k_t = k * e_out

    lane_head = lax.broadcasted_iota(jnp.int32, (1, W), 1) // HEAD_DIM

    def stack(xc):
        return jnp.concatenate([jnp.where(lane_head == h, xc, 0.0) for h in range(N_HEADS)], axis=0)

    n = N_HEADS * L
    rr = lax.broadcasted_iota(jnp.int32, (n, n), 0)
    cc = lax.broadcasted_iota(jnp.int32, (n, n), 1)
    own_head = _idiv(rr, L) == _idiv(cc, L)
    strict = own_head & (_imod(rr, L) > _imod(cc, L))
    incl = own_head & (_imod(rr, L) >= _imod(cc, L))
    eye = (rr == cc).astype(F32)

    over_heads = lambda x: jnp.concatenate([x, x], axis=1)

    def state_free(chunks):
        stk = [tuple(stack(u[sl]).astype(BF16) for u in (a_t, r_t, b_t, k_t, v)) for sl in chunks]
        bk_rep = [jnp.concatenate([b_t[sl]] * 2 + [k_t[sl]] * 2, axis=0).astype(BF16) for sl in chunks]
        ar = [_dot_nt(jnp.concatenate([s_[0], s_[1]], axis=0), x) for s_, x in zip(stk, bk_rep)]
        a_ab = [jnp.where(strict, over_heads(x[:n, :2 * L]), 0.0) for x in ar]
        a_ak = [jnp.where(strict, over_heads(x[:n, 2 * L:]), 0.0).astype(BF16) for x in ar]
        a_rb = [jnp.where(incl, over_heads(x[n:, :2 * L]), 0.0).astype(BF16) for x in ar]
        a_rk = [jnp.where(incl, over_heads(x[n:, 2 * L:]), 0.0).astype(BF16) for x in ar]
        yield
        inv = [eye + m for m in a_ab]
        pw = [m.astype(BF16) for m in a_ab]
        for _ in range(int(math.log2(L)) - 1):
            pw = [_dot(m, m).astype(BF16) for m in pw]
            inv = [t_ + _dot(t_.astype(BF16), m) for t_, m in zip(inv, pw)]
            yield
        inv = [t_.astype(BF16) for t_ in inv]
        akv = [_dot(m, s_[4]).astype(BF16) for m, s_ in zip(a_ak, stk)]
        wu = [_dot(t_, jnp.concatenate([s_[0], x], axis=1)) for t_, s_, x in zip(inv, stk, akv)]
        w_m = [x[:, :W].astype(BF16) for x in wu]
        u2 = [x[:, W:].astype(BF16) for x in wu]
        yield
        r_eff = [(s_[1].astype(F32) + _dot(m, w_)).astype(BF16) for s_, m, w_ in zip(stk, a_rb, w_m)]
        y_own = [_dot(m, u_) + _dot(n_, s_[4]) for m, u_, n_, s_ in zip(a_rb, u2, a_rk, stk)]
        yield
        st_mix = [_dot_tn(w_, s_[2]).astype(BF16) for w_, s_ in zip(w_m, stk)]
        st_own = [_dot_tn(u_, s_[2]) + _dot_tn(s_[4], s_[3]) for u_, s_ in zip(u2, stk)]
        return r_eff, y_own, st_mix, st_own

    def recurrence(st, sl, r_eff, y_own, st_mix, st_own):
        st_b = st.astype(BF16)
        y_s = _dot_nt(r_eff, st_b) + y_own
        y_c = y_s[0:L]
        for h in range(1, N_HEADS):
            y_c = y_c + y_s[h * L:(h + 1) * L]
        y_ref[sl, :] = y_c
        return (st + _dot(st_b, st_mix) + st_own) * e_in[sl.stop - 1:sl.stop, :]

    yield
    chunks = [slice(c * L, (c + 1) * L) for c in range(tc // L)]
    ready = yield from state_free(chunks)
    st = st_ref[...]
    for c, sl in enumerate(chunks):
        st = recurrence(st, sl, *(part[c] for part in ready))
        yield
    st_ref[...] = st

    y = y_ref[...]
    inv_n = 1.0 / HEAD_DIM
    mean = _dot_split(y, hsum, 1) * inv_n
    yc = y - mean
    var = _dot_split(yc * yc, hsum, 1) * inv_n
    y = yc * lax.rsqrt(var + RWKV_LN_EPS) * lnw_ref[...] + lnb_ref[...]
    bonus = _dot_split(r * k * rk_ref[...], hsum, 1) * v
    o_ref[...] = (y + bonus) * g


def _mlstm_kernel(z_ref, cw_ref, cb_ref, gb_ref, ng_ref, o_ref, tail_ref, c_ref, n_ref, m_ref, h_ref, *, tc):
    L = MLSTM_CHUNK
    W = MIX_W
    DK = MLSTM_DK
    DV = HEAD_DIM

    qk_in = z_ref[:, 0:W]
    v = z_ref[:, W:2 * W]
    og = z_ref[:, 2 * W:3 * W]
    gates = z_ref[:, 3 * W:3 * W + LANES]

    tail = tail_ref[...]
    conv = cb_ref[...] + cw_ref[MLSTM_CONV - 1:MLSTM_CONV, :] * qk_in
    for d in range(1, MLSTM_CONV):
        conv = conv + cw_ref[MLSTM_CONV - 1 - d:MLSTM_CONV - d, :] * _shift_rows(qk_in, tail, d)
    tail_ref[...] = qk_in[tc - SUBLANES:, :]
    qk = conv * _sigmoid(conv)
    q = qk[:, 0:N_HEADS * DK] * (DK ** -0.5)
    k = qk[:, N_HEADS * DK:]

    capped = GATE_SOFTCAP * jnp.tanh((gates + gb_ref[...]) / GATE_SOFTCAP)
    lf = _log_sigmoid(capped)

    gate_col = lax.broadcasted_iota(jnp.int32, (LANES, W), 0)
    lane_head = _idiv(lax.broadcasted_iota(jnp.int32, (LANES, W), 1), DV)
    pick_i = (gate_col == lane_head).astype(BF16)
    pick_f = (gate_col == N_HEADS + lane_head).astype(BF16)
    li_e = _dot_split(capped, pick_i, 3)
    span = min(tc, MXU_COLS)
    rt = lax.broadcasted_iota(jnp.int32, (span, span), 0)
    ct = lax.broadcasted_iota(jnp.int32, (span, span), 1)
    tri = ((_idiv(rt, L) == _idiv(ct, L)) & (ct <= rt)).astype(BF16)
    lf_e = [_dot(piece, pick_f).astype(BF16) for piece in _split_bf16(lf, 3)]
    b_e = jnp.concatenate([sum(_dot(tri, piece[r0:r0 + span]) for piece in lf_e)
                           for r0 in range(0, tc, span)], axis=0)

    key = lax.broadcasted_iota(jnp.int32, (L, W), 0)
    query = _imod(lax.broadcasted_iota(jnp.int32, (L, W), 1), L)
    on_diag = key == query
    causal_t = key <= query
    head_ones = _head_ones(W, DV)
    wide = lambda m: jnp.concatenate([m, m], axis=1)
    wide_lane_head = wide(lax.broadcasted_iota(jnp.int32, (1, W), 1) // DV)
    same_head_k = wide(_idiv(lax.broadcasted_iota(jnp.int32, (N_HEADS * DK, W), 0), DK)
                       == _idiv(lax.broadcasted_iota(jnp.int32, (N_HEADS * DK, W), 1), DV))
    q_lane_head = lax.broadcasted_iota(jnp.int32, (1, N_HEADS * DK), 1) // DK
    ones_b = jnp.ones((L, W), BF16)

    c_st = c_ref[...]
    n_st = n_ref[...]
    m_st = m_ref[0:1, :]
    for c in range(tc // L):
        yield
        sl = slice(c * L, (c + 1) * L)
        bc, lic, vc = b_e[sl], li_e[sl], v[sl]
        qc = q[sl]
        kc = k[sl].astype(BF16)
        q_stack = jnp.concatenate([jnp.where(q_lane_head == h, qc, 0.0) for h in range(N_HEADS)],
                                  axis=0).astype(BF16)
        b_q = jnp.sum(jnp.where(on_diag, bc, 0.0), axis=0, keepdims=True)
        dmat = jnp.where(causal_t, b_q - bc + lic, -jnp.inf)
        m_t = jnp.maximum(b_q + m_st, jnp.max(dmat, axis=0, keepdims=True))
        s_t = (_dot_nt(kc, q_stack) * jnp.exp(dmat - m_t)).astype(BF16)
        m_e = sum(_dot(jnp.where(on_diag, piece.astype(F32), 0.0).astype(BF16), head_ones)
                  for piece in _split_bf16(m_t, 3))
        inter = jnp.exp(bc + m_st - m_e)
        nv = _dot_tn(s_t, jnp.concatenate([vc.astype(BF16), ones_b], axis=1))
        nv = functools.reduce(lambda lo, h: jnp.where(wide_lane_head == h, nv[h * L:(h + 1) * L], lo),
                              range(1, N_HEADS), nv[0:L])
        qcn = _dot(qc.astype(BF16), jnp.concatenate([c_st, n_st], axis=1).astype(BF16))
        num = nv[:, :W] + inter * qcn[:, :W]
        den = nv[:, W:] + inter * qcn[:, W:]
        h_ref[sl, :] = num / jnp.maximum(jnp.abs(den), jnp.exp(-m_e))

        b_last = bc[L - 1:L, :]
        gexp = b_last - bc + lic
        m_new = jnp.maximum(b_last + m_st, jnp.max(gexp, axis=0, keepdims=True))
        wts = jnp.exp(gexp - m_new)
        dec = jnp.exp(b_last + m_st - m_new)
        upd = _dot_tn(kc, jnp.concatenate([wts * vc, wts], axis=1).astype(BF16))
        upd = jnp.where(same_head_k, upd, 0.0)
        c_st = dec * c_st + upd[:, :W]
        n_st = dec * n_st + upd[:, W:]
        m_st = m_new
    c_ref[...] = c_st
    n_ref[...] = n_st
    m_ref[0:1, :] = m_st
    yield

    hv = h_ref[...]
    ms = _dot_split(hv * hv, _head_ones(W, DV), 1) * (1.0 / DV)
    o_ref[...] = hv * lax.rsqrt(ms + NORM_EPS) * ng_ref[...] * _sigmoid(og)


def _t5_bucket(dist):
    max_exact = REL_BUCKETS // 2
    d = np.maximum(dist, 1).astype(np.float32)
    large = max_exact + (np.log(d / max_exact) / math.log(REL_MAX_DIST / max_exact)
                         * (REL_BUCKETS - max_exact)).astype(np.int32)
    large = np.minimum(large, REL_BUCKETS - 1)
    return np.where(dist < max_exact, dist, large).astype(np.int32)


def _swa_bias_table(rb_ref, bucket_ref, bias_ref):
    blk = ATTN_BLOCK
    grp = N_HEADS // SWA_KV_HEADS
    bucket = bucket_ref[...]
    for h in range(N_HEADS):
        acc = jnp.full((2 * blk, blk), -jnp.inf, F32)
        for bk in range(REL_BUCKETS):
            acc = jnp.where(bucket == bk, rb_ref[bk, h], acc)
        bias_ref[h // grp, :, (h % grp) * blk:(h % grp + 1) * blk] = acc


def _swa_kernel(sink_ref, q_ref, kp_ref, kc_ref, vp_ref, vc_ref, o_ref, bias_ref, *, layer):
    blk = ATTN_BLOCK
    grp = N_HEADS // SWA_KV_HEADS
    kvw = SWA_KV_HEADS * HEAD_DIM
    key = lax.broadcasted_iota(jnp.int32, (2 * blk, grp * blk), 0)
    live = (key >= blk) | (pl.program_id(1) > 0)
    member = lax.broadcasted_iota(jnp.int32, (1, grp * blk), 1) // blk
    kw = jnp.concatenate([kp_ref[...], kc_ref[...]], axis=0).astype(BF16)
    vw = jnp.concatenate([vp_ref[...], vc_ref[...]], axis=0).astype(BF16)
    lane_member = lax.broadcasted_iota(jnp.int32, (1, grp * HEAD_DIM), 1) // HEAD_DIM
    rr = lax.broadcasted_iota(jnp.int32, (kvw, grp * HEAD_DIM), 0)
    cc = lax.broadcasted_iota(jnp.int32, (kvw, grp * HEAD_DIM), 1)
    vr = lax.broadcasted_iota(jnp.int32, (HEAD_DIM, kvw), 0)
    vc_ = lax.broadcasted_iota(jnp.int32, (HEAD_DIM, kvw), 1)
    n_sub = q_ref.shape[0] // blk
    k_rep, v_t, sinks = [], [], []
    for j in range(SWA_KV_HEADS):
        spread = ((_idiv(rr, HEAD_DIM) == j) & (_imod(rr, HEAD_DIM) == _imod(cc, HEAD_DIM))).astype(BF16)
        pick = ((_idiv(vc_, HEAD_DIM) == j) & (_imod(vc_, HEAD_DIM) == vr)).astype(BF16)
        k_rep.append(_dot(kw, spread).astype(BF16))
        v_t.append(_dot_nt(pick, vw).astype(BF16))
        sinks.append(jnp.where(member == 0, sink_ref[layer, j * grp], sink_ref[layer, j * grp + 1]))
    pairs = [(n, j) for n in range(n_sub) for j in range(SWA_KV_HEADS)]
    yield
    scores = []
    for n, j in pairs:
        qp = q_ref[n * blk:(n + 1) * blk, j * grp * HEAD_DIM:(j + 1) * grp * HEAD_DIM] * (HEAD_DIM ** -0.5)
        q_stack = jnp.concatenate([jnp.where(lane_member == g, qp, 0.0) for g in range(grp)],
                                  axis=0).astype(BF16)
        s = _dot_nt(k_rep[j][n * blk:(n + 2) * blk], q_stack) + bias_ref[j]
        scores.append(jnp.where(live, s, -jnp.inf) if n == 0 else s)
    yield
    probs = []
    for (n, j), s in zip(pairs, scores):
        m = jnp.maximum(jnp.max(s, axis=0, keepdims=True), sinks[j])
        p = jnp.exp(s - m)
        denom = jnp.sum(p, axis=0, keepdims=True) + jnp.exp(sinks[j] - m)
        probs.append((p / denom).astype(BF16))
    yield
    outs = [_dot(v_t[j][:, n * blk:(n + 2) * blk], p) for (n, j), p in zip(pairs, probs)]
    for n in range(n_sub):
        heads = [outs[n * SWA_KV_HEADS + j][:, g * blk:(g + 1) * blk]
                 for j in range(SWA_KV_HEADS) for g in range(grp)]
        o_ref[n * blk:(n + 1) * blk, :] = jnp.concatenate(heads, axis=0).T


N_RWKV_PARAMS = 11
N_MLSTM_PARAMS = 4
N_RWKV_SCRATCH = 3
N_MLSTM_SCRATCH = 5


def _trace_interleaved(staged):
    done = object()
    staged = list(staged)
    while staged:
        for item in list(staged):
            body, stride = item
            for _ in range(stride):
                if next(body, done) is done:
                    staged.remove(item)
                    break


def _local_mixers_kernel(*refs, tc, layer):
    it = iter(refs)
    take = lambda n: [next(it) for _ in range(n)]
    (za_ref,), rwkv_p = take(1), take(N_RWKV_PARAMS)
    (zb_ref,), mlstm_p = take(1), take(N_MLSTM_PARAMS)
    rb_ref, sink_ref, bucket_ref = take(3)
    swa_in = take(5)
    ya_ref, yb_ref, yc_ref = take(3)
    rwkv_s, mlstm_s = take(N_RWKV_SCRATCH), take(N_MLSTM_SCRATCH)
    (bias_ref,) = take(1)

    @pl.when((pl.program_id(0) == 0) & (pl.program_id(1) == 0))
    def _():
        _swa_bias_table(rb_ref, bucket_ref, bias_ref)

    @pl.when(pl.program_id(1) == 0)
    def _():
        for ref in rwkv_s[:2] + mlstm_s[:4]:
            ref[...] = jnp.zeros_like(ref)

    _trace_interleaved([
        (_rwkv_kernel(za_ref, *rwkv_p, ya_ref, *rwkv_s, tc=tc), 2),
        (_mlstm_kernel(zb_ref, *mlstm_p, yb_ref, *mlstm_s, tc=tc), 1),
        (_swa_kernel(sink_ref, *swa_in, yc_ref, bias_ref, layer=layer), 1),
    ])


def _local_mixers(za, zb, zc, pr, pm, rel_bias, sinks, layer, bsz, t, tc):
    blk = ATTN_BLOCK
    assert tc % blk == 0 and t % tc == 0 and N_HEADS // SWA_KV_HEADS == 2
    nt = t // tc
    full = lambda a: pl.BlockSpec(a.shape, lambda b, i: (0,) * a.ndim)
    tile = lambda w, c=0: pl.BlockSpec((tc, w), lambda b, i: (b * nt + i, c))
    rwkv_p = [pr["mu"], pr["w0"], pr["w_up"], pr["a0"], pr["a_up"], pr["g_up"], pr["k_k"], pr["k_a"], pr["r_k"],
              pr["ln_w"], pr["ln_b"]]
    mlstm_p = [pm["conv_w"], pm["conv_b"], pm["gate_b"], pm["norm_g"]]
    assert len(rwkv_p) == N_RWKV_PARAMS and len(mlstm_p) == N_MLSTM_PARAMS
    dist = np.arange(blk)[None, :] + blk - np.arange(2 * blk)[:, None]
    bucket = jnp.asarray(np.where((dist >= 0) & (dist < SWA_WINDOW),
                                  _t5_bucket(np.clip(dist, 0, SWA_WINDOW - 1)), -1).astype(np.int32))
    kvw = SWA_KV_HEADS * HEAD_DIM
    kcol = MIX_W // kvw
    sub = tc // blk
    before = lambda c: pl.BlockSpec((blk, kvw), lambda b, i: (b * nt * sub + jnp.maximum(i * sub - 1, 0), c))
    smem = pl.BlockSpec(memory_space=pltpu.SMEM)
    return pl.pallas_call(
        functools.partial(_local_mixers_kernel, tc=tc, layer=layer),
        grid=(bsz, nt),
        in_specs=[tile(ZA_W)] + [_lspec(a, layer) for a in rwkv_p] + [tile(ZB_W)] + [_lspec(a, layer) for a in mlstm_p]
        + [smem, smem, full(bucket), tile(MIX_W), before(kcol), tile(kvw, kcol), before(kcol + 1), tile(kvw, kcol + 1)],
        out_specs=[tile(MIX_W)] * 3,
        out_shape=[jax.ShapeDtypeStruct((bsz * t, MIX_W), F32)] * 3,
        scratch_shapes=[
            pltpu.VMEM((SUBLANES, ZA_W), F32),
            pltpu.VMEM((MIX_W, MIX_W), F32),
            pltpu.VMEM((tc, MIX_W), F32),
            pltpu.VMEM((SUBLANES, MIX_W), F32),
            pltpu.VMEM((N_HEADS * MLSTM_DK, MIX_W), F32),
            pltpu.VMEM((N_HEADS * MLSTM_DK, MIX_W), F32),
            pltpu.VMEM((SUBLANES, MIX_W), F32),
            pltpu.VMEM((tc, MIX_W), F32),
            pltpu.VMEM((SWA_KV_HEADS, 2 * blk, 2 * blk), F32),
        ],
        compiler_params=pltpu.CompilerParams(dimension_semantics=("arbitrary", "arbitrary")),
        name="local_mixers",
    )(za, *rwkv_p, zb, *mlstm_p, rel_bias, sinks, bucket, zc, zc, zc, zc, zc)


FOX_BLOCK = 256
FOX_AUG = LANES
FOX_FEAT = 80
FOX_PARTS = 3


def _fox_placements():
    wide = N_HEADS * FOX_AUG
    pk = np.zeros((MIX_W, wide), np.float32)
    pck = np.zeros((LANES, wide), np.float32)
    ones_k = np.zeros((SUBLANES, wide), np.float32)
    tall = N_HEADS * FOX_FEAT
    pq_t = np.zeros((tall, MIX_W), np.float32)
    pv_t = np.zeros((tall, MIX_W), np.float32)
    pcq_t = np.zeros((tall, LANES), np.float32)
    ones_t = np.zeros((2, tall, LANES), np.float32)
    assert HEAD_DIM + 2 * FOX_PARTS <= FOX_FEAT <= FOX_AUG
    for h in range(N_HEADS):
        base = h * FOX_AUG
        base_t = h * FOX_FEAT
        for d in range(HEAD_DIM):
            pk[h * HEAD_DIM + d, base + d] = 1.0
            pq_t[base_t + d, h * HEAD_DIM + d] = HEAD_DIM ** -0.5
            pv_t[base_t + d, h * HEAD_DIM + d] = 1.0
        for n in range(FOX_PARTS):
            pcq_t[base_t + HEAD_DIM + n, n * N_HEADS + h] = 1.0
            pck[n * N_HEADS + h, base + HEAD_DIM + FOX_PARTS + n] = -1.0
            ones_t[0, base_t + HEAD_DIM + FOX_PARTS + n, :] = 1.0
            ones_k[0, base + HEAD_DIM + n] = 1.0
        ones_t[1, base_t + HEAD_DIM, :] = 1.0
    bf = lambda a: jnp.asarray(a, BF16)
    return bf(pk), bf(pck), jnp.asarray(ones_k), bf(np.stack([pq_t, pv_t])), bf(pcq_t), jnp.asarray(ones_t)


def _fox_kernel(q_ref, k_ref, v_ref, f_ref, fb_ref, pk_ref, pck_ref, onesk_ref, pqv_ref, pcq_ref, onest_ref,
                o_ref, kaug_ref, vaug_ref, m_ref, acc_ref, clast_ref, s_ref, s2_ref, p_ref):
    blk = FOX_BLOCK
    i = pl.program_id(1)

    @pl.when(i == 0)
    def _():
        clast_ref[...] = jnp.zeros_like(clast_ref)

    ls = _log_sigmoid(f_ref[...] + fb_ref[...])
    row = lax.broadcasted_iota(jnp.int32, (blk, blk), 0)
    col = lax.broadcasted_iota(jnp.int32, (blk, blk), 1)
    cq = _split_dot((col <= row).astype(BF16), ls, FOX_PARTS) + clast_ref[0:1, :]
    clast_ref[0:1, :] = cq[blk - 1:blk, :]

    widen = lambda a: jnp.concatenate([a] * (blk // LANES), axis=1)
    qa = _dot_nt(pqv_ref[0], q_ref[...].astype(BF16)) + widen(onest_ref[0])
    va = _dot_nt(pqv_ref[1], v_ref[...].astype(BF16)) + widen(onest_ref[1])
    ka = _dot(k_ref[...].astype(BF16), pk_ref[...]) + onesk_ref[0:1, :]
    lane = lax.broadcasted_iota(jnp.int32, (1, LANES), 1)
    pieces = _split_bf16(jnp.where(lane < N_HEADS, cq, 0.0), FOX_PARTS)
    packed = pieces[0].astype(F32)
    for n in range(1, FOX_PARTS):
        packed = packed + pltpu.roll(pieces[n].astype(F32), n * N_HEADS, axis=1)
    packed = packed.astype(BF16)
    qa = (qa + _dot_nt(pcq_ref[...], packed)).astype(BF16)
    ka = ka + _dot(packed, pck_ref[...])
    row0 = pl.multiple_of(i * blk, blk)
    for h in range(N_HEADS):
        kaug_ref[h, pl.ds(row0, blk), :] = ka[:, h * FOX_AUG:(h + 1) * FOX_AUG].astype(BF16)
        vaug_ref[h, :, pl.ds(row0, blk)] = va[h * FOX_FEAT:(h + 1) * FOX_FEAT, :].astype(BF16)

    key_le_query = row <= col

    def scores(j, buf, diagonal=False):
        off = pl.multiple_of(j * blk, blk)
        for h in range(N_HEADS):
            s = _dot(kaug_ref[h, pl.ds(off, blk), 0:FOX_FEAT], qa[h * FOX_FEAT:(h + 1) * FOX_FEAT, :])
            buf[h] = jnp.where(key_le_query, s, -jnp.inf) if diagonal else s

    def consume(j, buf):
        off = pl.multiple_of(j * blk, blk)
        m_new = [jnp.maximum(m_ref[h], jnp.max(buf[h], axis=0, keepdims=True)) for h in range(N_HEADS)]
        for h in range(N_HEADS):
            p_ref[h] = jnp.exp(buf[h] - m_new[h]).astype(BF16)
        for h in range(N_HEADS):
            pv = _dot(vaug_ref[h, :, pl.ds(off, blk)], p_ref[h])
            acc_ref[h] = jnp.exp(m_ref[h] - m_new[h]) * acc_ref[h] + pv
            m_ref[h] = m_new[h]

    m_ref[...] = jnp.full_like(m_ref, -jnp.inf)
    acc_ref[...] = jnp.zeros_like(acc_ref)
    scores(i, s_ref, diagonal=True)

    def body(j, carry):
        scores(2 * j, s2_ref)
        consume(jnp.where(j == 0, i, 2 * j - 1), s_ref)
        scores(jnp.minimum(2 * j + 1, i - 1), s_ref)
        consume(2 * j, s2_ref)
        return carry

    lax.fori_loop(0, (i + 1) // 2, body, 0)

    @pl.when(i % 2 == 0)
    def _():
        consume(jnp.maximum(i - 1, 0), s_ref)

    outs = []
    for h in range(N_HEADS):
        acc = acc_ref[h]
        outs.append(acc[0:HEAD_DIM, :] / acc[HEAD_DIM:HEAD_DIM + 1, :])
    o_ref[...] = jnp.concatenate(outs, axis=0).T


def _fox(zd, fb, layer, bsz, t):
    blk = FOX_BLOCK
    assert t % blk == 0
    nb = t // blk
    consts = _fox_placements()
    const = lambda a: pl.BlockSpec(a.shape, lambda b, i: (0,) * a.ndim)
    cur = lambda c: (lambda b, i: (b * nb + i, c))
    return pl.pallas_call(
        _fox_kernel,
        grid=(bsz, nb),
        in_specs=[
            pl.BlockSpec((blk, MIX_W), cur(0)),
            pl.BlockSpec((blk, MIX_W), cur(1)),
            pl.BlockSpec((blk, MIX_W), cur(2)),
            pl.BlockSpec((blk, LANES), cur(3 * MIX_W // LANES)),
            _lspec(fb, layer)] + [const(a) for a in consts],
        out_specs=pl.BlockSpec((blk, MIX_W), cur(0)),
        out_shape=jax.ShapeDtypeStruct((bsz * t, MIX_W), F32),
        scratch_shapes=[
            pltpu.VMEM((N_HEADS, t, FOX_AUG), BF16),
            pltpu.VMEM((N_HEADS, FOX_FEAT, t), BF16),
            pltpu.VMEM((N_HEADS, 1, blk), F32),
            pltpu.VMEM((N_HEADS, FOX_FEAT, blk), F32),
            pltpu.VMEM((SUBLANES, LANES), F32),
            pltpu.VMEM((N_HEADS, blk, blk), F32),
            pltpu.VMEM((N_HEADS, blk, blk), F32),
            pltpu.VMEM((N_HEADS, blk, blk), BF16),
        ],
        compiler_params=pltpu.CompilerParams(dimension_semantics=("arbitrary", "arbitrary")),
        name="fox",
    )(zd, zd, zd, zd, fb, *consts)


FFN_COL_CHUNK = 6 * MXU_COLS
FFN_ROW_PARTS = 2


def _outproj_ffn_kernel(x_ref, ya_ref, yb_ref, yc_ref, yd_ref, wo_ref, gmix_ref, gpre_ref, wg_ref, wu_ref, cw_ref,
                        cb_ref, wd_ref, gpost_ref, o_ref, tail_ref, *, tm, t):
    seq_start = (pl.program_id(0) * tm) % t == 0

    @pl.when(pl.program_id(0) == 0)
    def _():
        tail_ref[...] = jnp.zeros_like(tail_ref)

    rows = tm // FFN_ROW_PARTS
    gate_tails = {}

    def row_part(pi):
        rs = slice(pi * rows, (pi + 1) * rows)
        mixed = None
        for n, ref in enumerate((ya_ref, yb_ref, yc_ref, yd_ref)):
            part = _dot(ref[rs, :].astype(BF16), wo_ref[n * MIX_W:(n + 1) * MIX_W, :])
            mixed = part if mixed is None else mixed + part
        x = x_ref[rs, :] + mixed * _rms_scale(mixed) * gmix_ref[...]
        h = (x * _rms_scale(x) * gpre_ref[...]).astype(BF16)
        yield
        acc = None
        for c0 in range(0, D_FF, FFN_COL_CHUNK):
            cs = slice(c0, min(c0 + FFN_COL_CHUNK, D_FF))
            gate = _dot(h, wg_ref[:, cs])
            halo = jnp.where(seq_start, 0.0, tail_ref[:, cs]) if pi == 0 else gate_tails[pi - 1, c0]
            gate_tails[pi, c0] = gate[rows - SUBLANES:, :]
            if pi == FFN_ROW_PARTS - 1:
                tail_ref[:, cs] = gate_tails[pi, c0]
            conv = cb_ref[:, cs] + cw_ref[FFN_CONV - 1:FFN_CONV, cs] * gate
            for d in range(1, FFN_CONV):
                conv = conv + cw_ref[FFN_CONV - 1 - d:FFN_CONV - d, cs] * _shift_rows(gate, halo, d)
            f = jax.nn.gelu(conv, approximate=True) * _dot(h, wu_ref[:, cs])
            part = _dot(f.astype(BF16), wd_ref[cs, :])
            acc = part if acc is None else acc + part
            yield
        o_ref[rs, :] = x + acc * _rms_scale(acc) * gpost_ref[...]

    _trace_interleaved([(row_part(pi), 1) for pi in range(FFN_ROW_PARTS)])


def _outproj_ffn_vmem_bytes(tm):
    weights = 2 * (N_HEADS * MIX_W * D_MODEL + 2 * D_MODEL * D_FF + D_FF * D_MODEL)
    tiles = 2 * 4 * tm * (2 * D_MODEL + N_HEADS * MIX_W)
    temps = 4 * tm * (6 * FFN_COL_CHUNK + 3 * D_MODEL)
    return weights + tiles + temps


def _outproj_ffn(x2, ys, w_out, g_mix, p, layer, t, tm):
    rows = x2.shape[0]
    once = dict(pipeline_mode=pl.Buffered(1))
    gate_up = p["w_gate_up"]
    params = [w_out, g_mix, p["g_pre"], gate_up, gate_up, p["conv_w"], p["conv_b"], p["w_down"], p["g_post"]]
    specs = [_lspec(a, layer, **once) for a in params]
    specs[3] = _lspec(gate_up, layer, block=(D_MODEL, D_FF), at=(0, 0), **once)
    specs[4] = _lspec(gate_up, layer, block=(D_MODEL, D_FF), at=(0, 1), **once)
    return pl.pallas_call(
        functools.partial(_outproj_ffn_kernel, tm=tm, t=t),
        grid=(rows // tm,),
        in_specs=[pl.BlockSpec((tm, D_MODEL), lambda i: (i, 0))]
        + [pl.BlockSpec((tm, MIX_W), lambda i: (i, 0))] * len(ys)
        + specs,
        out_specs=pl.BlockSpec((tm, D_MODEL), lambda i: (i, 0)),
        out_shape=jax.ShapeDtypeStruct((rows, D_MODEL), F32),
        scratch_shapes=[pltpu.VMEM((SUBLANES, D_FF), F32)],
        compiler_params=pltpu.CompilerParams(dimension_semantics=("arbitrary",),
                                             vmem_limit_bytes=_outproj_ffn_vmem_bytes(tm)),
        name="outproj_ffn",
    )(x2, *ys, *params)


def _in_weights(w_in, l):
    wa_w = ZA_W
    wb_w = 3 * MIX_W + 2 * N_HEADS
    wc_w = ZC_W
    wi = jnp.transpose(w_in, (2, 0, 1))[:, l, :].astype(BF16)
    pad_rows = lambda a, n: jnp.pad(a, ((0, n - a.shape[0]), (0, 0)))
    groups = [wi[:wa_w],
              pad_rows(wi[wa_w:wa_w + wb_w], ZB_W),
              wi[wa_w + wb_w:wa_w + wb_w + wc_w],
              pad_rows(wi[wa_w + wb_w + wc_w:], ZD_W)]
    return jnp.concatenate(groups, axis=0)


def _rows(a):
    return a.reshape(a.shape[0], 1, -1).astype(F32)


def _lspec(a, layer, block=None, at=None, **kw):
    block = tuple(a.shape[1:]) if block is None else block
    at = (0,) * len(block) if at is None else at
    return pl.BlockSpec((None,) + block, lambda *_: (layer,) + at, **kw)


def _stacked_params(w_out, norm_mix_pre, norm_mix_post, norm_ffn_pre, norm_ffn_post,
                    rwkv_mu, rwkv_w0, rwkv_w_up, rwkv_a0, rwkv_a_up, rwkv_g_up, rwkv_k_k, rwkv_k_a,
                    rwkv_r_k, rwkv_ln_w, rwkv_ln_b, mlstm_conv_w, mlstm_conv_b, mlstm_b_i, mlstm_b_f,
                    mlstm_norm, swa_sinks, fox_b_f, ffn_w_up, ffn_conv_w, ffn_conv_b, ffn_w_down):
    pad_lanes = lambda a: jnp.pad(a, ((0, 0), (0, 0), (0, LANES - a.shape[-1])))
    return {
        "g_mix_pre": _rows(norm_mix_pre),
        "g_mix_post": _rows(norm_mix_post),
        "w_out": w_out.astype(BF16),
        "rwkv": {"mu": _rows(rwkv_mu), "w0": _rows(rwkv_w0), "w_up": rwkv_w_up.astype(BF16),
                 "a0": _rows(rwkv_a0), "a_up": rwkv_a_up.astype(BF16), "g_up": rwkv_g_up.astype(BF16),
                 "k_k": _rows(rwkv_k_k), "k_a": _rows(rwkv_k_a), "r_k": _rows(rwkv_r_k),
                 "ln_w": _rows(rwkv_ln_w), "ln_b": _rows(rwkv_ln_b)},
        "mlstm": {"conv_w": mlstm_conv_w, "conv_b": _rows(mlstm_conv_b),
                  "gate_b": pad_lanes(_rows(jnp.concatenate([mlstm_b_i, mlstm_b_f], axis=-1))),
                  "norm_g": _rows(mlstm_norm)},
        "swa_sinks": swa_sinks,
        "fox_b": pad_lanes(_rows(fox_b_f)),
        "ffn": {"g_pre": _rows(norm_ffn_pre), "w_gate_up": ffn_w_up.astype(BF16), "conv_w": ffn_conv_w,
                "conv_b": _rows(ffn_conv_b), "w_down": ffn_w_down.astype(BF16), "g_post": _rows(norm_ffn_post)},
    }


def _tiles(t):
    return min(512, t), min(512, t), min(512, t)


def kernel(x, w_in, w_out, norm_mix_pre, norm_mix_post, norm_ffn_pre, norm_ffn_post, rwkv_mu, rwkv_w0, rwkv_w_up, rwkv_a0, rwkv_a_up, rwkv_g_up, rwkv_k_k, rwkv_k_a, rwkv_r_k, rwkv_ln_w, rwkv_ln_b, mlstm_conv_w, mlstm_conv_b, mlstm_b_i, mlstm_b_f, mlstm_norm, swa_sinks, fox_b_f, rel_bias, ffn_w_up, ffn_conv_w, ffn_conv_b, ffn_w_down):
    bsz, t, d = x.shape
    assert d == D_MODEL and t % ATTN_BLOCK == 0
    tm, tc, tf = _tiles(t)
    x2 = x.reshape(bsz * t, d)
    p = _stacked_params(w_out, norm_mix_pre, norm_mix_post, norm_ffn_pre, norm_ffn_post,
                        rwkv_mu, rwkv_w0, rwkv_w_up, rwkv_a0, rwkv_a_up, rwkv_g_up, rwkv_k_k, rwkv_k_a,
                        rwkv_r_k, rwkv_ln_w, rwkv_ln_b, mlstm_conv_w, mlstm_conv_b, mlstm_b_i, mlstm_b_f,
                        mlstm_norm, swa_sinks, fox_b_f, ffn_w_up, ffn_conv_w, ffn_conv_b, ffn_w_down)
    for l in range(w_in.shape[0]):
        za, zb, zc, zd = _inproj(x2, p["g_mix_pre"], _in_weights(w_in, l), l, tm)
        ya, yb, yc = _local_mixers(za, zb, zc, p["rwkv"], p["mlstm"], rel_bias, p["swa_sinks"], l, bsz, t, tc)
        yd = _fox(zd, p["fox_b"], l, bsz, t)
        x2 = _outproj_ffn(x2, (ya, yb, yc, yd), p["w_out"], p["g_mix_post"], p["ffn"], l, t, tf)
    return x2.reshape(bsz, t, d)
```

```python
import functools
import math

import jax
import jax.numpy as jnp
import numpy as np
from jax import lax
from jax.experimental import pallas as pl
from jax.experimental.pallas import tpu as pltpu

F32 = jnp.float32
BF16 = jnp.bfloat16

D_MODEL = 1024
HEAD_DIM = 64
N_HEADS = 4
MIX_W = N_HEADS * HEAD_DIM
RWKV_DECAY_RANK = 64
RWKV_AAA_RANK = 64
RWKV_GATE_RANK = 128
RWKV_LN_EPS = 64e-5
RWKV_CHUNK = 64
MLSTM_DK = 32
MLSTM_CONV = 4
MLSTM_CHUNK = 64
GATE_SOFTCAP = 15.0
SWA_KV_HEADS = 2
SWA_WINDOW = 128
ATTN_BLOCK = 128
REL_BUCKETS = 32
REL_MAX_DIST = 128
D_FF = 2816
FFN_CONV = 3
NORM_EPS = 1e-6

LANES = 128
SUBLANES = 8
MXU_COLS = 256

ZA_W = 3 * MIX_W + RWKV_DECAY_RANK + RWKV_AAA_RANK + RWKV_GATE_RANK
ZB_W = 3 * MIX_W + LANES
ZC_W = MIX_W + 2 * SWA_KV_HEADS * HEAD_DIM
ZD_W = 3 * MIX_W + LANES
Z_W = ZA_W + ZB_W + ZC_W + ZD_W


def _split_bf16(x, parts):
    out = []
    for n in range(parts):
        piece = x.astype(BF16)
        out.append(piece)
        if n + 1 < parts:
            x = x - piece.astype(F32)
    return out


def _dot(a, b):
    return jnp.dot(a, b, preferred_element_type=F32)


def _dot_nt(a, b):
    return lax.dot_general(a, b, (((1,), (1,)), ((), ())), preferred_element_type=F32)


def _dot_tn(a, b):
    return lax.dot_general(a, b, (((0,), (0,)), ((), ())), preferred_element_type=F32)


def _dot_split(x, ones, parts):
    return sum(_dot(piece, ones) for piece in _split_bf16(x, parts))


def _split_dot(ones, x, parts):
    return sum(_dot(ones, piece) for piece in _split_bf16(x, parts))


def _bdot(a, b):
    return _dot(a.astype(BF16), b.astype(BF16))


def _sigmoid(x):
    return 1.0 / (1.0 + jnp.exp(-x))


def _log_sigmoid(x):
    return jnp.minimum(x, 0.0) - jnp.log(1.0 + jnp.exp(-jnp.abs(x)))


def _softplus(x):
    return jnp.maximum(x, 0.0) + jnp.log(1.0 + jnp.exp(-jnp.abs(x)))


def _rms_scale(x):
    return lax.rsqrt(jnp.mean(x * x, axis=-1, keepdims=True) + NORM_EPS)


def _shift_rows(x, prev_tail, d):
    rolled = pltpu.roll(x, d, axis=0)
    head_rows = lax.broadcasted_iota(jnp.int32, (SUBLANES, 1), 0)
    head = jnp.where(head_rows < d, pltpu.roll(prev_tail, d, axis=0), rolled[:SUBLANES])
    return jnp.concatenate([head, rolled[SUBLANES:]], axis=0)


def _idiv(x, n):
    assert n & (n - 1) == 0
    return lax.shift_right_logical(x, jnp.int32(n.bit_length() - 1))


def _imod(x, n):
    assert n & (n - 1) == 0
    return lax.bitwise_and(x, jnp.int32(n - 1))


def _head_ones(width, head):
    r = _idiv(lax.broadcasted_iota(jnp.int32, (width, width), 0), head)
    c = _idiv(lax.broadcasted_iota(jnp.int32, (width, width), 1), head)
    return (r == c).astype(BF16)


INPROJ_ROW_PARTS = 2


def _inproj_kernel(x_ref, g_ref, w_ref, za_ref, zb_ref, zc_ref, zd_ref):
    rows = x_ref.shape[0] // INPROJ_ROW_PARTS
    for pi in range(INPROJ_ROW_PARTS):
        rs = slice(pi * rows, (pi + 1) * rows)
        x = x_ref[rs, :]
        h = (x * _rms_scale(x) * g_ref[...]).astype(BF16)
        z = _dot_nt(h, w_ref[...])
        off = 0
        for ref in (za_ref, zb_ref, zc_ref, zd_ref):
            w = ref.shape[1]
            ref[rs, :] = z[:, off:off + w]
            off += w


def _inproj(x2, g, w, layer, tm):
    rows = x2.shape[0]
    return pl.pallas_call(
        _inproj_kernel,
        grid=(rows // tm,),
        in_specs=[
            pl.BlockSpec((tm, D_MODEL), lambda i: (i, 0)),
            _lspec(g, layer),
            pl.BlockSpec((Z_W, D_MODEL), lambda i: (0, 0)),
        ],
        out_specs=[pl.BlockSpec((tm, w_), lambda i: (i, 0)) for w_ in (ZA_W, ZB_W, ZC_W, ZD_W)],
        out_shape=[jax.ShapeDtypeStruct((rows, w_), F32) for w_ in (ZA_W, ZB_W, ZC_W, ZD_W)],
        compiler_params=pltpu.CompilerParams(dimension_semantics=("arbitrary",)),
        name="inproj",
    )(x2, g, w)


def _rwkv_kernel(z_ref, mu_ref, w0_ref, wup_ref, a0_ref, aup_ref, gup_ref, kk_ref, ka_ref, rk_ref,
                 lnw_ref, lnb_ref, o_ref, tail_ref, st_ref, y_ref, *, tc):
    L = RWKV_CHUNK
    W = MIX_W

    z = z_ref[...]
    zz = z + mu_ref[...] * (_shift_rows(z, tail_ref[...], 1) - z)
    tail_ref[...] = z[tc - SUBLANES:, :]

    r = zz[:, 0:W]
    k = zz[:, W:2 * W]
    v = zz[:, 2 * W:3 * W]
    o1 = 3 * W
    wd = zz[:, o1:o1 + RWKV_DECAY_RANK]
    ad = zz[:, o1 + RWKV_DECAY_RANK:o1 + RWKV_DECAY_RANK + RWKV_AAA_RANK]
    gd = zz[:, o1 + RWKV_DECAY_RANK + RWKV_AAA_RANK:]

    lw = -jnp.exp(-_softplus(-(w0_ref[...] + _bdot(jnp.tanh(wd), wup_ref[...]))) - 0.5)
    alpha = _sigmoid(a0_ref[...] + _bdot(ad, aup_ref[...]))
    g = _bdot(_sigmoid(gd), gup_ref[...])

    hsum = _head_ones(W, HEAD_DIM)
    kk = k * kk_ref[...]
    kk = kk * lax.rsqrt(jnp.maximum(_dot_split(kk * kk, hsum, 2), 1e-24))
    k = k * (1.0 + (alpha - 1.0) * ka_ref[...])

    span = min(tc, MXU_COLS)
    rt = lax.broadcasted_iota(jnp.int32, (span, span), 0)
    ct = lax.broadcasted_iota(jnp.int32, (span, span), 1)
    tri = ((_idiv(rt, L) == _idiv(ct, L)) & (ct <= rt)).astype(BF16)
    cum = jnp.concatenate([_split_dot(tri, lw[r0:r0 + span], 2) for r0 in range(0, tc, span)], axis=0)
    e_in = jnp.exp(cum)
    e_out = jnp.exp(-cum)
    r_t = r * e_in
    a_t = -kk * jnp.exp(cum - lw)
    b_t = kk * alpha * e_out
    k_t = k * e_out

    lane_head = lax.broadcasted_iota(jnp.int32, (1, W), 1) // HEAD_DIM

    def stack(xc):
        return jnp.concatenate([jnp.where(lane_head == h, xc, 0.0) for h in range(N_HEADS)], axis=0)

    n = N_HEADS * L
    rr = lax.broadcasted_iota(jnp.int32, (n, n), 0)
    cc = lax.broadcasted_iota(jnp.int32, (n, n), 1)
    own_head = _idiv(rr, L) == _idiv(cc, L)
    strict = own_head & (_imod(rr, L) > _imod(cc, L))
    incl = own_head & (_imod(rr, L) >= _imod(cc, L))
    eye = (rr == cc).astype(F32)

    over_heads = lambda x: jnp.concatenate([x, x], axis=1)

    def state_free(chunks):
        stk = [tuple(stack(u[sl]).astype(BF16) for u in (a_t, r_t, b_t, k_t, v)) for sl in chunks]
        bk_rep = [jnp.concatenate([b_t[sl]] * 2 + [k_t[sl]] * 2, axis=0).astype(BF16) for sl in chunks]
        ar = [_dot_nt(jnp.concatenate([s_[0], s_[1]], axis=0), x) for s_, x in zip(stk, bk_rep)]
        a_ab = [jnp.where(strict, over_heads(x[:n, :2 * L]), 0.0) for x in ar]
        a_ak = [jnp.where(strict, over_heads(x[:n, 2 * L:]), 0.0).astype(BF16) for x in ar]
        a_rb = [jnp.where(incl, over_heads(x[n:, :2 * L]), 0.0).astype(BF16) for x in ar]
        a_rk = [jnp.where(incl, over_heads(x[n:, 2 * L:]), 0.0).astype(BF16) for x in ar]
        yield
        inv = [eye + m for m in a_ab]
        pw = [m.astype(BF16) for m in a_ab]
        for _ in range(int(math.log2(L)) - 1):
            pw = [_dot(m, m).astype(BF16) for m in pw]
            inv = [t_ + _dot(t_.astype(BF16), m) for t_, m in zip(inv, pw)]
            yield
        inv = [t_.astype(BF16) for t_ in inv]
        akv = [_dot(m, s_[4]).astype(BF16) for m, s_ in zip(a_ak, stk)]
        wu = [_dot(t_, jnp.concatenate([s_[0], x], axis=1)) for t_, s_, x in zip(inv, stk, akv)]
        w_m = [x[:, :W].astype(BF16) for x in wu]
        u2 = [x[:, W:].astype(BF16) for x in wu]
        yield
        r_eff = [(s_[1].astype(F32) + _dot(m, w_)).astype(BF16) for s_, m, w_ in zip(stk, a_rb, w_m)]
        y_own = [_dot(m, u_) + _dot(n_, s_[4]) for m, u_, n_, s_ in zip(a_rb, u2, a_rk, stk)]
        yield
        st_mix = [_dot_tn(w_, s_[2]).astype(BF16) for w_, s_ in zip(w_m, stk)]
        st_own = [_dot_tn(u_, s_[2]) + _dot_tn(s_[4], s_[3]) for u_, s_ in zip(u2, stk)]
        return r_eff, y_own, st_mix, st_own

    def recurrence(st, sl, r_eff, y_own, st_mix, st_own):
        st_b = st.astype(BF16)
        y_s = _dot_nt(r_eff, st_b) + y_own
        y_c = y_s[0:L]
        for h in range(1, N_HEADS):
            y_c = y_c + y_s[h * L:(h + 1) * L]
        y_ref[sl, :] = y_c
        return (st + _dot(st_b, st_mix) + st_own) * e_in[sl.stop - 1:sl.stop, :]

    yield
    chunks = [slice(c * L, (c + 1) * L) for c in range(tc // L)]
    ready = yield from state_free(chunks)
    st = st_ref[...]
    for c, sl in enumerate(chunks):
        st = recurrence(st, sl, *(part[c] for part in ready))
        yield
    st_ref[...] = st

    y = y_ref[...]
    inv_n = 1.0 / HEAD_DIM
    mean = _dot_split(y, hsum, 1) * inv_n
    yc = y - mean
    var = _dot_split(yc * yc, hsum, 1) * inv_n
    y = yc * lax.rsqrt(var + RWKV_LN_EPS) * lnw_ref[...] + lnb_ref[...]
    bonus = _dot_split(r * k * rk_ref[...], hsum, 1) * v
    o_ref[...] = (y + bonus) * g


def _mlstm_kernel(z_ref, cw_ref, cb_ref, gb_ref, ng_ref, o_ref, tail_ref, c_ref, n_ref, m_ref, h_ref, *, tc):
    L = MLSTM_CHUNK
    W = MIX_W
    DK = MLSTM_DK
    DV = HEAD_DIM

    qk_in = z_ref[:, 0:W]
    v = z_ref[:, W:2 * W]
    og = z_ref[:, 2 * W:3 * W]
    gates = z_ref[:, 3 * W:3 * W + LANES]

    tail = tail_ref[...]
    conv = cb_ref[...] + cw_ref[MLSTM_CONV - 1:MLSTM_CONV, :] * qk_in
    for d in range(1, MLSTM_CONV):
        conv = conv + cw_ref[MLSTM_CONV - 1 - d:MLSTM_CONV - d, :] * _shift_rows(qk_in, tail, d)
    tail_ref[...] = qk_in[tc - SUBLANES:, :]
    qk = conv * _sigmoid(conv)
    q = qk[:, 0:N_HEADS * DK] * (DK ** -0.5)
    k = qk[:, N_HEADS * DK:]

    capped = GATE_SOFTCAP * jnp.tanh((gates + gb_ref[...]) / GATE_SOFTCAP)
    lf = _log_sigmoid(capped)

    gate_col = lax.broadcasted_iota(jnp.int32, (LANES, W), 0)
    lane_head = _idiv(lax.broadcasted_iota(jnp.int32, (LANES, W), 1), DV)
    pick_i = (gate_col == lane_head).astype(BF16)
    pick_f = (gate_col == N_HEADS + lane_head).astype(BF16)
    li_e = _dot_split(capped, pick_i, 3)
    span = min(tc, MXU_COLS)
    rt = lax.broadcasted_iota(jnp.int32, (span, span), 0)
    ct = lax.broadcasted_iota(jnp.int32, (span, span), 1)
    tri = ((_idiv(rt, L) == _idiv(ct, L)) & (ct <= rt)).astype(BF16)
    lf_e = [_dot(piece, pick_f).astype(BF16) for piece in _split_bf16(lf, 3)]
    b_e = jnp.concatenate([sum(_dot(tri, piece[r0:r0 + span]) for piece in lf_e)
                           for r0 in range(0, tc, span)], axis=0)

    key = lax.broadcasted_iota(jnp.int32, (L, W), 0)
    query = _imod(lax.broadcasted_iota(jnp.int32, (L, W), 1), L)
    on_diag = key == query
    causal_t = key <= query
    head_ones = _head_ones(W, DV)
    wide = lambda m: jnp.concatenate([m, m], axis=1)
    wide_lane_head = wide(lax.broadcasted_iota(jnp.int32, (1, W), 1) // DV)
    same_head_k = wide(_idiv(lax.broadcasted_iota(jnp.int32, (N_HEADS * DK, W), 0), DK)
                       == _idiv(lax.broadcasted_iota(jnp.int32, (N_HEADS * DK, W), 1), DV))
    q_lane_head = lax.broadcasted_iota(jnp.int32, (1, N_HEADS * DK), 1) // DK
    ones_b = jnp.ones((L, W), BF16)

    c_st = c_ref[...]
    n_st = n_ref[...]
    m_st = m_ref[0:1, :]
    for c in range(tc // L):
        yield
        sl = slice(c * L, (c + 1) * L)
        bc, lic, vc = b_e[sl], li_e[sl], v[sl]
        qc = q[sl]
        kc = k[sl].astype(BF16)
        q_stack = jnp.concatenate([jnp.where(q_lane_head == h, qc, 0.0) for h in range(N_HEADS)],
                                  axis=0).astype(BF16)
        b_q = jnp.sum(jnp.where(on_diag, bc, 0.0), axis=0, keepdims=True)
        dmat = jnp.where(causal_t, b_q - bc + lic, -jnp.inf)
        m_t = jnp.maximum(b_q + m_st, jnp.max(dmat, axis=0, keepdims=True))
        s_t = (_dot_nt(kc, q_stack) * jnp.exp(dmat - m_t)).astype(BF16)
        m_e = sum(_dot(jnp.where(on_diag, piece.astype(F32), 0.0).astype(BF16), head_ones)
                  for piece in _split_bf16(m_t, 3))
        inter = jnp.exp(bc + m_st - m_e)
        nv = _dot_tn(s_t, jnp.concatenate([vc.astype(BF16), ones_b], axis=1))
        nv = functools.reduce(lambda lo, h: jnp.where(wide_lane_head == h, nv[h * L:(h + 1) * L], lo),
                              range(1, N_HEADS), nv[0:L])
        qcn = _dot(qc.astype(BF16), jnp.concatenate([c_st, n_st], axis=1).astype(BF16))
        num = nv[:, :W] + inter * qcn[:, :W]
        den = nv[:, W:] + inter * qcn[:, W:]
        h_ref[sl, :] = num / jnp.maximum(jnp.abs(den), jnp.exp(-m_e))

        b_last = bc[L - 1:L, :]
        gexp = b_last - bc + lic
        m_new = jnp.maximum(b_last + m_st, jnp.max(gexp, axis=0, keepdims=True))
        wts = jnp.exp(gexp - m_new)
        dec = jnp.exp(b_last + m_st - m_new)
        upd = _dot_tn(kc, jnp.concatenate([wts * vc, wts], axis=1).astype(BF16))
        upd = jnp.where(same_head_k, upd, 0.0)
        c_st = dec * c_st + upd[:, :W]
        n_st = dec * n_st + upd[:, W:]
        m_st = m_new
    c_ref[...] = c_st
    n_ref[...] = n_st
    m_ref[0:1, :] = m_st
    yield

    hv = h_ref[...]
    ms = _dot_split(hv * hv, _head_ones(W, DV), 1) * (1.0 / DV)
    o_ref[...] = hv * lax.rsqrt(ms + NORM_EPS) * ng_ref[...] * _sigmoid(og)


def _t5_bucket(dist):
    max_exact = REL_BUCKETS // 2
    d = np.maximum(dist, 1).astype(np.float32)
    large = max_exact + (np.log(d / max_exact) / math.log(REL_MAX_DIST / max_exact)
                         * (REL_BUCKETS - max_exact)).astype(np.int32)
    large = np.minimum(large, REL_BUCKETS - 1)
    return np.where(dist < max_exact, dist, large).astype(np.int32)


def _swa_bias_table(rb_ref, bucket_ref, bias_ref):
    blk = ATTN_BLOCK
    grp = N_HEADS // SWA_KV_HEADS
    bucket = bucket_ref[...]
    for h in range(N_HEADS):
        acc = jnp.full((2 * blk, blk), -jnp.inf, F32)
        for bk in range(REL_BUCKETS):
            acc = jnp.where(bucket == bk, rb_ref[bk, h], acc)
        bias_ref[h // grp, :, (h % grp) * blk:(h % grp + 1) * blk] = acc


def _swa_kernel(sink_ref, q_ref, kp_ref, kc_ref, vp_ref, vc_ref, o_ref, bias_ref, *, layer):
    blk = ATTN_BLOCK
    grp = N_HEADS // SWA_KV_HEADS
    kvw = SWA_KV_HEADS * HEAD_DIM
    key = lax.broadcasted_iota(jnp.int32, (2 * blk, grp * blk), 0)
    live = (key >= blk) | (pl.program_id(1) > 0)
    member = lax.broadcasted_iota(jnp.int32, (1, grp * blk), 1) // blk
    kw = jnp.concatenate([kp_ref[...], kc_ref[...]], axis=0).astype(BF16)
    vw = jnp.concatenate([vp_ref[...], vc_ref[...]], axis=0).astype(BF16)
    lane_member = lax.broadcasted_iota(jnp.int32, (1, grp * HEAD_DIM), 1) // HEAD_DIM
    rr = lax.broadcasted_iota(jnp.int32, (kvw, grp * HEAD_DIM), 0)
    cc = lax.broadcasted_iota(jnp.int32, (kvw, grp * HEAD_DIM), 1)
    vr = lax.broadcasted_iota(jnp.int32, (HEAD_DIM, kvw), 0)
    vc_ = lax.broadcasted_iota(jnp.int32, (HEAD_DIM, kvw), 1)
    n_sub = q_ref.shape[0] // blk
    k_rep, v_t, sinks = [], [], []
    for j in range(SWA_KV_HEADS):
        spread = ((_idiv(rr, HEAD_DIM) == j) & (_imod(rr, HEAD_DIM) == _imod(cc, HEAD_DIM))).astype(BF16)
        pick = ((_idiv(vc_, HEAD_DIM) == j) & (_imod(vc_, HEAD_DIM) == vr)).astype(BF16)
        k_rep.append(_dot(kw, spread).astype(BF16))
        v_t.append(_dot_nt(pick, vw).astype(BF16))
        sinks.append(jnp.where(member == 0, sink_ref[layer, j * grp], sink_ref[layer, j * grp + 1]))
    pairs = [(n, j) for n in range(n_sub) for j in range(SWA_KV_HEADS)]
    yield
    scores = []
    for n, j in pairs:
        qp = q_ref[n * blk:(n + 1) * blk, j * grp * HEAD_DIM:(j + 1) * grp * HEAD_DIM] * (HEAD_DIM ** -0.5)
        q_stack = jnp.concatenate([jnp.where(lane_member == g, qp, 0.0) for g in range(grp)],
                                  axis=0).astype(BF16)
        s = _dot_nt(k_rep[j][n * blk:(n + 2) * blk], q_stack) + bias_ref[j]
        scores.append(jnp.where(live, s, -jnp.inf) if n == 0 else s)
    yield
    probs = []
    for (n, j), s in zip(pairs, scores):
        m = jnp.maximum(jnp.max(s, axis=0, keepdims=True), sinks[j])
        p = jnp.exp(s - m)
        denom = jnp.sum(p, axis=0, keepdims=True) + jnp.exp(sinks[j] - m)
        probs.append((p / denom).astype(BF16))
    yield
    outs = [_dot(v_t[j][:, n * blk:(n + 2) * blk], p) for (n, j), p in zip(pairs, probs)]
    for n in range(n_sub):
        heads = [outs[n * SWA_KV_HEADS + j][:, g * blk:(g + 1) * blk]
                 for j in range(SWA_KV_HEADS) for g in range(grp)]
        o_ref[n * blk:(n + 1) * blk, :] = jnp.concatenate(heads, axis=0).T


N_RWKV_PARAMS = 11
N_MLSTM_PARAMS = 4
N_RWKV_SCRATCH = 3
N_MLSTM_SCRATCH = 5


def _trace_interleaved(staged):
    done = object()
    staged = list(staged)
    while staged:
        for item in list(staged):
            body, stride = item
            for _ in range(stride):
                if next(body, done) is done:
                    staged.remove(item)
                    break


def _local_mixers_kernel(*refs, tc, layer):
    it = iter(refs)
    take = lambda n: [next(it) for _ in range(n)]
    (za_ref,), rwkv_p = take(1), take(N_RWKV_PARAMS)
    (zb_ref,), mlstm_p = take(1), take(N_MLSTM_PARAMS)
    rb_ref, sink_ref, bucket_ref = take(3)
    swa_in = take(5)
    ya_ref, yb_ref, yc_ref = take(3)
    rwkv_s, mlstm_s = take(N_RWKV_SCRATCH), take(N_MLSTM_SCRATCH)
    (bias_ref,) = take(1)

    @pl.when((pl.program_id(0) == 0) & (pl.program_id(1) == 0))
    def _():
        _swa_bias_table(rb_ref, bucket_ref, bias_ref)

    @pl.when(pl.program_id(1) == 0)
    def _():
        for ref in rwkv_s[:2] + mlstm_s[:4]:
            ref[...] = jnp.zeros_like(ref)

    _trace_interleaved([
        (_rwkv_kernel(za_ref, *rwkv_p, ya_ref, *rwkv_s, tc=tc), 1),
        (_mlstm_kernel(zb_ref, *mlstm_p, yb_ref, *mlstm_s, tc=tc), 1),
        (_swa_kernel(sink_ref, *swa_in, yc_ref, bias_ref, layer=layer), 1),
    ])


def _local_mixers(za, zb, zc, pr, pm, rel_bias, sinks, layer, bsz, t, tc):
    blk = ATTN_BLOCK
    assert tc % blk == 0 and t % tc == 0 and N_HEADS // SWA_KV_HEADS == 2
    nt = t // tc
    full = lambda a: pl.BlockSpec(a.shape, lambda b, i: (0,) * a.ndim)
    tile = lambda w, c=0: pl.BlockSpec((tc, w), lambda b, i: (b * nt + i, c))
    rwkv_p = [pr["mu"], pr["w0"], pr["w_up"], pr["a0"], pr["a_up"], pr["g_up"], pr["k_k"], pr["k_a"], pr["r_k"],
              pr["ln_w"], pr["ln_b"]]
    mlstm_p = [pm["conv_w"], pm["conv_b"], pm["gate_b"], pm["norm_g"]]
    assert len(rwkv_p) == N_RWKV_PARAMS and len(mlstm_p) == N_MLSTM_PARAMS
    dist = np.arange(blk)[None, :] + blk - np.arange(2 * blk)[:, None]
    bucket = jnp.asarray(np.where((dist >= 0) & (dist < SWA_WINDOW),
                                  _t5_bucket(np.clip(dist, 0, SWA_WINDOW - 1)), -1).astype(np.int32))
    kvw = SWA_KV_HEADS * HEAD_DIM
    kcol = MIX_W // kvw
    sub = tc // blk
    before = lambda c: pl.BlockSpec((blk, kvw), lambda b, i: (b * nt * sub + jnp.maximum(i * sub - 1, 0), c))
    smem = pl.BlockSpec(memory_space=pltpu.SMEM)
    return pl.pallas_call(
        functools.partial(_local_mixers_kernel, tc=tc, layer=layer),
        grid=(bsz, nt),
        in_specs=[tile(ZA_W)] + [_lspec(a, layer) for a in rwkv_p] + [tile(ZB_W)] + [_lspec(a, layer) for a in mlstm_p]
        + [smem, smem, full(bucket), tile(MIX_W), before(kcol), tile(kvw, kcol), before(kcol + 1), tile(kvw, kcol + 1)],
        out_specs=[tile(MIX_W)] * 3,
        out_shape=[jax.ShapeDtypeStruct((bsz * t, MIX_W), F32)] * 3,
        scratch_shapes=[
            pltpu.VMEM((SUBLANES, ZA_W), F32),
            pltpu.VMEM((MIX_W, MIX_W), F32),
            pltpu.VMEM((tc, MIX_W), F32),
            pltpu.VMEM((SUBLANES, MIX_W), F32),
            pltpu.VMEM((N_HEADS * MLSTM_DK, MIX_W), F32),
            pltpu.VMEM((N_HEADS * MLSTM_DK, MIX_W), F32),
            pltpu.VMEM((SUBLANES, MIX_W), F32),
            pltpu.VMEM((tc, MIX_W), F32),
            pltpu.VMEM((SWA_KV_HEADS, 2 * blk, 2 * blk), F32),
        ],
        compiler_params=pltpu.CompilerParams(dimension_semantics=("arbitrary", "arbitrary")),
        name="local_mixers",
    )(za, *rwkv_p, zb, *mlstm_p, rel_bias, sinks, bucket, zc, zc, zc, zc, zc)


FOX_BLOCK = 256
FOX_AUG = LANES
FOX_FEAT = 80
FOX_PARTS = 3


def _fox_placements():
    wide = N_HEADS * FOX_AUG
    pk = np.zeros((MIX_W, wide), np.float32)
    pck = np.zeros((LANES, wide), np.float32)
    ones_k = np.zeros((SUBLANES, wide), np.float32)
    tall = N_HEADS * FOX_FEAT
    pq_t = np.zeros((tall, MIX_W), np.float32)
    pv_t = np.zeros((tall, MIX_W), np.float32)
    pcq_t = np.zeros((tall, LANES), np.float32)
    ones_t = np.zeros((2, tall, LANES), np.float32)
    assert HEAD_DIM + 2 * FOX_PARTS <= FOX_FEAT <= FOX_AUG
    for h in range(N_HEADS):
        base = h * FOX_AUG
        base_t = h * FOX_FEAT
        for d in range(HEAD_DIM):
            pk[h * HEAD_DIM + d, base + d] = 1.0
            pq_t[base_t + d, h * HEAD_DIM + d] = HEAD_DIM ** -0.5
            pv_t[base_t + d, h * HEAD_DIM + d] = 1.0
        for n in range(FOX_PARTS):
            pcq_t[base_t + HEAD_DIM + n, n * N_HEADS + h] = 1.0
            pck[n * N_HEADS + h, base + HEAD_DIM + FOX_PARTS + n] = -1.0
            ones_t[0, base_t + HEAD_DIM + FOX_PARTS + n, :] = 1.0
            ones_k[0, base + HEAD_DIM + n] = 1.0
        ones_t[1, base_t + HEAD_DIM, :] = 1.0
    bf = lambda a: jnp.asarray(a, BF16)
    return bf(pk), bf(pck), jnp.asarray(ones_k), bf(np.stack([pq_t, pv_t])), bf(pcq_t), jnp.asarray(ones_t)


def _fox_kernel(q_ref, k_ref, v_ref, f_ref, fb_ref, pk_ref, pck_ref, onesk_ref, pqv_ref, pcq_ref, onest_ref,
                o_ref, kaug_ref, vaug_ref, m_ref, acc_ref, clast_ref, s_ref, s2_ref, p_ref):
    blk = FOX_BLOCK
    i = pl.program_id(1)

    @pl.when(i == 0)
    def _():
        clast_ref[...] = jnp.zeros_like(clast_ref)

    ls = _log_sigmoid(f_ref[...] + fb_ref[...])
    row = lax.broadcasted_iota(jnp.int32, (blk, blk), 0)
    col = lax.broadcasted_iota(jnp.int32, (blk, blk), 1)
    cq = _split_dot((col <= row).astype(BF16), ls, FOX_PARTS) + clast_ref[0:1, :]
    clast_ref[0:1, :] = cq[blk - 1:blk, :]

    widen = lambda a: jnp.concatenate([a] * (blk // LANES), axis=1)
    qa = _dot_nt(pqv_ref[0], q_ref[...].astype(BF16)) + widen(onest_ref[0])
    va = _dot_nt(pqv_ref[1], v_ref[...].astype(BF16)) + widen(onest_ref[1])
    ka = _dot(k_ref[...].astype(BF16), pk_ref[...]) + onesk_ref[0:1, :]
    lane = lax.broadcasted_iota(jnp.int32, (1, LANES), 1)
    pieces = _split_bf16(jnp.where(lane < N_HEADS, cq, 0.0), FOX_PARTS)
    packed = pieces[0].astype(F32)
    for n in range(1, FOX_PARTS):
        packed = packed + pltpu.roll(pieces[n].astype(F32), n * N_HEADS, axis=1)
    packed = packed.astype(BF16)
    qa = (qa + _dot_nt(pcq_ref[...], packed)).astype(BF16)
    ka = ka + _dot(packed, pck_ref[...])
    row0 = pl.multiple_of(i * blk, blk)
    for h in range(N_HEADS):
        kaug_ref[h, pl.ds(row0, blk), :] = ka[:, h * FOX_AUG:(h + 1) * FOX_AUG].astype(BF16)
        vaug_ref[h, :, pl.ds(row0, blk)] = va[h * FOX_FEAT:(h + 1) * FOX_FEAT, :].astype(BF16)

    key_le_query = row <= col

    def scores(j, buf, diagonal=False):
        off = pl.multiple_of(j * blk, blk)
        for h in range(N_HEADS):
            s = _dot(kaug_ref[h, pl.ds(off, blk), 0:FOX_FEAT], qa[h * FOX_FEAT:(h + 1) * FOX_FEAT, :])
            buf[h] = jnp.where(key_le_query, s, -jnp.inf) if diagonal else s

    def consume(j, buf):
        off = pl.multiple_of(j * blk, blk)
        m_new = [jnp.maximum(m_ref[h], jnp.max(buf[h], axis=0, keepdims=True)) for h in range(N_HEADS)]
        for h in range(N_HEADS):
            p_ref[h] = jnp.exp(buf[h] - m_new[h]).astype(BF16)
        for h in range(N_HEADS):
            pv = _dot(vaug_ref[h, :, pl.ds(off, blk)], p_ref[h])
            acc_ref[h] = jnp.exp(m_ref[h] - m_new[h]) * acc_ref[h] + pv
            m_ref[h] = m_new[h]

    m_ref[...] = jnp.full_like(m_ref, -jnp.inf)
    acc_ref[...] = jnp.zeros_like(acc_ref)
    scores(i, s_ref, diagonal=True)

    def body(j, carry):
        scores(2 * j, s2_ref)
        consume(jnp.where(j == 0, i, 2 * j - 1), s_ref)
        scores(jnp.minimum(2 * j + 1, i - 1), s_ref)
        consume(2 * j, s2_ref)
        return carry

    lax.fori_loop(0, (i + 1) // 2, body, 0)

    @pl.when(i % 2 == 0)
    def _():
        consume(jnp.maximum(i - 1, 0), s_ref)

    outs = []
    for h in range(N_HEADS):
        acc = acc_ref[h]
        outs.append(acc[0:HEAD_DIM, :] / acc[HEAD_DIM:HEAD_DIM + 1, :])
    o_ref[...] = jnp.concatenate(outs, axis=0).T


def _fox(zd, fb, layer, bsz, t):
    blk = FOX_BLOCK
    assert t % blk == 0
    nb = t // blk
    consts = _fox_placements()
    const = lambda a: pl.BlockSpec(a.shape, lambda b, i: (0,) * a.ndim)
    cur = lambda c: (lambda b, i: (b * nb + i, c))
    return pl.pallas_call(
        _fox_kernel,
        grid=(bsz, nb),
        in_specs=[
            pl.BlockSpec((blk, MIX_W), cur(0)),
            pl.BlockSpec((blk, MIX_W), cur(1)),
            pl.BlockSpec((blk, MIX_W), cur(2)),
            pl.BlockSpec((blk, LANES), cur(3 * MIX_W // LANES)),
            _lspec(fb, layer)] + [const(a) for a in consts],
        out_specs=pl.BlockSpec((blk, MIX_W), cur(0)),
        out_shape=jax.ShapeDtypeStruct((bsz * t, MIX_W), F32),
        scratch_shapes=[
            pltpu.VMEM((N_HEADS, t, FOX_AUG), BF16),
            pltpu.VMEM((N_HEADS, FOX_FEAT, t), BF16),
            pltpu.VMEM((N_HEADS, 1, blk), F32),
            pltpu.VMEM((N_HEADS, FOX_FEAT, blk), F32),
            pltpu.VMEM((SUBLANES, LANES), F32),
            pltpu.VMEM((N_HEADS, blk, blk), F32),
            pltpu.VMEM((N_HEADS, blk, blk), F32),
            pltpu.VMEM((N_HEADS, blk, blk), BF16),
        ],
        compiler_params=pltpu.CompilerParams(dimension_semantics=("arbitrary", "arbitrary")),
        name="fox",
    )(zd, zd, zd, zd, fb, *consts)


FFN_COL_CHUNK = 6 * MXU_COLS
FFN_ROW_PARTS = 2


def _outproj_ffn_kernel(x_ref, ya_ref, yb_ref, yc_ref, yd_ref, wo_ref, gmix_ref, gpre_ref, wg_ref, wu_ref, cw_ref,
                        cb_ref, wd_ref, gpost_ref, o_ref, tail_ref, *, tm, t):
    seq_start = (pl.program_id(0) * tm) % t == 0

    @pl.when(pl.program_id(0) == 0)
    def _():
        tail_ref[...] = jnp.zeros_like(tail_ref)

    rows = tm // FFN_ROW_PARTS
    gate_tails = {}

    def row_part(pi):
        rs = slice(pi * rows, (pi + 1) * rows)
        mixed = None
        for n, ref in enumerate((ya_ref, yb_ref, yc_ref, yd_ref)):
            part = _dot(ref[rs, :].astype(BF16), wo_ref[n * MIX_W:(n + 1) * MIX_W, :])
            mixed = part if mixed is None else mixed + part
        x = x_ref[rs, :] + mixed * _rms_scale(mixed) * gmix_ref[...]
        h = (x * _rms_scale(x) * gpre_ref[...]).astype(BF16)
        yield
        acc = None
        for c0 in range(0, D_FF, FFN_COL_CHUNK):
            cs = slice(c0, min(c0 + FFN_COL_CHUNK, D_FF))
            gate = _dot(h, wg_ref[:, cs])
            halo = jnp.where(seq_start, 0.0, tail_ref[:, cs]) if pi == 0 else gate_tails[pi - 1, c0]
            gate_tails[pi, c0] = gate[rows - SUBLANES:, :]
            if pi == FFN_ROW_PARTS - 1:
                tail_ref[:, cs] = gate_tails[pi, c0]
            conv = cb_ref[:, cs] + cw_ref[FFN_CONV - 1:FFN_CONV, cs] * gate
            for d in range(1, FFN_CONV):
                conv = conv + cw_ref[FFN_CONV - 1 - d:FFN_CONV - d, cs] * _shift_rows(gate, halo, d)
            f = jax.nn.gelu(conv, approximate=True) * _dot(h, wu_ref[:, cs])
            part = _dot(f.astype(BF16), wd_ref[cs, :])
            acc = part if acc is None else acc + part
            yield
        o_ref[rs, :] = x + acc * _rms_scale(acc) * gpost_ref[...]

    _trace_interleaved([(row_part(pi), 1) for pi in range(FFN_ROW_PARTS)])


def _outproj_ffn_vmem_bytes(tm):
    weights = 2 * (N_HEADS * MIX_W * D_MODEL + 2 * D_MODEL * D_FF + D_FF * D_MODEL)
    tiles = 2 * 4 * tm * (2 * D_MODEL + N_HEADS * MIX_W)
    temps = 4 * tm * (6 * FFN_COL_CHUNK + 3 * D_MODEL)
    return weights + tiles + temps


def _outproj_ffn(x2, ys, w_out, g_mix, p, layer, t, tm):
    rows = x2.shape[0]
    once = dict(pipeline_mode=pl.Buffered(1))
    gate_up = p["w_gate_up"]
    params = [w_out, g_mix, p["g_pre"], gate_up, gate_up, p["conv_w"], p["conv_b"], p["w_down"], p["g_post"]]
    specs = [_lspec(a, layer, **once) for a in params]
    specs[3] = _lspec(gate_up, layer, block=(D_MODEL, D_FF), at=(0, 0), **once)
    specs[4] = _lspec(gate_up, layer, block=(D_MODEL, D_FF), at=(0, 1), **once)
    return pl.pallas_call(
        functools.partial(_outproj_ffn_kernel, tm=tm, t=t),
        grid=(rows // tm,),
        in_specs=[pl.BlockSpec((tm, D_MODEL), lambda i: (i, 0))]
        + [pl.BlockSpec((tm, MIX_W), lambda i: (i, 0))] * len(ys)
        + specs,
        out_specs=pl.BlockSpec((tm, D_MODEL), lambda i: (i, 0)),
        out_shape=jax.ShapeDtypeStruct((rows, D_MODEL), F32),
        scratch_shapes=[pltpu.VMEM((SUBLANES, D_FF), F32)],
        compiler_params=pltpu.CompilerParams(dimension_semantics=("arbitrary",),
                                             vmem_limit_bytes=_outproj_ffn_vmem_bytes(tm)),
        name="outproj_ffn",
    )(x2, *ys, *params)


def _in_weights(w_in, l):
    wa_w = ZA_W
    wb_w = 3 * MIX_W + 2 * N_HEADS
    wc_w = ZC_W
    wi = jnp.transpose(w_in, (2, 0, 1))[:, l, :].astype(BF16)
    pad_rows = lambda a, n: jnp.pad(a, ((0, n - a.shape[0]), (0, 0)))
    groups = [wi[:wa_w],
              pad_rows(wi[wa_w:wa_w + wb_w], ZB_W),
              wi[wa_w + wb_w:wa_w + wb_w + wc_w],
              pad_rows(wi[wa_w + wb_w + wc_w:], ZD_W)]
    return jnp.concatenate(groups, axis=0)


def _rows(a):
    return a.reshape(a.shape[0], 1, -1).astype(F32)


def _lspec(a, layer, block=None, at=None, **kw):
    block = tuple(a.shape[1:]) if block is None else block
    at = (0,) * len(block) if at is None else at
    return pl.BlockSpec((None,) + block, lambda *_: (layer,) + at, **kw)


def _stacked_params(w_out, norm_mix_pre, norm_mix_post, norm_ffn_pre, norm_ffn_post,
                    rwkv_mu, rwkv_w0, rwkv_w_up, rwkv_a0, rwkv_a_up, rwkv_g_up, rwkv_k_k, rwkv_k_a,
                    rwkv_r_k, rwkv_ln_w, rwkv_ln_b, mlstm_conv_w, mlstm_conv_b, mlstm_b_i, mlstm_b_f,
                    mlstm_norm, swa_sinks, fox_b_f, ffn_w_up, ffn_conv_w, ffn_conv_b, ffn_w_down):
    pad_lanes = lambda a: jnp.pad(a, ((0, 0), (0, 0), (0, LANES - a.shape[-1])))
    return {
        "g_mix_pre": _rows(norm_mix_pre),
        "g_mix_post": _rows(norm_mix_post),
        "w_out": w_out.astype(BF16),
        "rwkv": {"mu": _rows(rwkv_mu), "w0": _rows(rwkv_w0), "w_up": rwkv_w_up.astype(BF16),
                 "a0": _rows(rwkv_a0), "a_up": rwkv_a_up.astype(BF16), "g_up": rwkv_g_up.astype(BF16),
                 "k_k": _rows(rwkv_k_k), "k_a": _rows(rwkv_k_a), "r_k": _rows(rwkv_r_k),
                 "ln_w": _rows(rwkv_ln_w), "ln_b": _rows(rwkv_ln_b)},
        "mlstm": {"conv_w": mlstm_conv_w, "conv_b": _rows(mlstm_conv_b),
                  "gate_b": pad_lanes(_rows(jnp.concatenate([mlstm_b_i, mlstm_b_f], axis=-1))),
                  "norm_g": _rows(mlstm_norm)},
        "swa_sinks": swa_sinks,
        "fox_b": pad_lanes(_rows(fox_b_f)),
        "ffn": {"g_pre": _rows(norm_ffn_pre), "w_gate_up": ffn_w_up.astype(BF16), "conv_w": ffn_conv_w,
                "conv_b": _rows(ffn_conv_b), "w_down": ffn_w_down.astype(BF16), "g_post": _rows(norm_ffn_post)},
    }


def _tiles(t):
    return min(512, t), min(512, t), min(512, t)


def kernel(x, w_in, w_out, norm_mix_pre, norm_mix_post, norm_ffn_pre, norm_ffn_post, rwkv_mu, rwkv_w0, rwkv_w_up, rwkv_a0, rwkv_a_up, rwkv_g_up, rwkv_k_k, rwkv_k_a, rwkv_r_k, rwkv_ln_w, rwkv_ln_b, mlstm_conv_w, mlstm_conv_b, mlstm_b_i, mlstm_b_f, mlstm_norm, swa_sinks, fox_b_f, rel_bias, ffn_w_up, ffn_conv_w, ffn_conv_b, ffn_w_down):
    bsz, t, d = x.shape
    assert d == D_MODEL and t % ATTN_BLOCK == 0
    tm, tc, tf = _tiles(t)
    x2 = x.reshape(bsz * t, d)
    p = _stacked_params(w_out, norm_mix_pre, norm_mix_post, norm_ffn_pre, norm_ffn_post,
                        rwkv_mu, rwkv_w0, rwkv_w_up, rwkv_a0, rwkv_a_up, rwkv_g_up, rwkv_k_k, rwkv_k_a,
                        rwkv_r_k, rwkv_ln_w, rwkv_ln_b, mlstm_conv_w, mlstm_conv_b, mlstm_b_i, mlstm_b_f,
                        mlstm_norm, swa_sinks, fox_b_f, ffn_w_up, ffn_conv_w, ffn_conv_b, ffn_w_down)
    for l in range(w_in.shape[0]):
        za, zb, zc, zd = _inproj(x2, p["g_mix_pre"], _in_weights(w_in, l), l, tm)
        ya, yb, yc = _local_mixers(za, zb, zc, p["rwkv"], p["mlstm"], rel_bias, p["swa_sinks"], l, bsz, t, tc)
        yd = _fox(zd, p["fox_b"], l, bsz, t)
        x2 = _outproj_ffn(x2, (ya, yb, yc, yd), p["w_out"], p["g_mix_post"], p["ffn"], l, t, tf)
    return x2.reshape(bsz, t, d)
```
